```python
import math
import jax, jax.numpy as jnp
from jax import lax
import numpy as np

D_MODEL = 4096
BATCH = 4
SEQ = 2048
DEPTH = 2
DEC_BATCH = 8
DEC_SEQ = 4
PAST_LEN = 16384
PAGE_SIZE = 128

HGRN_HEADS = 16
HGRN_DK = 128
HGRN_DV = 128
HGRN_QK = HGRN_HEADS * HGRN_DK
HGRN_WIDTH = HGRN_HEADS * HGRN_DV
HGRN_CHUNK = 64
NSA_HEADS = 16
NSA_KV = 4
NSA_GROUP = NSA_HEADS // NSA_KV
HEAD_DIM = 128
NSA_WIDTH = NSA_HEADS * HEAD_DIM
KV_WIDTH = NSA_KV * HEAD_DIM
BLOCK = 64
N_SEL = 16
N_LOCAL = 2
WINDOW = 512
Q_BLOCK = 128
SEL_Q_CHUNK = 16
FORCE_BONUS = float(NSA_GROUP + 1)
SCALE = HEAD_DIM ** -0.5
D_FF = 4 * D_MODEL
EPS = 1e-6
MASK_VALUE = -1e30
IN_SIZES = (HGRN_QK, HGRN_QK, HGRN_WIDTH, HGRN_WIDTH, NSA_WIDTH,
            KV_WIDTH, KV_WIDTH, KV_WIDTH, KV_WIDTH, KV_WIDTH, KV_WIDTH,
            3 * NSA_HEADS, 2 * D_MODEL)
N_IN = sum(IN_SIZES)

kernel_name = "hgrn2_nsa_parallel_hybrid_step"


def _rmsnorm(x, w):
    xf = x.astype(jnp.float32)
    y = xf * lax.rsqrt(jnp.mean(xf * xf, axis=-1, keepdims=True) + EPS)
    return (y * w.astype(jnp.float32)).astype(x.dtype)


def _split_in(z):
    idx, acc = [], 0
    for s in IN_SIZES[:-1]:
        acc += s
        idx.append(acc)
    return jnp.split(z, idx, axis=-1)


def _hgrn_lower_bounds(lb_logits):
    p = jax.nn.softmax(lb_logits.astype(jnp.float32), axis=0)
    c = jnp.cumsum(p, axis=0)
    return c - c[0:1]


def _hgrn2_recurrence(q, k, v, log_f, s0):
    B, L = q.shape[:2]
    C = HGRN_CHUNK if L % HGRN_CHUNK == 0 else L
    n = L // C

    def chunks(a):
        a = a.astype(jnp.float32).reshape((B, n, C) + a.shape[2:])
        return jnp.moveaxis(a, 1, 0)

    causal = jnp.tril(jnp.ones((C, C), dtype=bool))[None, :, :, None, None]

    def step(S, inp):
        qc, kc, vc, gc = inp
        b = jnp.cumsum(gc, axis=1)
        o_inter = jnp.einsum('bthk,bhkv->bthv', qc * jnp.exp(b), S)
        decay = jnp.exp(jnp.where(causal, b[:, :, None] - b[:, None, :], -jnp.inf))
        a = jnp.einsum('bthk,bshk,btshk->bhts', qc, kc, decay)
        o_intra = jnp.einsum('bhts,bshv->bthv', a, vc)
        b_end = b[:, -1]
        S_new = jnp.exp(b_end)[..., None] * S + jnp.einsum(
            'bshk,bshv->bhkv', kc * jnp.exp(b_end[:, None] - b), vc)
        return S_new, o_inter + o_intra

    S, o = lax.scan(step, s0.astype(jnp.float32), (chunks(q), chunks(k), chunks(v), chunks(log_f)))
    o = jnp.moveaxis(o, 0, 1).reshape((B, L) + o.shape[3:])
    return o, S


def _hgrn2_branch(a_q, a_f, a_i, a_g, lb, norm_w, s0):
    B, L = a_q.shape[:2]
    heads = lambda t: t.reshape(B, L, HGRN_HEADS, -1)
    f = lb + (1.0 - lb) * jax.nn.sigmoid(a_f.astype(jnp.float32))
    o, S = _hgrn2_recurrence(heads(jax.nn.silu(a_q)), heads(1.0 - f), heads(a_i), heads(jnp.log(f)), s0)
    o = _rmsnorm(o, norm_w) * jax.nn.silu(heads(a_g).astype(jnp.float32))
    return o.reshape(B, L, HGRN_WIDTH).astype(a_q.dtype), S


def _nsa_compressed(q, q_pos, k_c, v_c, pos_logits, kn_w):
    B, T = k_c.shape[:2]
    nf = T // BLOCK
    w = jax.nn.softmax(pos_logits.astype(jnp.float32))

    def pool(t):
        t = t[:, :nf * BLOCK].astype(jnp.float32).reshape(B, nf, BLOCK, NSA_KV, HEAD_DIM)
        return jnp.einsum('bnjgd,j->bngd', t, w)

    kc = _rmsnorm(pool(k_c), kn_w)
    vc = pool(v_c)
    s = jnp.einsum('bqghd,bngd->bgqhn', q.astype(jnp.float32), kc) * SCALE
    ready = ((jnp.arange(nf) + 1) * BLOCK - 1)[None, :] <= q_pos[:, None]
    m = ready[None, None, :, None, :]
    p = jax.nn.softmax(jnp.where(m, s, MASK_VALUE), axis=-1) * m.astype(jnp.float32)
    o = jnp.einsum('bgqhn,bngd->bqghd', p, vc)
    n_cand = -(-T // BLOCK)
    imp = jnp.pad(jnp.sum(p, axis=3), ((0, 0), (0, 0), (0, 0), (0, n_cand - nf)))
    cand = jnp.arange(n_cand)[None, :]
    cur = (q_pos // BLOCK)[:, None]
    valid = cand <= cur
    forced = (cand == 0) | (cand > cur - N_LOCAL)
    score = jnp.where(valid, imp + jnp.where(forced, FORCE_BONUS, 0.0), -jnp.inf)
    top_s, top_i = lax.top_k(score, min(N_SEL, n_cand))
    return o, top_i, jnp.isfinite(top_s)


def _nsa_selected(q, q_pos, top_i, sel_ok, k_s, v_s):
    B, T = k_s.shape[:2]
    n_cand = -(-T // BLOCK)

    def blocks(t):
        t = jnp.pad(t, ((0, 0), (0, n_cand * BLOCK - T), (0, 0), (0, 0)))
        return jnp.moveaxis(t.reshape(B, n_cand, BLOCK, NSA_KV, HEAD_DIM), 3, 1)

    kb, vb = blocks(k_s), blocks(v_s)
    gather = jax.vmap(jax.vmap(lambda t, ix: t[ix]))

    def attend(qc, pc, ic, okc):
        kg = gather(kb, ic).astype(jnp.float32)
        vg = gather(vb, ic).astype(jnp.float32)
        s = jnp.einsum('bqghd,bgqkjd->bgqhkj', qc.astype(jnp.float32), kg) * SCALE
        kpos = ic[..., None] * BLOCK + jnp.arange(BLOCK)
        m = (okc[..., None] & (kpos <= pc[None, None, :, None, None]))[:, :, :, None]
        s = jnp.where(m, s, MASK_VALUE)
        shp = s.shape
        p = jax.nn.softmax(s.reshape(shp[:4] + (-1,)), axis=-1).reshape(shp)
        return jnp.einsum('bgqhkj,bgqkjd->bqghd', p, vg)

    Q = q.shape[1]
    if Q > SEL_Q_CHUNK and Q % SEL_Q_CHUNK == 0:
        n = Q // SEL_Q_CHUNK
        qs = jnp.moveaxis(q.reshape((B, n, SEL_Q_CHUNK) + q.shape[2:]), 1, 0)
        ps = q_pos.reshape(n, SEL_Q_CHUNK)
        split = lambda t: jnp.moveaxis(t.reshape(t.shape[:2] + (n, SEL_Q_CHUNK) + t.shape[3:]), 2, 0)
        o = lax.map(lambda a: attend(*a), (qs, ps, split(top_i), split(sel_ok)))
        return jnp.moveaxis(o, 0, 1).reshape((B, Q) + o.shape[3:])
    return attend(q, q_pos, top_i, sel_ok)


def _window_attend(q, q_pos, k, v, k_pos):
    s = jnp.einsum('bqghd,btgd->bgqht', q.astype(jnp.float32), k.astype(jnp.float32)) * SCALE
    d = q_pos[:, None] - k_pos[None, :]
    m = ((d >= 0) & (d < WINDOW) & (k_pos[None, :] >= 0))[None, None, :, None, :]
    p = jax.nn.softmax(jnp.where(m, s, MASK_VALUE), axis=-1)
    return jnp.einsum('bgqht,btgd->bqghd', p, v.astype(jnp.float32))


def _window_prompt(q, k, v):
    B, L = q.shape[:2]
    qb = min(Q_BLOCK, L)
    n = L // qb
    idx = jnp.arange(n)[:, None] * qb + jnp.arange(qb + WINDOW)[None, :]
    band = lambda t: jnp.pad(t, ((0, 0), (WINDOW, 0), (0, 0), (0, 0)))[:, idx]
    o = jax.vmap(_window_attend, in_axes=(1, 0, 1, 1, 0), out_axes=1)(
        q.reshape((B, n, qb) + q.shape[2:]), jnp.arange(L).reshape(n, qb), band(k), band(v), idx - WINDOW)
    return o.reshape((B, L) + q.shape[2:])


def _layer(x, past_kv, past_win, s0, lb, norm1_w, w_in, hgrn_norm_w, q_norm_w, k_norm_w,
           cmp_pos_logits, w_branch_a, w_branch_b, w_out, norm2_w, w_up, w_down):
    B, L, _ = x.shape
    P = 0 if past_kv is None else past_kv.shape[1]
    q_pos = P + jnp.arange(L)
    h = _rmsnorm(x, norm1_w)
    (a_q, a_f, a_i, a_g, b_q, k_c, v_c, k_s, v_s, k_w, v_w, b_gate, m_gate) = _split_in(h @ w_in)
    y_a, s_new = _hgrn2_branch(a_q, a_f, a_i, a_g, lb, hgrn_norm_w, s0)
    kv = lambda t: t.reshape(B, L, NSA_KV, HEAD_DIM)
    q = _rmsnorm(b_q.reshape(B, L, NSA_KV, NSA_GROUP, HEAD_DIM), q_norm_w)
    k_s = _rmsnorm(kv(k_s), k_norm_w[1])
    k_w = _rmsnorm(kv(k_w), k_norm_w[2])
    new_kv = jnp.stack([kv(k_c), kv(v_c), k_s, kv(v_s)], axis=2)
    new_win = jnp.stack([k_w, kv(v_w)], axis=2)
    full = new_kv if past_kv is None else jnp.concatenate([past_kv.astype(x.dtype), new_kv], axis=1)
    o_cmp, top_i, sel_ok = _nsa_compressed(q, q_pos, full[:, :, 0], full[:, :, 1], cmp_pos_logits, k_norm_w[0])
    o_sel = _nsa_selected(q, q_pos, top_i, sel_ok, full[:, :, 2], full[:, :, 3])
    if past_win is None:
        o_win = _window_prompt(q, new_win[:, :, 0], new_win[:, :, 1])
        win_all = new_win
    else:
        win_all = jnp.concatenate([past_win.astype(x.dtype), new_win], axis=1)
        k_pos = P - past_win.shape[1] + jnp.arange(win_all.shape[1])
        o_win = _window_attend(q, q_pos, win_all[:, :, 0], win_all[:, :, 1], k_pos)
    win_state = win_all[:, win_all.shape[1] - min(WINDOW, win_all.shape[1]):]
    g = jax.nn.sigmoid(b_gate.astype(jnp.float32)).reshape(B, L, NSA_KV, NSA_GROUP, 3)
    y_b = (g[..., 0:1] * o_cmp + g[..., 1:2] * o_sel + g[..., 2:3] * o_win).reshape(B, L, NSA_WIDTH).astype(x.dtype)
    mg = jax.nn.sigmoid(m_gate.astype(jnp.float32)).reshape(B, L, 2, D_MODEL)
    mix = (mg[:, :, 0] * (y_a @ w_branch_a) + mg[:, :, 1] * (y_b @ w_branch_b)).astype(x.dtype)
    x = x + mix @ w_out
    h2 = _rmsnorm(x, norm2_w)
    x = x + jnp.square(jax.nn.relu(h2 @ w_up)) @ w_down
    return x, new_kv, win_state, s_new


def setup_inputs(seed: int = 0) -> dict:
    key = jax.random.key(seed)
    ks = jax.random.split(key, 20)
    n_pages = PAST_LEN // PAGE_SIZE
    n_used = DEC_BATCH * n_pages
    n_pool = n_used + -(-n_used // 4)
    perm = jax.random.permutation(ks[0], n_pool)
    page_table = perm[:n_used].reshape(DEC_BATCH, n_pages).astype(jnp.int32)
    w_buf = min(WINDOW, PAST_LEN)
    nrm = lambda k, shape, scale: jax.random.normal(k, shape, jnp.float32) * scale
    gain = lambda k, shape: 1.0 + 0.1 * jax.random.normal(k, shape, jnp.float32)
    return {
        "x_prompt": nrm(ks[1], (BATCH, SEQ, D_MODEL), 1.0),
        "x_sample": nrm(ks[2], (DEC_BATCH, DEC_SEQ, D_MODEL), 1.0),
        "cache_kv": nrm(ks[3], (DEPTH, n_pool, PAGE_SIZE, 4, NSA_KV, HEAD_DIM), 1.0),
        "cache_win": nrm(ks[4], (DEPTH, DEC_BATCH, w_buf, 2, NSA_KV, HEAD_DIM), 1.0),
        "state_hgrn": nrm(ks[5], (DEPTH, DEC_BATCH, HGRN_HEADS, HGRN_DK, HGRN_DV), 0.5),
        "page_table": page_table,
        "norm1_w": gain(ks[6], (DEPTH, D_MODEL)),
        "w_in": nrm(ks[7], (DEPTH, D_MODEL, N_IN), D_MODEL ** -0.5),
        "hgrn_lb_logits": nrm(ks[8], (DEPTH, HGRN_QK), 1.0),
        "hgrn_norm_w": gain(ks[9], (DEPTH, HGRN_DV)),
        "q_norm_w": gain(ks[10], (DEPTH, HEAD_DIM)),
        "k_norm_w": gain(ks[11], (DEPTH, 3, HEAD_DIM)),
        "cmp_pos_logits": nrm(ks[12], (DEPTH, BLOCK), 0.5),
        "w_branch_a": nrm(ks[13], (DEPTH, HGRN_WIDTH, D_MODEL), HGRN_WIDTH ** -0.5),
        "w_branch_b": nrm(ks[14], (DEPTH, NSA_WIDTH, D_MODEL), NSA_WIDTH ** -0.5),
        "w_out": nrm(ks[15], (DEPTH, D_MODEL, D_MODEL), D_MODEL ** -0.5),
        "norm2_w": gain(ks[16], (DEPTH, D_MODEL)),
        "w_up": nrm(ks[17], (DEPTH, D_MODEL, D_FF), D_MODEL ** -0.5),
        "w_down": nrm(ks[18], (DEPTH, D_FF, D_MODEL), D_FF ** -0.5),
    }


def reference(x_prompt, x_sample, cache_kv, cache_win, state_hgrn, page_table, norm1_w, w_in,
              hgrn_lb_logits, hgrn_norm_w, q_norm_w, k_norm_w, cmp_pos_logits, w_branch_a,
              w_branch_b, w_out, norm2_w, w_up, w_down):
    lbs = _hgrn_lower_bounds(hgrn_lb_logits)
    dec_b, n_pages = page_table.shape
    y_p, y_s = x_prompt, x_sample
    kv_p, kv_s, win_p, win_s, st_p, st_s = [], [], [], [], [], []
    for l in range(DEPTH):
        w_l = (lbs[l], norm1_w[l], w_in[l], hgrn_norm_w[l], q_norm_w[l], k_norm_w[l], cmp_pos_logits[l],
               w_branch_a[l], w_branch_b[l], w_out[l], norm2_w[l], w_up[l], w_down[l])
        s0 = jnp.zeros((x_prompt.shape[0], HGRN_HEADS, HGRN_DK, HGRN_DV), jnp.float32)
        y_p, kv_new, win_new, st_new = _layer(y_p, None, None, s0, *w_l)
        kv_p.append(kv_new)
        win_p.append(win_new)
        st_p.append(st_new)
        past = cache_kv[l][page_table].reshape((dec_b, n_pages * PAGE_SIZE) + cache_kv.shape[3:])
        y_s, kv_new, win_new, st_new = _layer(y_s, past, cache_win[l], state_hgrn[l], *w_l)
        kv_s.append(kv_new)
        win_s.append(win_new)
        st_s.append(st_new)
    return (y_p, y_s, jnp.stack(kv_p), jnp.stack(kv_s), jnp.stack(win_p), jnp.stack(win_s),
            jnp.stack(st_p).astype(state_hgrn.dtype), jnp.stack(st_s).astype(state_hgrn.dtype))
```

```python
import functools
import math

import jax
import jax.numpy as jnp
from jax import lax
from jax.experimental import pallas as pl
from jax.experimental.pallas import tpu as pltpu

F32 = jnp.float32
BF16 = jnp.bfloat16

LANES = 128
SUBLANES = 8
VMEM_LIMIT = 56 * 1024 * 1024

HEAD_DIM = 128
NSA_KV = 4
NSA_GROUP = 4
BLOCK = 64
N_SEL = 16
N_LOCAL = 2
WINDOW = 512
FORCE_BONUS = float(NSA_GROUP + 1)
SCALE = HEAD_DIM ** -0.5
EPS = 1e-6
MASK_VALUE = -1e30
HGRN_CHUNK = 128


def _cparams(sem):
    return pltpu.CompilerParams(dimension_semantics=sem, vmem_limit_bytes=VMEM_LIMIT)


def _pick(n, prefs):
    for p in prefs:
        if n % p == 0:
            return p
    return n


def _rmsnorm_kernel(x_ref, w_ref, o_ref):
    x = x_ref[...]
    y = x * lax.rsqrt(jnp.mean(x * x, axis=-1, keepdims=True) + EPS)
    o_ref[...] = (y * w_ref[...]).astype(o_ref.dtype)


def rmsnorm_cast(x, w):
    M, D = x.shape
    tm = _pick(M, (256, 32))
    return pl.pallas_call(
        _rmsnorm_kernel,
        grid=(M // tm,),
        in_specs=[pl.BlockSpec((tm, D), lambda i: (i, 0)),
                  pl.BlockSpec((1, D), lambda i: (0, 0))],
        out_specs=pl.BlockSpec((tm, D), lambda i: (i, 0)),
        out_shape=jax.ShapeDtypeStruct((M, D), BF16),
        compiler_params=_cparams(("parallel",)),
        name="rmsnorm_cast",
    )(x, w.reshape(1, D))


def _mm_kernel(*refs, nk, n_tile, n_row, epilogue):
    x_ref, w_ref = refs[0], refs[1]
    tile_refs = refs[2:2 + n_tile]
    row_refs = refs[2 + n_tile:2 + n_tile + n_row]
    o_ref = refs[2 + n_tile + n_row]

    def finish(acc):
        extras = [r[...] for r in tile_refs] + [r[...] for r in row_refs]
        o_ref[...] = epilogue(acc, *extras).astype(o_ref.dtype)

    part = jnp.dot(x_ref[...], w_ref[...], preferred_element_type=F32)
    if nk == 1:
        finish(part)
    else:
        acc_ref = refs[3 + n_tile + n_row]
        k = pl.program_id(2)

        @pl.when(k == 0)
        def _():
            acc_ref[...] = part

        @pl.when(k > 0)
        def _():
            acc_ref[...] += part

        @pl.when(k == nk - 1)
        def _():
            finish(acc_ref[...])


def matmul(x, w, *, out_dtype, epilogue=None, tile_extras=(), row_extras=(),
           tm=None, tn=None, tk=None, name="matmul"):
    M, K = x.shape
    K2, N = w.shape
    assert K == K2
    tm = tm or _pick(M, (1024, 512, 256, 32))
    tn = tn or _pick(N, (1024, 512, 256, 128))
    tk = tk or (K if K <= 4096 else _pick(K, (2048,)))
    nk = K // tk
    assert M % tm == 0 and N % tn == 0 and K % tk == 0
    if epilogue is None:
        epilogue = lambda acc: acc
    in_specs = [pl.BlockSpec((tm, tk), lambda i, j, k: (i, k)),
                pl.BlockSpec((tk, tn), lambda i, j, k: (k, j))]
    in_specs += [pl.BlockSpec((tm, tn), lambda i, j, k: (i, j)) for _ in tile_extras]
    in_specs += [pl.BlockSpec((1, tn), lambda i, j, k: (0, j)) for _ in row_extras]
    scratch = [pltpu.VMEM((tm, tn), F32)] if nk > 1 else []
    return pl.pallas_call(
        functools.partial(_mm_kernel, nk=nk, n_tile=len(tile_extras), n_row=len(row_extras),
                          epilogue=epilogue),
        grid=(M // tm, N // tn, nk),
        in_specs=in_specs,
        out_specs=pl.BlockSpec((tm, tn), lambda i, j, k: (i, j)),
        out_shape=jax.ShapeDtypeStruct((M, N), out_dtype),
        scratch_shapes=scratch,
        compiler_params=_cparams(("parallel", "parallel", "arbitrary")),
        name=name,
    )(x, w, *tile_extras, *row_extras)


def _headnorm_epilogue(acc, nw, flag):
    outs = []
    for c in range(acc.shape[1] // HEAD_DIM):
        sl = slice(c * HEAD_DIM, (c + 1) * HEAD_DIM)
        z = acc[:, sl]
        zn = z * lax.rsqrt(jnp.mean(z * z, axis=-1, keepdims=True) + EPS) * nw[:, sl]
        outs.append(jnp.where(flag[:, sl] != 0.0, zn, z))
    return jnp.concatenate(outs, axis=1)


def _residual_epilogue(acc, res):
    return res + acc


def _relu2_epilogue(acc):
    r = jnp.maximum(acc, 0.0)
    return r * r


def _merge_kernel(ya_ref, yb_ref, wa_ref, wb_ref, g0_ref, g1_ref, o_ref):
    pa = jnp.dot(ya_ref[...], wa_ref[...], preferred_element_type=F32)
    pb = jnp.dot(yb_ref[...], wb_ref[...], preferred_element_type=F32)
    mix = jax.nn.sigmoid(g0_ref[...]) * pa + jax.nn.sigmoid(g1_ref[...]) * pb
    o_ref[...] = mix.astype(o_ref.dtype)


def gated_merge(ya, yb, wa, wb, m_gate):
    M, Ka = ya.shape
    Kb = yb.shape[1]
    D = wa.shape[1]
    tm = _pick(M, (1024, 512, 256, 32))
    tn = _pick(D, (512, 256, 128))
    nj = D // tn
    return pl.pallas_call(
        _merge_kernel,
        grid=(M // tm, nj),
        in_specs=[pl.BlockSpec((tm, Ka), lambda i, j: (i, 0)),
                  pl.BlockSpec((tm, Kb), lambda i, j: (i, 0)),
                  pl.BlockSpec((Ka, tn), lambda i, j: (0, j)),
                  pl.BlockSpec((Kb, tn), lambda i, j: (0, j)),
                  pl.BlockSpec((tm, tn), lambda i, j: (i, j)),
                  pl.BlockSpec((tm, tn), lambda i, j: (i, j + nj))],
        out_specs=pl.BlockSpec((tm, tn), lambda i, j: (i, j)),
        out_shape=jax.ShapeDtypeStruct((M, D), BF16),
        compiler_params=_cparams(("parallel", "parallel")),
        name="gated_merge",
    )(ya, yb, wa, wb, m_gate, m_gate)


def _silu(x):
    return x * jax.nn.sigmoid(x)


def _pair_total(cm, m, t_io):
    C = cm.shape[0]
    if m == 1:
        return jnp.where((t_io & 1) != 0, pltpu.roll(cm, 1, 0), cm)
    if m == 2:
        j = t_io & 3
        return jnp.where(j == 0, pltpu.roll(cm, C - 1, 0),
                         jnp.where(j == 1, cm,
                                   jnp.where(j == 2, pltpu.roll(cm, 1, 0), pltpu.roll(cm, 2, 0))))
    x3 = cm.reshape(C // (2 * m), 2 * m, cm.shape[1])
    return jnp.broadcast_to(x3[:, m - 1:m, :], x3.shape).reshape(cm.shape)


def _hgrn_chunk(aq, af, ai, ag, lb, nw, st, n_valid):
    C = aq.shape[0]
    t_io = lax.broadcasted_iota(jnp.int32, (C, LANES), 0)
    row_io = lax.broadcasted_iota(jnp.int32, (C, C), 0)
    col_io = lax.broadcasted_iota(jnp.int32, (C, C), 1)
    xor_io = row_io ^ col_io
    f = lb + (1.0 - lb) * jax.nn.sigmoid(af)
    if n_valid < C:
        f = jnp.where(t_io < n_valid, f, 1.0)
    g = jnp.log(f)
    kk = 1.0 - f
    qq = _silu(aq)
    v = ai.astype(BF16)
    a = jnp.zeros((C, C), F32)
    cm = g
    m = 1
    while m < C:
        tot = _pair_total(cm, m, t_io)
        odd = (t_io & m) != 0
        e = jnp.exp(jnp.where(odd, cm, tot - cm))
        qs = jnp.where(odd, qq * e, 0.0).astype(BF16)
        ks = jnp.where(odd, 0.0, kk * e).astype(BF16)
        p = lax.dot_general(qs, ks, (((1,), (1,)), ((), ())), preferred_element_type=F32)
        a = a + jnp.where(xor_io < 2 * m, p, 0.0)
        cm = cm + jnp.where(odd, tot, 0.0)
        m *= 2
    b = cm
    d = jnp.sum(qq * kk, axis=-1, keepdims=True)
    a = jnp.where(row_io == col_io, d, a)
    o_intra = jnp.dot(a.astype(BF16), v, preferred_element_type=F32)
    qe = (qq * jnp.exp(b)).astype(BF16)
    o_inter = lax.dot_general(qe, st.astype(BF16), (((1,), (1,)), ((), ())), preferred_element_type=F32)
    b_end = b[C - 1:C, :]
    ku = (kk * jnp.exp(b_end - b)).astype(BF16)
    ut = lax.dot_general(v, ku, (((0,), (0,)), ((), ())), preferred_element_type=F32)
    st_new = st * jnp.exp(b_end) + ut
    o = o_inter + o_intra
    on = o * lax.rsqrt(jnp.mean(o * o, axis=-1, keepdims=True) + EPS) * nw
    return on * _silu(ag), st_new


def _hgrn_kernel(*refs, n_sub, chunk, n_valid, has_s0):
    aq_ref, af_ref, ai_ref, ag_ref, lb_ref, nw_ref = refs[:6]
    s0_ref = refs[6] if has_s0 else None
    y_ref, s_ref, st_ref = refs[6 + has_s0:]
    c = pl.program_id(2)

    @pl.when(c == 0)
    def _():
        st_ref[...] = s0_ref[...].T if has_s0 else jnp.zeros_like(st_ref)

    lb = lb_ref[...]
    nw = nw_ref[...]
    for j in range(n_sub):
        rows = pl.ds(j * chunk, chunk)
        y, st_new = _hgrn_chunk(aq_ref[rows, :], af_ref[rows, :], ai_ref[rows, :], ag_ref[rows, :],
                                lb, nw, st_ref[...], n_valid)
        st_ref[...] = st_new
        y_ref[rows, :] = y.astype(y_ref.dtype)

    @pl.when(c == pl.num_programs(2) - 1)
    def _():
        s_ref[...] = st_ref[...].T


def hgrn(za, lb, nw, s0, B, L, H, chunk, n_valid=None):
    dk = LANES
    assert L % chunk == 0 and (n_valid is None or L == chunk)
    n_sub = _pick(L // chunk, (4, 2, 1))
    tc = n_sub * chunk
    nc = L // tc
    row = lambda off: pl.BlockSpec((tc, dk), lambda b, h, c: (b * nc + c, off + h))
    state = pl.BlockSpec((None, None, dk, dk), lambda b, h, c: (b, h, 0, 0))
    has_s0 = s0 is not None
    return pl.pallas_call(
        functools.partial(_hgrn_kernel, n_sub=n_sub, chunk=chunk,
                          n_valid=chunk if n_valid is None else n_valid, has_s0=has_s0),
        grid=(B, H, nc),
        in_specs=[row(0), row(H), row(2 * H), row(3 * H),
                  pl.BlockSpec((1, dk), lambda b, h, c: (0, h)),
                  pl.BlockSpec((1, dk), lambda b, h, c: (0, 0))] + ([state] if has_s0 else []),
        out_specs=[pl.BlockSpec((tc, dk), lambda b, h, c: (b * nc + c, h)), state],
        out_shape=[jax.ShapeDtypeStruct((B * L, H * dk), BF16),
                   jax.ShapeDtypeStruct((B, H, dk, dk), F32)],
        scratch_shapes=[pltpu.VMEM((dk, dk), F32)],
        compiler_params=_cparams(("parallel", "parallel", "arbitrary")),
        name="hgrn",
    )(za, za, za, za, lb.reshape(1, -1), nw.reshape(1, dk), *([s0] if has_s0 else []))


def hgrn_prompt(za, lb, nw, B, L, H):
    return hgrn(za, lb, nw, None, B, L, H, HGRN_CHUNK)


def _pool_rows(x, wl, nw):
    wl = wl - jnp.max(wl, axis=0, keepdims=True)
    e = jnp.exp(wl)
    w = e / jnp.sum(e, axis=0, keepdims=True)
    R = x.shape[0] // BLOCK
    pooled = jnp.sum(x.reshape(R, BLOCK, x.shape[1]) * w[None], axis=1)
    half = x.shape[1] // 2
    outs = []
    for c in range(x.shape[1] // HEAD_DIM):
        z = pooled[:, c * HEAD_DIM:(c + 1) * HEAD_DIM]
        if c * HEAD_DIM < half:
            z = z * lax.rsqrt(jnp.mean(z * z, axis=-1, keepdims=True) + EPS) * nw
        outs.append(z)
    return jnp.concatenate(outs, axis=1)


def _pool_kernel(x_ref, wl_ref, nw_ref, o_ref):
    o_ref[...] = _pool_rows(x_ref[...], wl_ref[...], nw_ref[...])


def pool_prompt(kv, pos_logits, kn_w):
    M = kv.shape[0]
    kvw2 = kv.shape[1] // 2
    rb = SUBLANES * BLOCK
    assert M % rb == 0
    return pl.pallas_call(
        _pool_kernel,
        grid=(M // rb,),
        in_specs=[pl.BlockSpec((rb, kvw2), lambda i: (i, 0)),
                  pl.BlockSpec((BLOCK, 1), lambda i: (0, 0)),
                  pl.BlockSpec((1, HEAD_DIM), lambda i: (0, 0))],
        out_specs=pl.BlockSpec((SUBLANES, kvw2), lambda i: (i, 0)),
        out_shape=jax.ShapeDtypeStruct((M // BLOCK, kvw2), F32),
        compiler_params=_cparams(("parallel",)),
        name="pool_prompt",
    )(kv, pos_logits.reshape(BLOCK, 1), kn_w.reshape(1, HEAD_DIM))


NSA_TQ = 128


def _select_blocks(imp, qpos, n_cand, score_ref):
    cand = lax.broadcasted_iota(jnp.int32, imp.shape, 0)
    cur = qpos // BLOCK
    valid = cand <= cur
    forced = (cand == 0) | (cand > cur - N_LOCAL)
    score = jnp.where(valid, imp + jnp.where(forced, FORCE_BONUS, 0.0), -jnp.inf)
    score_ref[...] = score

    def body(m, rank):
        row = score_ref[pl.ds(m, 1), :]
        beats = (row > score) | ((row == score) & (cand > m))
        return rank + jnp.where(beats, 1.0, 0.0)

    rank = lax.fori_loop(0, n_cand, body, jnp.zeros(imp.shape, F32), unroll=n_cand <= 32)
    return valid & (rank < float(N_SEL))


def _flash_step(k, v, qt_b, mask, m_ref, l_ref, acc_ref):
    s = jnp.dot(k.astype(BF16), qt_b, preferred_element_type=F32) * SCALE
    s = jnp.where(mask, s, MASK_VALUE)
    m_old = m_ref[...]
    m_new = jnp.maximum(m_old, jnp.max(s, axis=0, keepdims=True))
    alpha = jnp.exp(m_old - m_new)
    p = jnp.exp(s - m_new)
    l_ref[...] = alpha * l_ref[...] + jnp.sum(p, axis=0, keepdims=True)
    pv = lax.dot_general(v.astype(BF16), p.astype(BF16), (((0,), (0,)), ((), ())),
                         preferred_element_type=F32)
    acc_ref[...] = alpha * acc_ref[...] + pv
    m_ref[...] = m_new


def _flash_init(m_ref, l_ref, acc_ref):
    m_ref[...] = jnp.full(m_ref.shape, MASK_VALUE, F32)
    l_ref[...] = jnp.zeros(l_ref.shape, F32)
    acc_ref[...] = jnp.zeros(acc_ref.shape, F32)


def _nsa_prompt_kernel(q_ref, ks_ref, vs_ref, kw_ref, vw_ref, kc_ref, vc_ref, g_ref, o_ref,
                       sel_ref, score_ref, m_ref, l_ref, acc_ref, *, nb):
    tq = NSA_TQ
    qt = pl.program_id(2)
    nh = NSA_GROUP
    q = q_ref[...]
    qT = jnp.concatenate([q[:, h * HEAD_DIM:(h + 1) * HEAD_DIM].T for h in range(nh)], axis=1)
    qt_b = qT.astype(BF16)
    lane = lax.broadcasted_iota(jnp.int32, (1, nh * tq), 1)
    qpos = qt * tq + (lane & (tq - 1))

    s = jnp.dot(kc_ref[...], qT, preferred_element_type=F32, precision=lax.Precision.HIGHEST) * SCALE
    n_io = lax.broadcasted_iota(jnp.int32, s.shape, 0)
    ready = ((n_io + 1) * BLOCK - 1) <= qpos
    s = jnp.where(ready, s, MASK_VALUE)
    e = jnp.exp(s - jnp.max(s, axis=0, keepdims=True))
    p = e / jnp.sum(e, axis=0, keepdims=True) * jnp.where(ready, 1.0, 0.0)
    o_cmp = lax.dot_general(vc_ref[...].astype(BF16), p.astype(BF16), (((0,), (0,)), ((), ())),
                            preferred_element_type=F32)
    imp = p[:, 0:tq]
    for h in range(1, nh):
        imp = imp + p[:, h * tq:(h + 1) * tq]
    sel = _select_blocks(imp, qpos[:, 0:tq], nb, score_ref)
    sel_f = jnp.where(sel, 1.0, 0.0)
    bpt = tq // BLOCK
    for n in range(nb):
        sel_ref[n // bpt, n % bpt:n % bpt + 1, :] = sel_f[n:n + 1, :]

    krow = lax.broadcasted_iota(jnp.int32, (tq, nh * tq), 0)

    _flash_init(m_ref, l_ref, acc_ref)

    def sel_body(kt, carry):
        rows = pl.ds(pl.multiple_of(kt * tq, tq), tq)
        blk = sel_ref[kt]
        bm = blk[0:1, :]
        for j in range(1, bpt):
            bm = jnp.where(krow[:, 0:tq] >= j * BLOCK, blk[j:j + 1, :], bm)
        bm = jnp.concatenate([jnp.broadcast_to(bm, (tq, tq))] * nh, axis=1)
        mask = (bm > 0.0) & ((kt * tq + krow) <= qpos)
        _flash_step(ks_ref[rows, :], vs_ref[rows, :], qt_b, mask, m_ref, l_ref, acc_ref)
        return carry

    lax.fori_loop(0, qt + 1, sel_body, 0)
    o_sel = acc_ref[...] * (1.0 / l_ref[...])

    _flash_init(m_ref, l_ref, acc_ref)

    def win_body(kt, carry):
        rows = pl.ds(pl.multiple_of(kt * tq, tq), tq)
        d = qpos - (kt * tq + krow)
        mask = (d >= 0) & (d < WINDOW)
        _flash_step(kw_ref[rows, :], vw_ref[rows, :], qt_b, mask, m_ref, l_ref, acc_ref)
        return carry

    lax.fori_loop(jnp.maximum(qt - WINDOW // tq, 0), qt + 1, win_body, 0)
    o_win = acc_ref[...] * (1.0 / l_ref[...])

    gT = jax.nn.sigmoid(g_ref[...]).T
    for h in range(nh):
        sl = slice(h * tq, (h + 1) * tq)
        y = (gT[3 * h:3 * h + 1, :] * o_cmp[:, sl] + gT[3 * h + 1:3 * h + 2, :] * o_sel[:, sl]
             + gT[3 * h + 2:3 * h + 3, :] * o_win[:, sl])
        o_ref[:, h * HEAD_DIM:(h + 1) * HEAD_DIM] = y.T.astype(o_ref.dtype)


def nsa_prompt(q, kv, win, pooled, gates, B, L):
    tq = NSA_TQ
    G, nh, hd = NSA_KV, NSA_GROUP, HEAD_DIM
    assert L % tq == 0 and tq % BLOCK == 0 and WINDOW % tq == 0
    nq = L // tq
    nb = L // BLOCK
    full = lambda off: pl.BlockSpec((L, hd), lambda b, g, t: (b, off + g))
    return pl.pallas_call(
        functools.partial(_nsa_prompt_kernel, nb=nb),
        grid=(B, G, nq),
        in_specs=[pl.BlockSpec((tq, nh * hd), lambda b, g, t: (b * nq + t, g)),
                  full(2 * G), full(3 * G),
                  pl.BlockSpec((L, hd), lambda b, g, t: (b, g)),
                  pl.BlockSpec((L, hd), lambda b, g, t: (b, G + g)),
                  pl.BlockSpec((nb, hd), lambda b, g, t: (b, g)),
                  pl.BlockSpec((nb, hd), lambda b, g, t: (b, G + g)),
                  pl.BlockSpec((tq, LANES), lambda b, g, t: (b * nq + t, g))],
        out_specs=pl.BlockSpec((tq, nh * hd), lambda b, g, t: (b * nq + t, g)),
        out_shape=jax.ShapeDtypeStruct((B * L, G * nh * hd), BF16),
        scratch_shapes=[pltpu.VMEM((nb * BLOCK // tq, tq // BLOCK, tq), F32),
                        pltpu.VMEM((nb, tq), F32),
                        pltpu.VMEM((1, nh * tq), F32),
                        pltpu.VMEM((1, nh * tq), F32),
                        pltpu.VMEM((hd, nh * tq), F32)],
        compiler_params=_cparams(("parallel", "parallel", "arbitrary")),
        name="nsa_prompt",
    )(q, kv, kv, win, win, pooled, pooled, gates)


DEC_PAGES_PER_STEP = 4
DEC_ROWS = 8


def _dec_qpos(past_len, n_new, shape):
    lane = lax.broadcasted_iota(jnp.int32, shape, len(shape) - 1)
    return past_len + lax.rem(lane & (LANES - 1), n_new)


def _pool_paged_kernel(pt_ref, *refs):
    del pt_ref
    npg = DEC_PAGES_PER_STEP
    wl_ref, nw_ref, o_ref = refs[npg:npg + 3]
    for i in range(npg):
        o_ref[i] = _pool_rows(refs[i][...], wl_ref[...], nw_ref[...])


def pool_paged(cache, layer, page_table, pos_logits, kn_w):
    B, n_pages = page_table.shape
    page = cache.shape[2]
    kvw2 = cache.shape[3] // 2
    npg = DEC_PAGES_PER_STEP
    assert n_pages % npg == 0 and page % BLOCK == 0
    bpp = page // BLOCK
    page_spec = lambda i: pl.BlockSpec((None, None, page, kvw2),
                                       lambda b, s, pt: (layer, pt[b, s * npg + i], 0, 0))
    out = pl.pallas_call(
        _pool_paged_kernel,
        grid_spec=pltpu.PrefetchScalarGridSpec(
            num_scalar_prefetch=1,
            grid=(B, n_pages // npg),
            in_specs=[page_spec(i) for i in range(npg)]
            + [pl.BlockSpec((BLOCK, 1), lambda b, s, pt: (0, 0)),
               pl.BlockSpec((1, HEAD_DIM), lambda b, s, pt: (0, 0))],
            out_specs=pl.BlockSpec((None, npg, bpp, kvw2), lambda b, s, pt: (b, s, 0, 0)),
        ),
        out_shape=jax.ShapeDtypeStruct((B, n_pages, bpp, kvw2), F32),
        compiler_params=_cparams(("parallel", "arbitrary")),
        name="pool_paged",
    )(page_table, *([cache] * npg), pos_logits.reshape(BLOCK, 1), kn_w.reshape(1, HEAD_DIM))
    return out.reshape(B, n_pages * bpp, kvw2)


def _nsa_dec_front_kernel(qbd_ref, pooled_ref, cwin_ref, nwin_ref, gl_ref, part_ref, sel_ref,
                          score_ref, m_ref, l_ref, acc_ref, *, past_len, n_new, n_cand):
    kvw = qbd_ref.shape[0]
    nl = qbd_ref.shape[1]
    qbd = qbd_ref[...]
    qbd_b = qbd.astype(BF16)
    qpos = _dec_qpos(past_len, n_new, (1, nl))

    nf = pooled_ref.shape[0]
    s = jnp.dot(pooled_ref[:, 0:kvw], qbd, preferred_element_type=F32,
                precision=lax.Precision.HIGHEST) * SCALE
    n_io = lax.broadcasted_iota(jnp.int32, s.shape, 0)
    ready = ((n_io + 1) * BLOCK - 1) <= qpos
    s = jnp.where(ready, s, MASK_VALUE)
    e = jnp.exp(s - jnp.max(s, axis=0, keepdims=True))
    p = e / jnp.sum(e, axis=0, keepdims=True) * jnp.where(ready, 1.0, 0.0)
    o_cmp = lax.dot_general(pooled_ref[:, kvw:2 * kvw].astype(BF16), p.astype(BF16),
                            (((0,), (0,)), ((), ())), preferred_element_type=F32)
    imp = p[:, 0:LANES]
    for h in range(1, NSA_GROUP):
        imp = imp + p[:, h * LANES:(h + 1) * LANES]
    imp = jnp.concatenate([imp, jnp.zeros((score_ref.shape[0] - nf, LANES), F32)], axis=0)
    sel = _select_blocks(imp, qpos[:, 0:LANES], n_cand, score_ref)
    sel_ref[...] = jnp.where(sel, 1.0, 0.0)

    _flash_init(m_ref, l_ref, acc_ref)
    w_rows = cwin_ref.shape[0]
    krow = lax.broadcasted_iota(jnp.int32, (w_rows, nl), 0)
    d = qpos - (past_len - w_rows + krow)
    _flash_step(cwin_ref[:, 0:kvw], cwin_ref[:, kvw:2 * kvw], qbd_b, (d >= 0) & (d < WINDOW),
                m_ref, l_ref, acc_ref)
    nrow = lax.broadcasted_iota(jnp.int32, (nwin_ref.shape[0], nl), 0)
    d = qpos - (past_len + nrow)
    _flash_step(nwin_ref[:, 0:kvw], nwin_ref[:, kvw:2 * kvw], qbd_b,
                (d >= 0) & (d < WINDOW) & (nrow < n_new), m_ref, l_ref, acc_ref)
    o_win = acc_ref[...] * (1.0 / l_ref[...])
    g = jax.nn.sigmoid(gl_ref[...])
    part_ref[...] = g[0:1, :] * o_cmp + g[2:3, :] * o_win


def _nsa_dec_sel_kernel(pt_ref, *refs, past_len, n_new, page):
    del pt_ref
    npg = DEC_PAGES_PER_STEP
    page_refs = refs[:npg]
    qbd_ref, sel_ref, sel_new_ref, nkv_ref, part_ref, gl_ref, y_ref, m_ref, l_ref, acc_ref = refs[npg:]
    kvw = qbd_ref.shape[0]
    nl = qbd_ref.shape[1]
    step = pl.program_id(1)
    qbd_b = qbd_ref[...].astype(BF16)
    qpos = _dec_qpos(past_len, n_new, (1, nl))

    @pl.when(step == 0)
    def _():
        _flash_init(m_ref, l_ref, acc_ref)

    def block_mask(blk, rows):
        bm = blk[0:1, :]
        for j in range(1, blk.shape[0]):
            bm = jnp.where(rows[:, 0:LANES] >= j * BLOCK, blk[j:j + 1, :], bm)
        return jnp.concatenate([jnp.broadcast_to(bm, (rows.shape[0], LANES))] * NSA_GROUP, axis=1) > 0.0

    krow = lax.broadcasted_iota(jnp.int32, (page, nl), 0)
    for i in range(npg):
        kpos = (step * npg + i) * page + krow
        mask = block_mask(sel_ref[i], krow) & (kpos <= qpos)
        _flash_step(page_refs[i][:, 0:kvw], page_refs[i][:, kvw:2 * kvw], qbd_b, mask, m_ref, l_ref, acc_ref)

    @pl.when(step == pl.num_programs(1) - 1)
    def _():
        nrow = lax.broadcasted_iota(jnp.int32, (nkv_ref.shape[0], nl), 0)
        mask = block_mask(sel_new_ref[0], nrow) & (past_len + nrow <= qpos) & (nrow < n_new)
        _flash_step(nkv_ref[:, 0:kvw], nkv_ref[:, kvw:2 * kvw], qbd_b, mask, m_ref, l_ref, acc_ref)
        g = jax.nn.sigmoid(gl_ref[...])
        yT = part_ref[...] + g[1:2, :] * (acc_ref[...] * (1.0 / l_ref[...]))
        y_ref[...] = jnp.zeros(y_ref.shape, y_ref.dtype)
        for grp in range(NSA_KV):
            for h in range(NSA_GROUP):
                blk = yT[grp * HEAD_DIM:(grp + 1) * HEAD_DIM, h * LANES:(h + 1) * LANES].T
                col = (grp * NSA_GROUP + h) * HEAD_DIM
                y_ref[0:n_new, col:col + HEAD_DIM] = blk[grp * n_new:(grp + 1) * n_new, :].astype(y_ref.dtype)


def nsa_decode(q, kv_new, win_new, gates, cache, layer, page_table, cache_win, pos_logits, kn_w, n_new):
    B, n_pages = page_table.shape
    page = cache.shape[2]
    G, nh, hd = NSA_KV, NSA_GROUP, HEAD_DIM
    kvw = G * hd
    nl = nh * LANES
    past_len = n_pages * page
    npg = DEC_PAGES_PER_STEP
    bpp = page // BLOCK
    n_cand = -(-(past_len + n_new) // BLOCK)
    n_rows = -(-(n_cand + bpp) // SUBLANES) * SUBLANES // bpp * bpp
    assert G * n_new <= LANES and n_new <= DEC_ROWS and n_new <= BLOCK and past_len % BLOCK == 0

    q5 = q.reshape(B, n_new, G, nh, hd)
    qt = jnp.transpose(q5, (0, 2, 4, 3, 1))
    qbd = qt[:, :, :, :, None, :] * jnp.eye(G, dtype=F32)[None, :, None, None, :, None]
    qbd = jnp.pad(qbd.reshape(B, kvw, nh, G * n_new), ((0, 0), (0, 0), (0, 0), (0, LANES - G * n_new)))
    qbd = qbd.reshape(B, kvw, nl)
    gl = gates.reshape(B, n_new, G, LANES)[..., :3 * nh].reshape(B, n_new, G, nh, 3)
    gl = jnp.transpose(gl, (0, 4, 3, 2, 1)).reshape(B, 3, nh, G * n_new)
    gl = jnp.pad(gl, ((0, 0), (0, DEC_ROWS - 3), (0, 0), (0, LANES - G * n_new))).reshape(B, DEC_ROWS, nl)
    pad_rows = lambda t: jnp.pad(t.reshape(B, n_new, -1), ((0, 0), (0, DEC_ROWS - n_new), (0, 0)))
    nkv = pad_rows(kv_new)
    nwin = pad_rows(win_new)

    pooled = pool_paged(cache, layer, page_table, pos_logits, kn_w)
    nf = pooled.shape[1]
    per_b = lambda *shape: pl.BlockSpec((None,) + shape, lambda b: (b,) + (0,) * len(shape))
    part, sel = pl.pallas_call(
        functools.partial(_nsa_dec_front_kernel, past_len=past_len, n_new=n_new, n_cand=n_cand),
        grid=(B,),
        in_specs=[per_b(kvw, nl), per_b(nf, 2 * kvw), per_b(cache_win.shape[1], 2 * kvw),
                  per_b(DEC_ROWS, 2 * kvw), per_b(DEC_ROWS, nl)],
        out_specs=[per_b(kvw, nl), per_b(n_rows, LANES)],
        out_shape=[jax.ShapeDtypeStruct((B, kvw, nl), F32), jax.ShapeDtypeStruct((B, n_rows, LANES), F32)],
        scratch_shapes=[pltpu.VMEM((n_rows, LANES), F32), pltpu.VMEM((1, nl), F32),
                        pltpu.VMEM((1, nl), F32), pltpu.VMEM((kvw, nl), F32)],
        compiler_params=_cparams(("parallel",)),
        name="nsa_dec_front",
    )(qbd, pooled, cache_win, nwin, gl)

    sel4 = sel.reshape(B, n_rows // bpp, bpp, LANES)
    page_spec = lambda i: pl.BlockSpec((None, None, page, 2 * kvw),
                                       lambda b, s, pt: (layer, pt[b, s * npg + i], 0, 1))
    y = pl.pallas_call(
        functools.partial(_nsa_dec_sel_kernel, past_len=past_len, n_new=n_new, page=page),
        grid_spec=pltpu.PrefetchScalarGridSpec(
            num_scalar_prefetch=1,
            grid=(B, n_pages // npg),
            in_specs=[page_spec(i) for i in range(npg)]
            + [pl.BlockSpec((None, kvw, nl), lambda b, s, pt: (b, 0, 0)),
               pl.BlockSpec((None, npg, bpp, LANES), lambda b, s, pt: (b, s, 0, 0)),
               pl.BlockSpec((None, 1, bpp, LANES), lambda b, s, pt: (b, n_pages, 0, 0)),
               pl.BlockSpec((None, DEC_ROWS, 2 * kvw), lambda b, s, pt: (b, 0, 1)),
               pl.BlockSpec((None, kvw, nl), lambda b, s, pt: (b, 0, 0)),
               pl.BlockSpec((None, DEC_ROWS, nl), lambda b, s, pt: (b, 0, 0))],
            out_specs=pl.BlockSpec((None, DEC_ROWS, G * nh * hd), lambda b, s, pt: (b, 0, 0)),
            scratch_shapes=[pltpu.VMEM((1, nl), F32), pltpu.VMEM((1, nl), F32), pltpu.VMEM((kvw, nl), F32)],
        ),
        out_shape=jax.ShapeDtypeStruct((B, DEC_ROWS, G * nh * hd), F32),
        compiler_params=_cparams(("parallel", "arbitrary")),
        name="nsa_dec_sel",
    )(page_table, *([cache] * npg), qbd, sel4, sel4, nkv, part, gl)
    return y[:, :n_new].reshape(B * n_new, G * nh * hd).astype(BF16)


def _lower_bounds_kernel(x_ref, o_ref):
    x = x_ref[...]
    e = jnp.exp(x - jnp.max(x, axis=0, keepdims=True))
    p = e / jnp.sum(e, axis=0, keepdims=True)
    c = p[0:1, :]
    o_ref[0:1, :] = jnp.zeros_like(c)
    for i in range(1, x.shape[0]):
        c = c + p[i:i + 1, :]
        o_ref[i:i + 1, :] = c - p[0:1, :]


def hgrn_lower_bounds(lb_logits):
    return pl.pallas_call(
        _lower_bounds_kernel,
        out_shape=jax.ShapeDtypeStruct(lb_logits.shape, F32),
        name="hgrn_lower_bounds",
    )(lb_logits)


def _layer_weights(w_in, w_branch_a, w_branch_b, w_out, w_up, w_down, hq, nq, kvw):
    o_q = 4 * hq
    o_kv = o_q + nq
    o_win = o_kv + 4 * kvw
    o_bg = o_win + 2 * kvw
    n_bg = 3 * NSA_KV * NSA_GROUP
    o_mg = o_bg + n_bg
    c = lambda t: t.astype(BF16)
    w_bg = w_in[:, o_bg:o_mg].reshape(-1, NSA_KV, 3 * NSA_GROUP)
    w_bg = jnp.pad(w_bg, ((0, 0), (0, 0), (0, LANES - 3 * NSA_GROUP))).reshape(-1, NSA_KV * LANES)
    return dict(hgrn=c(w_in[:, :o_q]), q=c(w_in[:, o_q:o_kv]), kv=c(w_in[:, o_kv:o_win]),
                win=c(w_in[:, o_win:o_bg]), bg=c(w_bg), mg=c(w_in[:, o_mg:]),
                a=c(w_branch_a), b=c(w_branch_b), out=c(w_out), up=c(w_up), down=c(w_down))


def _project_in(x, w, norm1_w, q_norm_w, k_norm_w, kvw):
    h = rmsnorm_cast(x, norm1_w)
    nq = w["q"].shape[1]
    tile = lambda v, n: jnp.tile(v, n // HEAD_DIM).reshape(1, n)
    ones = lambda n: jnp.ones((1, n), F32)
    zeros = lambda n: jnp.zeros((1, n), F32)
    za = matmul(h, w["hgrn"], out_dtype=F32, name="in_hgrn")
    q = matmul(h, w["q"], out_dtype=F32, epilogue=_headnorm_epilogue,
               row_extras=(tile(q_norm_w, nq), ones(nq)), name="in_q")
    kv_flag = jnp.concatenate([zeros(2 * kvw), ones(kvw), zeros(kvw)], axis=1)
    kv = matmul(h, w["kv"], out_dtype=F32, epilogue=_headnorm_epilogue,
                row_extras=(tile(k_norm_w[1], 4 * kvw), kv_flag), name="in_kv")
    win_flag = jnp.concatenate([ones(kvw), zeros(kvw)], axis=1)
    win = matmul(h, w["win"], out_dtype=F32, epilogue=_headnorm_epilogue,
                 row_extras=(tile(k_norm_w[2], 2 * kvw), win_flag), name="in_win")
    gates = matmul(h, w["bg"], out_dtype=F32, name="in_gate")
    mg = matmul(h, w["mg"], out_dtype=F32, name="in_merge_gate")
    return za, q, kv, win, gates, mg


def _finish_layer(x, ya, yb, mg, w, norm2_w):
    mix = gated_merge(ya, yb, w["a"], w["b"], mg)
    x1 = matmul(mix, w["out"], out_dtype=F32, epilogue=_residual_epilogue, tile_extras=(x,), name="out_proj")
    h2 = rmsnorm_cast(x1, norm2_w)
    u = matmul(h2, w["up"], out_dtype=BF16, epilogue=_relu2_epilogue, name="mlp_up")
    return matmul(u, w["down"], out_dtype=F32, epilogue=_residual_epilogue, tile_extras=(x1,), name="mlp_down")


def kernel(x_prompt, x_sample, cache_kv, cache_win, state_hgrn, page_table, norm1_w, w_in, hgrn_lb_logits,
           hgrn_norm_w, q_norm_w, k_norm_w, cmp_pos_logits, w_branch_a, w_branch_b, w_out, norm2_w, w_up,
           w_down):
    depth = w_in.shape[0]
    B, L, D = x_prompt.shape
    Bs, Ls, _ = x_sample.shape
    H = state_hgrn.shape[2]
    hq = H * state_hgrn.shape[3]
    G, hd = cache_kv.shape[4], cache_kv.shape[5]
    kvw = G * hd
    nq = w_branch_b.shape[1]
    assert (G, hd) == (NSA_KV, HEAD_DIM) and nq == NSA_KV * NSA_GROUP * HEAD_DIM
    assert state_hgrn.shape[3] == LANES and state_hgrn.shape[4] == LANES
    lbs = hgrn_lower_bounds(hgrn_lb_logits)
    cache = cache_kv.reshape(cache_kv.shape[:3] + (4 * kvw,))
    dec_chunk = 2 * SUBLANES
    assert Ls <= dec_chunk

    xp = x_prompt.reshape(B * L, D)
    xs = x_sample.reshape(Bs * Ls, D)
    kv_p, kv_s, win_p, win_s, st_p, st_s = [], [], [], [], [], []
    for l in range(depth):
        w = _layer_weights(w_in[l], w_branch_a[l], w_branch_b[l], w_out[l], w_up[l], w_down[l], hq, nq, kvw)
        za, q, kv, win, gates, mg = _project_in(xp, w, norm1_w[l], q_norm_w[l], k_norm_w[l], kvw)
        ya, st = hgrn(za, lbs[l], hgrn_norm_w[l], None, B, L, H, HGRN_CHUNK)
        pooled = pool_prompt(kv, cmp_pos_logits[l], k_norm_w[l, 0])
        yb = nsa_prompt(q, kv, win, pooled, gates, B, L)
        xp = _finish_layer(xp, ya, yb, mg, w, norm2_w[l])
        kv_p.append(kv.reshape(B, L, 4, G, hd))
        wk = min(WINDOW, L)
        win_p.append(win.reshape(B, L, 2, G, hd)[:, L - wk:])
        st_p.append(st)
        za, q, kv, win, gates, mg = _project_in(xs, w, norm1_w[l], q_norm_w[l], k_norm_w[l], kvw)
        za_pad = jnp.pad(za.reshape(Bs, Ls, -1), ((0, 0), (0, dec_chunk - Ls), (0, 0)))
        ya, st = hgrn(za_pad.reshape(Bs * dec_chunk, -1), lbs[l], hgrn_norm_w[l], state_hgrn[l],
                      Bs, dec_chunk, H, dec_chunk, n_valid=Ls)
        ya = ya.reshape(Bs, dec_chunk, -1)[:, :Ls].reshape(Bs * Ls, -1)
        cwin = cache_win[l].reshape(Bs, cache_win.shape[2], 2 * kvw)
        yb = nsa_decode(q, kv, win, gates, cache, l, page_table, cwin, cmp_pos_logits[l], k_norm_w[l, 0], Ls)
        xs = _finish_layer(xs, ya, yb, mg, w, norm2_w[l])
        kv_s.append(kv.reshape(Bs, Ls, 4, G, hd))
        win_all = jnp.concatenate([cwin, win.reshape(Bs, Ls, 2 * kvw)], axis=1)
        ws = min(WINDOW, win_all.shape[1])
        win_s.append(win_all[:, win_all.shape[1] - ws:].reshape(Bs, ws, 2, G, hd))
        st_s.append(st)
    return (xp.reshape(B, L, D), xs.reshape(Bs, Ls, D), jnp.stack(kv_p), jnp.stack(kv_s),
            jnp.stack(win_p), jnp.stack(win_s), jnp.stack(st_p).astype(state_hgrn.dtype),
            jnp.stack(st_s).astype(state_hgrn.dtype))
```

```python
import functools
import math

import jax
import jax.numpy as jnp
from jax import lax
from jax.experimental import pallas as pl
from jax.experimental.pallas import tpu as pltpu

F32 = jnp.float32
BF16 = jnp.bfloat16

LANES = 128
SUBLANES = 8
VMEM_LIMIT = 56 * 1024 * 1024

HEAD_DIM = 128
NSA_KV = 4
NSA_GROUP = 4
BLOCK = 64
N_SEL = 16
N_LOCAL = 2
WINDOW = 512
FORCE_BONUS = float(NSA_GROUP + 1)
SCALE = HEAD_DIM ** -0.5
EPS = 1e-6
MASK_VALUE = -1e30
HGRN_CHUNK = 128


def _cparams(sem):
    return pltpu.CompilerParams(dimension_semantics=sem, vmem_limit_bytes=VMEM_LIMIT)


def _pick(n, prefs):
    for p in prefs:
        if n % p == 0:
            return p
    return n


def _rmsnorm_kernel(x_ref, w_ref, o_ref):
    x = x_ref[...]
    y = x * lax.rsqrt(jnp.mean(x * x, axis=-1, keepdims=True) + EPS)
    o_ref[...] = (y * w_ref[...]).astype(o_ref.dtype)


def rmsnorm_cast(x, w):
    M, D = x.shape
    tm = _pick(M, (256, 32))
    return pl.pallas_call(
        _rmsnorm_kernel,
        grid=(M // tm,),
        in_specs=[pl.BlockSpec((tm, D), lambda i: (i, 0)),
                  pl.BlockSpec((1, D), lambda i: (0, 0))],
        out_specs=pl.BlockSpec((tm, D), lambda i: (i, 0)),
        out_shape=jax.ShapeDtypeStruct((M, D), BF16),
        compiler_params=_cparams(("parallel",)),
        name="rmsnorm_cast",
    )(x, w.reshape(1, D))


def _mm_kernel(*refs, nk, n_tile, n_row, epilogue):
    x_ref, w_ref = refs[0], refs[1]
    tile_refs = refs[2:2 + n_tile]
    row_refs = refs[2 + n_tile:2 + n_tile + n_row]
    o_ref = refs[2 + n_tile + n_row]

    def finish(acc):
        extras = [r[...] for r in tile_refs] + [r[...] for r in row_refs]
        o_ref[...] = epilogue(acc, *extras).astype(o_ref.dtype)

    part = jnp.dot(x_ref[...], w_ref[...], preferred_element_type=F32)
    if nk == 1:
        finish(part)
    else:
        acc_ref = refs[3 + n_tile + n_row]
        k = pl.program_id(2)

        @pl.when(k == 0)
        def _():
            acc_ref[...] = part

        @pl.when(k > 0)
        def _():
            acc_ref[...] += part

        @pl.when(k == nk - 1)
        def _():
            finish(acc_ref[...])


def matmul(x, w, *, out_dtype, epilogue=None, tile_extras=(), row_extras=(),
           tm=None, tn=None, tk=None, name="matmul"):
    M, K = x.shape
    K2, N = w.shape
    assert K == K2
    tm = tm or _pick(M, (1024, 512, 256, 32))
    tn = tn or _pick(N, (1024, 512, 256, 128))
    tk = tk or (K if K <= 4096 else _pick(K, (2048,)))
    nk = K // tk
    assert M % tm == 0 and N % tn == 0 and K % tk == 0
    if epilogue is None:
        epilogue = lambda acc: acc
    in_specs = [pl.BlockSpec((tm, tk), lambda i, j, k: (i, k)),
                pl.BlockSpec((tk, tn), lambda i, j, k: (k, j))]
    in_specs += [pl.BlockSpec((tm, tn), lambda i, j, k: (i, j)) for _ in tile_extras]
    in_specs += [pl.BlockSpec((1, tn), lambda i, j, k: (0, j)) for _ in row_extras]
    scratch = [pltpu.VMEM((tm, tn), F32)] if nk > 1 else []
    return pl.pallas_call(
        functools.partial(_mm_kernel, nk=nk, n_tile=len(tile_extras), n_row=len(row_extras),
                          epilogue=epilogue),
        grid=(M // tm, N // tn, nk),
        in_specs=in_specs,
        out_specs=pl.BlockSpec((tm, tn), lambda i, j, k: (i, j)),
        out_shape=jax.ShapeDtypeStruct((M, N), out_dtype),
        scratch_shapes=scratch,
        compiler_params=_cparams(("parallel", "parallel", "arbitrary")),
        name=name,
    )(x, w, *tile_extras, *row_extras)


def _mm_ws_kernel(*refs, nmp, n_tile, n_row, epilogue):
    xp_ref, xs_ref, w_ref = refs[:3]
    tp_refs = refs[3:3 + n_tile]
    ts_refs = refs[3 + n_tile:3 + 2 * n_tile]
    row_refs = refs[3 + 2 * n_tile:3 + 2 * n_tile + n_row]
    op_ref, os_ref, wb_ref = refs[3 + 2 * n_tile + n_row:]
    m = pl.program_id(1)

    @pl.when(m == 0)
    def _():
        wb_ref[...] = w_ref[...].astype(BF16)

    def run(x_ref, t_refs, o_ref):
        acc = jnp.dot(x_ref[...], wb_ref[...], preferred_element_type=F32)
        extras = [t[...] for t in t_refs] + [r[...] for r in row_refs]
        o_ref[...] = epilogue(acc, *extras).astype(o_ref.dtype)

    @pl.when(m < nmp)
    def _():
        run(xp_ref, tp_refs, op_ref)

    @pl.when(m == nmp)
    def _():
        run(xs_ref, ts_refs, os_ref)


WS_TN = 512


def matmul_ws(xp, xs, w, layer, col_off, n, *, out_dtype, epilogue=None, tile_extras=(), row_extras=(),
              name="matmul_ws"):
    Mp, K = xp.shape
    Ms = xs.shape[0]
    tn = WS_TN
    tm = _pick(Mp, (1024, 512, 256))
    assert col_off % tn == 0 and n % tn == 0 and Mp % tm == 0 and w.shape[-2] == K
    nmp = Mp // tm
    joff = col_off // tn
    prow = lambda m: jnp.minimum(m, nmp - 1)
    if layer is None:
        w_spec = pl.BlockSpec((K, tn), lambda j, m: (0, joff + j))
    else:
        w_spec = pl.BlockSpec((None, K, tn), lambda j, m: (layer, 0, joff + j))
    if epilogue is None:
        epilogue = lambda acc: acc
    in_specs = [pl.BlockSpec((tm, K), lambda j, m: (prow(m), 0)),
                pl.BlockSpec((Ms, K), lambda j, m: (0, 0)),
                w_spec]
    in_specs += [pl.BlockSpec((tm, tn), lambda j, m: (prow(m), j)) for _ in tile_extras]
    in_specs += [pl.BlockSpec((Ms, tn), lambda j, m: (0, j)) for _ in tile_extras]
    in_specs += [pl.BlockSpec((1, tn), lambda j, m: (0, j)) for _ in row_extras]
    return pl.pallas_call(
        functools.partial(_mm_ws_kernel, nmp=nmp, n_tile=len(tile_extras), n_row=len(row_extras),
                          epilogue=epilogue),
        grid=(n // tn, nmp + 1),
        in_specs=in_specs,
        out_specs=[pl.BlockSpec((tm, tn), lambda j, m: (prow(m), j)),
                   pl.BlockSpec((Ms, tn), lambda j, m: (0, j))],
        out_shape=[jax.ShapeDtypeStruct((Mp, n), out_dtype), jax.ShapeDtypeStruct((Ms, n), out_dtype)],
        scratch_shapes=[pltpu.VMEM((K, tn), BF16)],
        compiler_params=_cparams(("parallel", "arbitrary")),
        name=name,
    )(xp, xs, w, *[t[0] for t in tile_extras], *[t[1] for t in tile_extras], *row_extras)


def _headnorm_epilogue(acc, nw, flag):
    outs = []
    for c in range(acc.shape[1] // HEAD_DIM):
        sl = slice(c * HEAD_DIM, (c + 1) * HEAD_DIM)
        z = acc[:, sl]
        zn = z * lax.rsqrt(jnp.mean(z * z, axis=-1, keepdims=True) + EPS) * nw[:, sl]
        outs.append(jnp.where(flag[:, sl] != 0.0, zn, z))
    return jnp.concatenate(outs, axis=1)


def _residual_epilogue(acc, res):
    return res + acc


def _relu2_epilogue(acc):
    r = jnp.maximum(acc, 0.0)
    return r * r


def _merge_kernel(ya_ref, yb_ref, wa_ref, wb_ref, g0_ref, g1_ref, o_ref):
    pa = jnp.dot(ya_ref[...], wa_ref[...], preferred_element_type=F32)
    pb = jnp.dot(yb_ref[...], wb_ref[...], preferred_element_type=F32)
    mix = jax.nn.sigmoid(g0_ref[...]) * pa + jax.nn.sigmoid(g1_ref[...]) * pb
    o_ref[...] = mix.astype(o_ref.dtype)


def gated_merge(ya, yb, wa, wb, m_gate, gate_off=0):
    M, Ka = ya.shape
    Kb = yb.shape[1]
    D = wa.shape[1]
    tm = _pick(M, (1024, 512, 256, 32))
    tn = _pick(D, (512, 256, 128))
    nj = D // tn
    assert gate_off % tn == 0
    goff = gate_off // tn
    return pl.pallas_call(
        _merge_kernel,
        grid=(M // tm, nj),
        in_specs=[pl.BlockSpec((tm, Ka), lambda i, j: (i, 0)),
                  pl.BlockSpec((tm, Kb), lambda i, j: (i, 0)),
                  pl.BlockSpec((Ka, tn), lambda i, j: (0, j)),
                  pl.BlockSpec((Kb, tn), lambda i, j: (0, j)),
                  pl.BlockSpec((tm, tn), lambda i, j: (i, goff + j)),
                  pl.BlockSpec((tm, tn), lambda i, j: (i, goff + nj + j))],
        out_specs=pl.BlockSpec((tm, tn), lambda i, j: (i, j)),
        out_shape=jax.ShapeDtypeStruct((M, D), BF16),
        compiler_params=_cparams(("parallel", "parallel")),
        name="gated_merge",
    )(ya, yb, wa, wb, m_gate, m_gate)


def _silu(x):
    return x * jax.nn.sigmoid(x)


def _pair_total(cm, m, t_io):
    C = cm.shape[0]
    if m == 1:
        return jnp.where((t_io & 1) != 0, pltpu.roll(cm, 1, 0), cm)
    if m == 2:
        j = t_io & 3
        return jnp.where(j == 0, pltpu.roll(cm, C - 1, 0),
                         jnp.where(j == 1, cm,
                                   jnp.where(j == 2, pltpu.roll(cm, 1, 0), pltpu.roll(cm, 2, 0))))
    x3 = cm.reshape(C // (2 * m), 2 * m, cm.shape[1])
    return jnp.broadcast_to(x3[:, m - 1:m, :], x3.shape).reshape(cm.shape)


def _hgrn_chunk(aq, af, ai, ag, lb, nw, st, n_valid):
    C = aq.shape[0]
    t_io = lax.broadcasted_iota(jnp.int32, (C, LANES), 0)
    row_io = lax.broadcasted_iota(jnp.int32, (C, C), 0)
    col_io = lax.broadcasted_iota(jnp.int32, (C, C), 1)
    xor_io = row_io ^ col_io
    f = lb + (1.0 - lb) * jax.nn.sigmoid(af)
    if n_valid < C:
        f = jnp.where(t_io < n_valid, f, 1.0)
    g = jnp.log(f)
    kk = 1.0 - f
    qq = _silu(aq)
    v = ai.astype(BF16)
    a = jnp.zeros((C, C), F32)
    cm = g
    m = 1
    while m < C:
        tot = _pair_total(cm, m, t_io)
        odd = (t_io & m) != 0
        e = jnp.exp(jnp.where(odd, cm, tot - cm))
        qs = jnp.where(odd, qq * e, 0.0).astype(BF16)
        ks = jnp.where(odd, 0.0, kk * e).astype(BF16)
        p = lax.dot_general(qs, ks, (((1,), (1,)), ((), ())), preferred_element_type=F32)
        a = a + jnp.where(xor_io < 2 * m, p, 0.0)
        cm = cm + jnp.where(odd, tot, 0.0)
        m *= 2
    b = cm
    d = jnp.sum(qq * kk, axis=-1, keepdims=True)
    a = jnp.where(row_io == col_io, d, a)
    o_intra = jnp.dot(a.astype(BF16), v, preferred_element_type=F32)
    qe = (qq * jnp.exp(b)).astype(BF16)
    o_inter = lax.dot_general(qe, st.astype(BF16), (((1,), (1,)), ((), ())), preferred_element_type=F32)
    b_end = b[C - 1:C, :]
    ku = (kk * jnp.exp(b_end - b)).astype(BF16)
    ut = lax.dot_general(v, ku, (((0,), (0,)), ((), ())), preferred_element_type=F32)
    st_new = st * jnp.exp(b_end) + ut
    o = o_inter + o_intra
    on = o * lax.rsqrt(jnp.mean(o * o, axis=-1, keepdims=True) + EPS) * nw
    return on * _silu(ag), st_new


def _hgrn_kernel(*refs, n_sub, chunk, n_valid, has_s0):
    aq_ref, af_ref, ai_ref, ag_ref, lb_ref, nw_ref = refs[:6]
    s0_ref = refs[6] if has_s0 else None
    y_ref, s_ref, st_ref = refs[6 + has_s0:]
    c = pl.program_id(2)

    @pl.when(c == 0)
    def _():
        st_ref[...] = s0_ref[...].T if has_s0 else jnp.zeros_like(st_ref)

    lb = lb_ref[...]
    nw = nw_ref[...]
    for j in range(n_sub):
        rows = pl.ds(j * chunk, chunk)
        y, st_new = _hgrn_chunk(aq_ref[rows, :], af_ref[rows, :], ai_ref[rows, :], ag_ref[rows, :],
                                lb, nw, st_ref[...], n_valid)
        st_ref[...] = st_new
        y_ref[rows, :] = y.astype(y_ref.dtype)

    @pl.when(c == pl.num_programs(2) - 1)
    def _():
        s_ref[...] = st_ref[...].T


def hgrn(za, lb, nw, s0, B, L, H, chunk, n_valid=None):
    dk = LANES
    assert L % chunk == 0 and (n_valid is None or L == chunk)
    n_sub = _pick(L // chunk, (4, 2, 1))
    tc = n_sub * chunk
    nc = L // tc
    row = lambda off: pl.BlockSpec((tc, dk), lambda b, h, c: (b * nc + c, off + h))
    state = pl.BlockSpec((None, None, dk, dk), lambda b, h, c: (b, h, 0, 0))
    has_s0 = s0 is not None
    return pl.pallas_call(
        functools.partial(_hgrn_kernel, n_sub=n_sub, chunk=chunk,
                          n_valid=chunk if n_valid is None else n_valid, has_s0=has_s0),
        grid=(B, H, nc),
        in_specs=[row(0), row(H), row(2 * H), row(3 * H),
                  pl.BlockSpec((1, dk), lambda b, h, c: (0, h)),
                  pl.BlockSpec((1, dk), lambda b, h, c: (0, 0))] + ([state] if has_s0 else []),
        out_specs=[pl.BlockSpec((tc, dk), lambda b, h, c: (b * nc + c, h)), state],
        out_shape=[jax.ShapeDtypeStruct((B * L, H * dk), BF16),
                   jax.ShapeDtypeStruct((B, H, dk, dk), F32)],
        scratch_shapes=[pltpu.VMEM((dk, dk), F32)],
        compiler_params=_cparams(("parallel", "parallel", "arbitrary")),
        name="hgrn",
    )(za, za, za, za, lb.reshape(1, -1), nw.reshape(1, dk), *([s0] if has_s0 else []))


def hgrn_prompt(za, lb, nw, B, L, H):
    return hgrn(za, lb, nw, None, B, L, H, HGRN_CHUNK)


def _pool_rows(x, wl, nw):
    wl = wl - jnp.max(wl, axis=0, keepdims=True)
    e = jnp.exp(wl)
    w = e / jnp.sum(e, axis=0, keepdims=True)
    R = x.shape[0] // BLOCK
    pooled = jnp.sum(x.reshape(R, BLOCK, x.shape[1]) * w[None], axis=1)
    half = x.shape[1] // 2
    outs = []
    for c in range(x.shape[1] // HEAD_DIM):
        z = pooled[:, c * HEAD_DIM:(c + 1) * HEAD_DIM]
        if c * HEAD_DIM < half:
            z = z * lax.rsqrt(jnp.mean(z * z, axis=-1, keepdims=True) + EPS) * nw
        outs.append(z)
    return jnp.concatenate(outs, axis=1)


def _pool_kernel(x_ref, wl_ref, nw_ref, o_ref):
    o_ref[...] = _pool_rows(x_ref[...], wl_ref[...], nw_ref[...])


def pool_prompt(kv, pos_logits, kn_w):
    M = kv.shape[0]
    kvw2 = kv.shape[1] // 2
    rb = SUBLANES * BLOCK
    assert M % rb == 0
    return pl.pallas_call(
        _pool_kernel,
        grid=(M // rb,),
        in_specs=[pl.BlockSpec((rb, kvw2), lambda i: (i, 0)),
                  pl.BlockSpec((BLOCK, 1), lambda i: (0, 0)),
                  pl.BlockSpec((1, HEAD_DIM), lambda i: (0, 0))],
        out_specs=pl.BlockSpec((SUBLANES, kvw2), lambda i: (i, 0)),
        out_shape=jax.ShapeDtypeStruct((M // BLOCK, kvw2), F32),
        compiler_params=_cparams(("parallel",)),
        name="pool_prompt",
    )(kv, pos_logits.reshape(BLOCK, 1), kn_w.reshape(1, HEAD_DIM))


NSA_TQ = 256


def _select_blocks(imp, qpos, n_cand, score_ref):
    cand = lax.broadcasted_iota(jnp.int32, imp.shape, 0)
    cur = qpos // BLOCK
    valid = cand <= cur
    forced = (cand == 0) | (cand > cur - N_LOCAL)
    score = jnp.where(valid, imp + jnp.where(forced, FORCE_BONUS, 0.0), -jnp.inf)
    score_ref[...] = score

    def body(m, rank):
        row = score_ref[pl.ds(m, 1), :]
        beats = (row > score) | ((row == score) & (cand > m))
        return rank + jnp.where(beats, 1.0, 0.0)

    rank = lax.fori_loop(0, n_cand, body, jnp.zeros(imp.shape, F32), unroll=n_cand <= 32)
    return valid & (rank < float(N_SEL))


class _Flash:
    def __init__(self, m_ref, l_ref, acc_ref):
        self.m_ref, self.l_ref, self.acc_ref = m_ref, l_ref, acc_ref

    def init(self):
        self.m_ref[...] = jnp.full(self.m_ref.shape, MASK_VALUE, F32)
        self.l_ref[...] = jnp.zeros(self.l_ref.shape, F32)
        self.acc_ref[...] = jnp.zeros(self.acc_ref.shape, F32)

    def step(self, k, v, qt_b, *, scale=None, bias=None, mask=None):
        s = jnp.dot(k.astype(BF16), qt_b, preferred_element_type=F32)
        if scale is not None:
            s = s * scale
        if bias is not None:
            s = s + bias
        if mask is not None:
            s = jnp.where(mask, s, MASK_VALUE)
        m_old = self.m_ref[...]
        m_new = jnp.maximum(m_old, jnp.max(s, axis=0, keepdims=True))
        alpha = jnp.exp(m_old - m_new)
        p = jnp.exp(s - m_new)
        self.l_ref[...] = alpha * self.l_ref[...] + jnp.sum(p, axis=0, keepdims=True)
        pv = lax.dot_general(v.astype(BF16), p.astype(BF16), (((0,), (0,)), ((), ())),
                             preferred_element_type=F32)
        self.acc_ref[...] = alpha * self.acc_ref[...] + pv
        self.m_ref[...] = m_new

    def result(self):
        return self.acc_ref[...] * (1.0 / self.l_ref[...])


def _nsa_prompt_kernel(q_ref, ks_ref, vs_ref, kw_ref, vw_ref, kc_ref, vc_ref, g_ref, o_ref,
                       sel_ref, score_ref, ms_ref, ls_ref, accs_ref, mw_ref, lw_ref, accw_ref, *, nb):
    tq = NSA_TQ
    qt = pl.program_id(2)
    nh = NSA_GROUP
    q = q_ref[...] * SCALE
    qT = jnp.concatenate([q[:, h * HEAD_DIM:(h + 1) * HEAD_DIM].T for h in range(nh)], axis=1)
    qt_b = qT.astype(BF16)
    lane = lax.broadcasted_iota(jnp.int32, (1, nh * tq), 1)
    qpos = qt * tq + (lane & (tq - 1))

    s = jnp.dot(kc_ref[...], qT, preferred_element_type=F32, precision=lax.Precision.HIGHEST)
    n_io = lax.broadcasted_iota(jnp.int32, s.shape, 0)
    ready = ((n_io + 1) * BLOCK - 1) <= qpos
    s = jnp.where(ready, s, MASK_VALUE)
    e = jnp.exp(s - jnp.max(s, axis=0, keepdims=True))
    p = e / jnp.sum(e, axis=0, keepdims=True) * jnp.where(ready, 1.0, 0.0)
    o_cmp = lax.dot_general(vc_ref[...].astype(BF16), p.astype(BF16), (((0,), (0,)), ((), ())),
                            preferred_element_type=F32)
    imp = p[:, 0:tq]
    for h in range(1, nh):
        imp = imp + p[:, h * tq:(h + 1) * tq]
    sel = _select_blocks(imp, qpos[:, 0:tq], nb, score_ref)
    sel_bias = jnp.where(sel, 0.0, MASK_VALUE)
    bpt = tq // BLOCK
    for n in range(nb):
        sel_ref[n // bpt, n % bpt:n % bpt + 1, :] = sel_bias[n:n + 1, :]

    krow = lax.broadcasted_iota(jnp.int32, (tq, nh * tq), 0)
    sel_acc = _Flash(ms_ref, ls_ref, accs_ref)
    win_acc = _Flash(mw_ref, lw_ref, accw_ref)
    sel_acc.init()
    win_acc.init()

    def tile_bias(kt):
        blk = sel_ref[kt]
        rows = [jnp.broadcast_to(jnp.concatenate([blk[j:j + 1, :]] * nh, axis=1), (BLOCK, nh * tq))
                for j in range(bpt)]
        return jnp.concatenate(rows, axis=0)

    def sel_step(kt, mask=None):
        rows = pl.ds(pl.multiple_of(kt * tq, tq), tq)
        sel_acc.step(ks_ref[rows, :], vs_ref[rows, :], qt_b, bias=tile_bias(kt), mask=mask)

    def win_step(kt, mask):
        rows = pl.ds(pl.multiple_of(kt * tq, tq), tq)
        win_acc.step(kw_ref[rows, :], vw_ref[rows, :], qt_b, mask=mask)

    w_lo = jnp.maximum(qt - WINDOW // tq, 0)

    def far_body(kt, carry):
        sel_step(kt)
        return carry

    lax.fori_loop(0, w_lo, far_body, 0)

    def near_body(kt, carry):
        sel_step(kt)
        win_step(kt, (qpos - (kt * tq + krow)) < WINDOW)
        return carry

    lax.fori_loop(w_lo, qt, near_body, 0)

    causal = (qt * tq + krow) <= qpos
    sel_step(qt, causal)
    win_step(qt, causal)
    o_sel = sel_acc.result()
    o_win = win_acc.result()

    gT = jax.nn.sigmoid(g_ref[...]).T
    for h in range(nh):
        sl = slice(h * tq, (h + 1) * tq)
        y = (gT[3 * h:3 * h + 1, :] * o_cmp[:, sl] + gT[3 * h + 1:3 * h + 2, :] * o_sel[:, sl]
             + gT[3 * h + 2:3 * h + 3, :] * o_win[:, sl])
        o_ref[:, h * HEAD_DIM:(h + 1) * HEAD_DIM] = y.T.astype(o_ref.dtype)


def nsa_prompt(q, kv, win, pooled, gates, B, L):
    tq = NSA_TQ
    G, nh, hd = NSA_KV, NSA_GROUP, HEAD_DIM
    assert L % tq == 0 and tq % BLOCK == 0 and WINDOW % tq == 0
    nq = L // tq
    nb = L // BLOCK
    full = lambda off: pl.BlockSpec((L, hd), lambda b, g, t: (b, off + g))
    return pl.pallas_call(
        functools.partial(_nsa_prompt_kernel, nb=nb),
        grid=(B, G, nq),
        in_specs=[pl.BlockSpec((tq, nh * hd), lambda b, g, t: (b * nq + t, g)),
                  full(2 * G), full(3 * G),
                  pl.BlockSpec((L, hd), lambda b, g, t: (b, g)),
                  pl.BlockSpec((L, hd), lambda b, g, t: (b, G + g)),
                  pl.BlockSpec((nb, hd), lambda b, g, t: (b, g)),
                  pl.BlockSpec((nb, hd), lambda b, g, t: (b, G + g)),
                  pl.BlockSpec((tq, LANES), lambda b, g, t: (b * nq + t, g))],
        out_specs=pl.BlockSpec((tq, nh * hd), lambda b, g, t: (b * nq + t, g)),
        out_shape=jax.ShapeDtypeStruct((B * L, G * nh * hd), BF16),
        scratch_shapes=[pltpu.VMEM((nb * BLOCK // tq, tq // BLOCK, tq), F32),
                        pltpu.VMEM((nb, tq), F32)]
        + 2 * [pltpu.VMEM((1, nh * tq), F32), pltpu.VMEM((1, nh * tq), F32), pltpu.VMEM((hd, nh * tq), F32)],
        compiler_params=_cparams(("parallel", "parallel", "arbitrary")),
        name="nsa_prompt",
    )(q, kv, kv, win, win, pooled, pooled, gates)


DEC_PAGES_PER_STEP = 4
DEC_ROWS = 8


def _dec_qpos(past_len, n_new, shape):
    lane = lax.broadcasted_iota(jnp.int32, shape, len(shape) - 1)
    return past_len + lax.rem(lane & (LANES - 1), n_new)


def _page_spec(cache, layer, half, i, npg):
    return pl.BlockSpec((None, None, cache.shape[2], None) + cache.shape[4:],
                        lambda b, s, pt: (layer, pt[b, s * npg + i], 0, half, 0, 0))


def _page_rows(page_ref, r):
    n, rows, hd = page_ref.shape
    return page_ref.reshape(n * rows, hd)[pl.ds(r, n, stride=rows), :]


def _pool_paged_kernel(pt_ref, *refs):
    del pt_ref
    npg = DEC_PAGES_PER_STEP
    wl_ref, nw_ref, o_ref = refs[npg:npg + 3]
    wl = wl_ref[...]
    e = jnp.exp(wl - jnp.max(wl, axis=0, keepdims=True))
    w = e / jnp.sum(e, axis=0, keepdims=True)
    bpp = refs[0].shape[0] // BLOCK
    for i in range(npg):
        for r in range(2 * NSA_KV):
            x = _page_rows(refs[i], r)
            z = jnp.sum(x.reshape(bpp, BLOCK, HEAD_DIM) * w[None], axis=1)
            if r < NSA_KV:
                z = z * lax.rsqrt(jnp.mean(z * z, axis=-1, keepdims=True) + EPS) * nw_ref[...]
            o_ref[r, i * bpp:(i + 1) * bpp, :] = z


def pool_paged(cache, layer, page_table, pos_logits, kn_w):
    B, n_pages = page_table.shape
    page = cache.shape[2]
    npg = DEC_PAGES_PER_STEP
    assert n_pages % npg == 0 and page % BLOCK == 0
    bpp = page // BLOCK
    assert (npg * bpp) % SUBLANES == 0
    return pl.pallas_call(
        _pool_paged_kernel,
        grid_spec=pltpu.PrefetchScalarGridSpec(
            num_scalar_prefetch=1,
            grid=(B, n_pages // npg),
            in_specs=[_page_spec(cache, layer, 0, i, npg) for i in range(npg)]
            + [pl.BlockSpec((BLOCK, 1), lambda b, s, pt: (0, 0)),
               pl.BlockSpec((1, HEAD_DIM), lambda b, s, pt: (0, 0))],
            out_specs=pl.BlockSpec((None, 2 * NSA_KV, npg * bpp, HEAD_DIM), lambda b, s, pt: (b, 0, s, 0)),
        ),
        out_shape=jax.ShapeDtypeStruct((B, 2 * NSA_KV, n_pages * bpp, HEAD_DIM), F32),
        compiler_params=_cparams(("parallel", "arbitrary")),
        name="pool_paged",
    )(page_table, *([cache] * npg), pos_logits.reshape(BLOCK, 1), kn_w.reshape(1, HEAD_DIM))


def _nsa_dec_front_kernel(qbd_ref, pooled_ref, cwin_ref, nwin_ref, gl_ref, part_ref, sel_ref,
                          score_ref, m_ref, l_ref, acc_ref, *, past_len, n_new, n_cand):
    kvw = qbd_ref.shape[0]
    nl = qbd_ref.shape[1]
    qbd = qbd_ref[...]
    qbd_b = qbd.astype(BF16)
    qpos = _dec_qpos(past_len, n_new, (1, nl))

    nf = pooled_ref.shape[1]
    kc = jnp.concatenate([pooled_ref[g] for g in range(NSA_KV)], axis=1)
    vc = jnp.concatenate([pooled_ref[NSA_KV + g] for g in range(NSA_KV)], axis=1)
    s = jnp.dot(kc, qbd, preferred_element_type=F32, precision=lax.Precision.HIGHEST) * SCALE
    n_io = lax.broadcasted_iota(jnp.int32, s.shape, 0)
    ready = ((n_io + 1) * BLOCK - 1) <= qpos
    s = jnp.where(ready, s, MASK_VALUE)
    e = jnp.exp(s - jnp.max(s, axis=0, keepdims=True))
    p = e / jnp.sum(e, axis=0, keepdims=True) * jnp.where(ready, 1.0, 0.0)
    o_cmp = lax.dot_general(vc.astype(BF16), p.astype(BF16), (((0,), (0,)), ((), ())),
                            preferred_element_type=F32)
    imp = p[:, 0:LANES]
    for h in range(1, NSA_GROUP):
        imp = imp + p[:, h * LANES:(h + 1) * LANES]
    imp = jnp.concatenate([imp, jnp.zeros((score_ref.shape[0] - nf, LANES), F32)], axis=0)
    sel = _select_blocks(imp, qpos[:, 0:LANES], n_cand, score_ref)
    sel_ref[...] = jnp.where(sel, 0.0, MASK_VALUE)

    win_acc = _Flash(m_ref, l_ref, acc_ref)
    win_acc.init()
    w_rows = cwin_ref.shape[0]
    krow = lax.broadcasted_iota(jnp.int32, (w_rows, nl), 0)
    d = qpos - (past_len - w_rows + krow)
    win_acc.step(cwin_ref[:, 0:kvw], cwin_ref[:, kvw:2 * kvw], qbd_b, scale=SCALE,
                 mask=(d >= 0) & (d < WINDOW))
    nrow = lax.broadcasted_iota(jnp.int32, (nwin_ref.shape[0], nl), 0)
    d = qpos - (past_len + nrow)
    win_acc.step(nwin_ref[:, 0:kvw], nwin_ref[:, kvw:2 * kvw], qbd_b, scale=SCALE,
                 mask=(d >= 0) & (d < WINDOW) & (nrow < n_new))
    g = jax.nn.sigmoid(gl_ref[...])
    part_ref[...] = g[0:1, :] * o_cmp + g[2:3, :] * win_acc.result()


def _nsa_dec_sel_kernel(pt_ref, *refs, past_len, n_new, page):
    del pt_ref
    npg = DEC_PAGES_PER_STEP
    page_refs = refs[:npg]
    qbd_ref, sel_ref, sel_new_ref, nkv_ref, part_ref, gl_ref, y_ref, m_ref, l_ref, acc_ref = refs[npg:]
    kvw = qbd_ref.shape[0]
    nl = qbd_ref.shape[1]
    step = pl.program_id(1)
    qbd_b = qbd_ref[...].astype(BF16)
    qpos = _dec_qpos(past_len, n_new, (1, nl))

    sel_acc = _Flash(m_ref, l_ref, acc_ref)

    @pl.when(step == 0)
    def _():
        sel_acc.init()

    def lanes_of(row):
        return jnp.concatenate([row] * NSA_GROUP, axis=1)

    for i in range(npg):
        blk = sel_ref[i]
        bias = jnp.concatenate([jnp.broadcast_to(lanes_of(blk[j:j + 1, :]), (BLOCK, nl))
                                for j in range(page // BLOCK)], axis=0)
        k = jnp.concatenate([_page_rows(page_refs[i], g) for g in range(NSA_KV)], axis=1)
        v = jnp.concatenate([_page_rows(page_refs[i], NSA_KV + g) for g in range(NSA_KV)], axis=1)
        sel_acc.step(k, v, qbd_b, scale=SCALE, bias=bias)

    @pl.when(step == pl.num_programs(1) - 1)
    def _():
        nrow = lax.broadcasted_iota(jnp.int32, (nkv_ref.shape[0], nl), 0)
        sel_acc.step(nkv_ref[:, 0:kvw], nkv_ref[:, kvw:2 * kvw], qbd_b, scale=SCALE,
                     bias=lanes_of(sel_new_ref[0][0:1, :]),
                     mask=(past_len + nrow <= qpos) & (nrow < n_new))
        g = jax.nn.sigmoid(gl_ref[...])
        yT = part_ref[...] + g[1:2, :] * sel_acc.result()
        y_ref[...] = jnp.zeros(y_ref.shape, y_ref.dtype)
        for grp in range(NSA_KV):
            for h in range(NSA_GROUP):
                blk = yT[grp * HEAD_DIM:(grp + 1) * HEAD_DIM, h * LANES:(h + 1) * LANES].T
                col = (grp * NSA_GROUP + h) * HEAD_DIM
                y_ref[0:n_new, col:col + HEAD_DIM] = blk[grp * n_new:(grp + 1) * n_new, :].astype(y_ref.dtype)


def nsa_decode(q, kv_new, win_new, gates, cache, layer, page_table, cache_win, pos_logits, kn_w, n_new):
    B, n_pages = page_table.shape
    page = cache.shape[2]
    G, nh, hd = NSA_KV, NSA_GROUP, HEAD_DIM
    kvw = G * hd
    nl = nh * LANES
    past_len = n_pages * page
    npg = DEC_PAGES_PER_STEP
    bpp = page // BLOCK
    n_cand = -(-(past_len + n_new) // BLOCK)
    n_rows = -(-(n_cand + bpp) // SUBLANES) * SUBLANES // bpp * bpp
    assert G * n_new <= LANES and n_new <= DEC_ROWS and n_new <= BLOCK and past_len % BLOCK == 0

    q5 = q.reshape(B, n_new, G, nh, hd)
    qt = jnp.transpose(q5, (0, 2, 4, 3, 1))
    qbd = qt[:, :, :, :, None, :] * jnp.eye(G, dtype=F32)[None, :, None, None, :, None]
    qbd = jnp.pad(qbd.reshape(B, kvw, nh, G * n_new), ((0, 0), (0, 0), (0, 0), (0, LANES - G * n_new)))
    qbd = qbd.reshape(B, kvw, nl)
    gl = gates.reshape(B, n_new, G, LANES)[..., :3 * nh].reshape(B, n_new, G, nh, 3)
    gl = jnp.transpose(gl, (0, 4, 3, 2, 1)).reshape(B, 3, nh, G * n_new)
    gl = jnp.pad(gl, ((0, 0), (0, DEC_ROWS - 3), (0, 0), (0, LANES - G * n_new))).reshape(B, DEC_ROWS, nl)
    pad_rows = lambda t: jnp.pad(t.reshape(B, n_new, -1), ((0, 0), (0, DEC_ROWS - n_new), (0, 0)))
    nkv = pad_rows(kv_new)
    nwin = pad_rows(win_new)

    pooled = pool_paged(cache, layer, page_table, pos_logits, kn_w)
    nf = pooled.shape[2]
    per_b = lambda *shape: pl.BlockSpec((None,) + shape, lambda b: (b,) + (0,) * len(shape))
    part, sel = pl.pallas_call(
        functools.partial(_nsa_dec_front_kernel, past_len=past_len, n_new=n_new, n_cand=n_cand),
        grid=(B,),
        in_specs=[per_b(kvw, nl), per_b(2 * G, nf, hd), per_b(cache_win.shape[1], 2 * kvw),
                  per_b(DEC_ROWS, 2 * kvw), per_b(DEC_ROWS, nl)],
        out_specs=[per_b(kvw, nl), per_b(n_rows, LANES)],
        out_shape=[jax.ShapeDtypeStruct((B, kvw, nl), F32), jax.ShapeDtypeStruct((B, n_rows, LANES), F32)],
        scratch_shapes=[pltpu.VMEM((n_rows, LANES), F32), pltpu.VMEM((1, nl), F32),
                        pltpu.VMEM((1, nl), F32), pltpu.VMEM((kvw, nl), F32)],
        compiler_params=_cparams(("parallel",)),
        name="nsa_dec_front",
    )(qbd, pooled, cache_win, nwin, gl)

    sel4 = sel.reshape(B, n_rows // bpp, bpp, LANES)
    page_spec = lambda i: _page_spec(cache, layer, 1, i, npg)
    y = pl.pallas_call(
        functools.partial(_nsa_dec_sel_kernel, past_len=past_len, n_new=n_new, page=page),
        grid_spec=pltpu.PrefetchScalarGridSpec(
            num_scalar_prefetch=1,
            grid=(B, n_pages // npg),
            in_specs=[page_spec(i) for i in range(npg)]
            + [pl.BlockSpec((None, kvw, nl), lambda b, s, pt: (b, 0, 0)),
               pl.BlockSpec((None, npg, bpp, LANES), lambda b, s, pt: (b, s, 0, 0)),
               pl.BlockSpec((None, 1, bpp, LANES), lambda b, s, pt: (b, n_pages, 0, 0)),
               pl.BlockSpec((None, DEC_ROWS, 2 * kvw), lambda b, s, pt: (b, 0, 1)),
               pl.BlockSpec((None, kvw, nl), lambda b, s, pt: (b, 0, 0)),
               pl.BlockSpec((None, DEC_ROWS, nl), lambda b, s, pt: (b, 0, 0))],
            out_specs=pl.BlockSpec((None, DEC_ROWS, G * nh * hd), lambda b, s, pt: (b, 0, 0)),
            scratch_shapes=[pltpu.VMEM((1, nl), F32), pltpu.VMEM((1, nl), F32), pltpu.VMEM((kvw, nl), F32)],
        ),
        out_shape=jax.ShapeDtypeStruct((B, DEC_ROWS, G * nh * hd), F32),
        compiler_params=_cparams(("parallel", "arbitrary")),
        name="nsa_dec_sel",
    )(page_table, *([cache] * npg), qbd, sel4, sel4, nkv, part, gl)
    return y[:, :n_new].reshape(B * n_new, G * nh * hd).astype(BF16)


def _lower_bounds_kernel(x_ref, o_ref):
    x = x_ref[...]
    e = jnp.exp(x - jnp.max(x, axis=0, keepdims=True))
    p = e / jnp.sum(e, axis=0, keepdims=True)
    c = p[0:1, :]
    o_ref[0:1, :] = jnp.zeros_like(c)
    for i in range(1, x.shape[0]):
        c = c + p[i:i + 1, :]
        o_ref[i:i + 1, :] = c - p[0:1, :]


def hgrn_lower_bounds(lb_logits):
    return pl.pallas_call(
        _lower_bounds_kernel,
        out_shape=jax.ShapeDtypeStruct(lb_logits.shape, F32),
        name="hgrn_lower_bounds",
    )(lb_logits)


GATE_W = NSA_KV * LANES


def _project_in(xp, xs, w_in, layer, norm1_w, q_norm_w, k_norm_w, hq, nq, kvw):
    hp = rmsnorm_cast(xp, norm1_w)
    hs = rmsnorm_cast(xs, norm1_w)
    o_q = 4 * hq
    o_kv = o_q + nq
    o_win = o_kv + 4 * kvw
    o_bg = o_win + 2 * kvw
    n_bg = 3 * NSA_KV * NSA_GROUP
    o_mg = o_bg + n_bg
    tile = lambda v, n: jnp.tile(v, n // HEAD_DIM).reshape(1, n)
    ones = lambda n: jnp.ones((1, n), F32)
    zeros = lambda n: jnp.zeros((1, n), F32)
    za = matmul_ws(hp, hs, w_in, layer, 0, o_q, out_dtype=F32, name="in_hgrn")
    q = matmul_ws(hp, hs, w_in, layer, o_q, nq, out_dtype=F32, epilogue=_headnorm_epilogue,
                  row_extras=(tile(q_norm_w, nq), ones(nq)), name="in_q")
    kv_flag = jnp.concatenate([zeros(2 * kvw), ones(kvw), zeros(kvw)], axis=1)
    kv = matmul_ws(hp, hs, w_in, layer, o_kv, 4 * kvw, out_dtype=F32, epilogue=_headnorm_epilogue,
                   row_extras=(tile(k_norm_w[1], 4 * kvw), kv_flag), name="in_kv")
    win_flag = jnp.concatenate([ones(kvw), zeros(kvw)], axis=1)
    win = matmul_ws(hp, hs, w_in, layer, o_win, 2 * kvw, out_dtype=F32, epilogue=_headnorm_epilogue,
                    row_extras=(tile(k_norm_w[2], 2 * kvw), win_flag), name="in_win")
    w_l = w_in[layer]
    w_bg = w_l[:, o_bg:o_mg].reshape(-1, NSA_KV, 3 * NSA_GROUP)
    w_bg = jnp.pad(w_bg, ((0, 0), (0, 0), (0, LANES - 3 * NSA_GROUP))).reshape(-1, GATE_W)
    w_gm = jnp.concatenate([w_bg, w_l[:, o_mg:]], axis=1)
    gm = matmul_ws(hp, hs, w_gm, None, 0, w_gm.shape[1], out_dtype=F32, name="in_gates")
    return za, q, kv, win, gm


def _finish_layer(xp, xs, ya, yb, gm, wa, wb, w_out, w_up, w_down_b, layer, norm2_w):
    mix = [gated_merge(ya[i], yb[i], wa, wb, gm[i], gate_off=GATE_W) for i in range(2)]
    D = w_out.shape[2]
    x1 = matmul_ws(mix[0], mix[1], w_out, layer, 0, D, out_dtype=F32, epilogue=_residual_epilogue,
                   tile_extras=((xp, xs),), name="out_proj")
    h2 = [rmsnorm_cast(x, norm2_w) for x in x1]
    u = matmul_ws(h2[0], h2[1], w_up, layer, 0, w_up.shape[2], out_dtype=BF16, epilogue=_relu2_epilogue,
                  name="mlp_up")
    return [matmul(u[i], w_down_b, out_dtype=F32, epilogue=_residual_epilogue, tile_extras=(x1[i],),
                   name="mlp_down") for i in range(2)]


def kernel(x_prompt, x_sample, cache_kv, cache_win, state_hgrn, page_table, norm1_w, w_in, hgrn_lb_logits,
           hgrn_norm_w, q_norm_w, k_norm_w, cmp_pos_logits, w_branch_a, w_branch_b, w_out, norm2_w, w_up,
           w_down):
    depth = w_in.shape[0]
    B, L, D = x_prompt.shape
    Bs, Ls, _ = x_sample.shape
    H = state_hgrn.shape[2]
    hq = H * state_hgrn.shape[3]
    G, hd = cache_kv.shape[4], cache_kv.shape[5]
    kvw = G * hd
    nq = w_branch_b.shape[1]
    assert (G, hd) == (NSA_KV, HEAD_DIM) and nq == NSA_KV * NSA_GROUP * HEAD_DIM
    assert state_hgrn.shape[3] == LANES and state_hgrn.shape[4] == LANES
    lbs = hgrn_lower_bounds(hgrn_lb_logits)
    cache = cache_kv.reshape(cache_kv.shape[:3] + (2, 2 * G, hd))
    dec_chunk = 2 * SUBLANES
    assert Ls <= dec_chunk

    xp = x_prompt.reshape(B * L, D)
    xs = x_sample.reshape(Bs * Ls, D)
    kv_p, kv_s, win_p, win_s, st_p, st_s = [], [], [], [], [], []
    for l in range(depth):
        za, q, kv, win, gm = _project_in(xp, xs, w_in, l, norm1_w[l], q_norm_w[l], k_norm_w[l], hq, nq, kvw)
        ya_p, st = hgrn(za[0], lbs[l], hgrn_norm_w[l], None, B, L, H, HGRN_CHUNK)
        pooled = pool_prompt(kv[0], cmp_pos_logits[l], k_norm_w[l, 0])
        yb_p = nsa_prompt(q[0], kv[0], win[0], pooled, gm[0], B, L)
        kv_p.append(kv[0].reshape(B, L, 4, G, hd))
        wk = min(WINDOW, L)
        win_p.append(win[0].reshape(B, L, 2, G, hd)[:, L - wk:])
        st_p.append(st)
        za_pad = jnp.pad(za[1].reshape(Bs, Ls, -1), ((0, 0), (0, dec_chunk - Ls), (0, 0)))
        ya_s, st = hgrn(za_pad.reshape(Bs * dec_chunk, -1), lbs[l], hgrn_norm_w[l], state_hgrn[l],
                        Bs, dec_chunk, H, dec_chunk, n_valid=Ls)
        ya_s = ya_s.reshape(Bs, dec_chunk, -1)[:, :Ls].reshape(Bs * Ls, -1)
        cwin = cache_win[l].reshape(Bs, cache_win.shape[2], 2 * kvw)
        yb_s = nsa_decode(q[1], kv[1], win[1], gm[1][:, :GATE_W], cache, l, page_table, cwin,
                          cmp_pos_logits[l], k_norm_w[l, 0], Ls)
        kv_s.append(kv[1].reshape(Bs, Ls, 4, G, hd))
        win_all = jnp.concatenate([cwin, win[1].reshape(Bs, Ls, 2 * kvw)], axis=1)
        ws = min(WINDOW, win_all.shape[1])
        win_s.append(win_all[:, win_all.shape[1] - ws:].reshape(Bs, ws, 2, G, hd))
        st_s.append(st)
        xp, xs = _finish_layer(xp, xs, (ya_p, ya_s), (yb_p, yb_s), gm, w_branch_a[l].astype(BF16),
                               w_branch_b[l].astype(BF16), w_out, w_up, w_down[l].astype(BF16), l, norm2_w[l])
    return (xp.reshape(B, L, D), xs.reshape(Bs, Ls, D), jnp.stack(kv_p), jnp.stack(kv_s),
            jnp.stack(win_p), jnp.stack(win_s), jnp.stack(st_p).astype(state_hgrn.dtype),
            jnp.stack(st_s).astype(state_hgrn.dtype))
```

```python
import functools
import math

import jax
import jax.numpy as jnp
from jax import lax
from jax.experimental import pallas as pl
from jax.experimental.pallas import tpu as pltpu

F32 = jnp.float32
BF16 = jnp.bfloat16

LANES = 128
SUBLANES = 8
VMEM_LIMIT = 56 * 1024 * 1024

HEAD_DIM = 128
NSA_KV = 4
NSA_GROUP = 4
BLOCK = 64
N_SEL = 16
N_LOCAL = 2
WINDOW = 512
FORCE_BONUS = float(NSA_GROUP + 1)
SCALE = HEAD_DIM ** -0.5
EPS = 1e-6
MASK_VALUE = -1e30
HGRN_CHUNK = 128


def _cparams(sem):
    return pltpu.CompilerParams(dimension_semantics=sem, vmem_limit_bytes=VMEM_LIMIT)


def _pick(n, prefs):
    for p in prefs:
        if n % p == 0:
            return p
    return n


def _rmsnorm_kernel(x_ref, w_ref, o_ref):
    x = x_ref[...]
    y = x * lax.rsqrt(jnp.mean(x * x, axis=-1, keepdims=True) + EPS)
    o_ref[...] = (y * w_ref[...]).astype(o_ref.dtype)


def rmsnorm_cast(x, w):
    M, D = x.shape
    tm = _pick(M, (256, 32))
    return pl.pallas_call(
        _rmsnorm_kernel,
        grid=(M // tm,),
        in_specs=[pl.BlockSpec((tm, D), lambda i: (i, 0)),
                  pl.BlockSpec((1, D), lambda i: (0, 0))],
        out_specs=pl.BlockSpec((tm, D), lambda i: (i, 0)),
        out_shape=jax.ShapeDtypeStruct((M, D), BF16),
        compiler_params=_cparams(("parallel",)),
        name="rmsnorm_cast",
    )(x, w.reshape(1, D))


def _mm_kernel(*refs, nk, n_tile, n_row, epilogue):
    x_ref, w_ref = refs[0], refs[1]
    tile_refs = refs[2:2 + n_tile]
    row_refs = refs[2 + n_tile:2 + n_tile + n_row]
    o_ref = refs[2 + n_tile + n_row]

    def finish(acc):
        extras = [r[...] for r in tile_refs] + [r[...] for r in row_refs]
        o_ref[...] = epilogue(acc, *extras).astype(o_ref.dtype)

    part = jnp.dot(x_ref[...], w_ref[...], preferred_element_type=F32)
    if nk == 1:
        finish(part)
    else:
        acc_ref = refs[3 + n_tile + n_row]
        k = pl.program_id(2)

        @pl.when(k == 0)
        def _():
            acc_ref[...] = part

        @pl.when(k > 0)
        def _():
            acc_ref[...] += part

        @pl.when(k == nk - 1)
        def _():
            finish(acc_ref[...])


def matmul(x, w, *, out_dtype, layer=None, col_off=0, n=None, epilogue=None, tile_extras=(), row_extras=(),
           tm=None, tn=None, tk=None, name="matmul"):
    M, K = x.shape
    assert w.shape[-2] == K and (layer is None) == (w.ndim == 2)
    N = n or w.shape[-1]
    tm = tm or _pick(M, (1024, 512, 256, 32))
    tn = tn or _pick(N, (1024, 512, 256, 128))
    tk = tk or (K if K <= 4096 else _pick(K, (2048,)))
    nk = K // tk
    assert M % tm == 0 and N % tn == 0 and K % tk == 0 and col_off % tn == 0
    joff = col_off // tn
    if epilogue is None:
        epilogue = lambda acc: acc
    if layer is None:
        w_spec = pl.BlockSpec((tk, tn), lambda i, j, k: (k, joff + j))
    else:
        w_spec = pl.BlockSpec((None, tk, tn), lambda i, j, k: (layer, k, joff + j))
    in_specs = [pl.BlockSpec((tm, tk), lambda i, j, k: (i, k)), w_spec]
    in_specs += [pl.BlockSpec((tm, tn), lambda i, j, k: (i, j)) for _ in tile_extras]
    in_specs += [pl.BlockSpec((1, tn), lambda i, j, k: (0, j)) for _ in row_extras]
    scratch = [pltpu.VMEM((tm, tn), F32)] if nk > 1 else []
    return pl.pallas_call(
        functools.partial(_mm_kernel, nk=nk, n_tile=len(tile_extras), n_row=len(row_extras),
                          epilogue=epilogue),
        grid=(M // tm, N // tn, nk),
        in_specs=in_specs,
        out_specs=pl.BlockSpec((tm, tn), lambda i, j, k: (i, j)),
        out_shape=jax.ShapeDtypeStruct((M, N), out_dtype),
        scratch_shapes=scratch,
        compiler_params=_cparams(("parallel", "parallel", "arbitrary")),
        name=name,
    )(x, w, *tile_extras, *row_extras)


def _headnorm_epilogue(acc, nw, flag):
    outs = []
    for c in range(acc.shape[1] // HEAD_DIM):
        sl = slice(c * HEAD_DIM, (c + 1) * HEAD_DIM)
        z = acc[:, sl]
        zn = z * lax.rsqrt(jnp.mean(z * z, axis=-1, keepdims=True) + EPS) * nw[:, sl]
        outs.append(jnp.where(flag[:, sl] != 0.0, zn, z))
    return jnp.concatenate(outs, axis=1)


def _residual_epilogue(acc, res):
    return res + acc


def _relu2_epilogue(acc):
    r = jnp.maximum(acc, 0.0)
    return r * r


def _merge_kernel(ya_ref, yb_ref, wa_ref, wb_ref, g0_ref, g1_ref, o_ref):
    pa = jnp.dot(ya_ref[...], wa_ref[...], preferred_element_type=F32)
    pb = jnp.dot(yb_ref[...], wb_ref[...], preferred_element_type=F32)
    mix = jax.nn.sigmoid(g0_ref[...]) * pa + jax.nn.sigmoid(g1_ref[...]) * pb
    o_ref[...] = mix.astype(o_ref.dtype)


def gated_merge(ya, yb, wa, wb, layer, m_gate, gate_off=0):
    M, Ka = ya.shape
    Kb = yb.shape[1]
    D = wa.shape[2]
    tm = _pick(M, (1024, 512, 256, 32))
    tn = _pick(D, (512, 256, 128))
    nj = D // tn
    assert gate_off % tn == 0
    goff = gate_off // tn
    return pl.pallas_call(
        _merge_kernel,
        grid=(M // tm, nj),
        in_specs=[pl.BlockSpec((tm, Ka), lambda i, j: (i, 0)),
                  pl.BlockSpec((tm, Kb), lambda i, j: (i, 0)),
                  pl.BlockSpec((None, Ka, tn), lambda i, j: (layer, 0, j)),
                  pl.BlockSpec((None, Kb, tn), lambda i, j: (layer, 0, j)),
                  pl.BlockSpec((tm, tn), lambda i, j: (i, goff + j)),
                  pl.BlockSpec((tm, tn), lambda i, j: (i, goff + nj + j))],
        out_specs=pl.BlockSpec((tm, tn), lambda i, j: (i, j)),
        out_shape=jax.ShapeDtypeStruct((M, D), BF16),
        compiler_params=_cparams(("parallel", "parallel")),
        name="gated_merge",
    )(ya, yb, wa, wb, m_gate, m_gate)


def _silu(x):
    return x * jax.nn.sigmoid(x)


def _pair_total(cm, m, t_io):
    C = cm.shape[0]
    if m == 1:
        return jnp.where((t_io & 1) != 0, pltpu.roll(cm, 1, 0), cm)
    if m == 2:
        j = t_io & 3
        return jnp.where(j == 0, pltpu.roll(cm, C - 1, 0),
                         jnp.where(j == 1, cm,
                                   jnp.where(j == 2, pltpu.roll(cm, 1, 0), pltpu.roll(cm, 2, 0))))
    x3 = cm.reshape(C // (2 * m), 2 * m, cm.shape[1])
    return jnp.broadcast_to(x3[:, m - 1:m, :], x3.shape).reshape(cm.shape)


def _hgrn_chunk(aq, af, ai, ag, lb, nw, st, n_valid):
    C = aq.shape[0]
    t_io = lax.broadcasted_iota(jnp.int32, (C, LANES), 0)
    row_io = lax.broadcasted_iota(jnp.int32, (C, C), 0)
    col_io = lax.broadcasted_iota(jnp.int32, (C, C), 1)
    split = jnp.where(row_io > col_io, row_io ^ col_io, 0)
    for sh in (1, 2, 4, 8, 16):
        split = split | (split >> sh)
    split = split - (split >> 1)
    f = lb + (1.0 - lb) * jax.nn.sigmoid(af)
    if n_valid < C:
        f = jnp.where(t_io < n_valid, f, 1.0)
    g = jnp.log(f)
    kk = 1.0 - f
    qq = _silu(aq)
    v = ai.astype(BF16)
    a = jnp.zeros((C, C), F32)
    cm = g
    m = 1
    while m < C:
        tot = _pair_total(cm, m, t_io)
        odd = (t_io & m) != 0
        z = (jnp.where(odd, qq, kk) * jnp.exp(jnp.where(odd, cm, tot - cm))).astype(BF16)
        p = lax.dot_general(z, z, (((1,), (1,)), ((), ())), preferred_element_type=F32)
        a = a + jnp.where(split == m, p, 0.0)
        cm = cm + jnp.where(odd, tot, 0.0)
        m *= 2
    b = cm
    d = jnp.sum(qq * kk, axis=-1, keepdims=True)
    a = jnp.where(row_io == col_io, d, a)
    o_intra = jnp.dot(a.astype(BF16), v, preferred_element_type=F32)
    qe = (qq * jnp.exp(b)).astype(BF16)
    o_inter = lax.dot_general(qe, st.astype(BF16), (((1,), (1,)), ((), ())), preferred_element_type=F32)
    b_end = b[C - 1:C, :]
    ku = (kk * jnp.exp(b_end - b)).astype(BF16)
    ut = lax.dot_general(v, ku, (((0,), (0,)), ((), ())), preferred_element_type=F32)
    st_new = st * jnp.exp(b_end) + ut
    o = o_inter + o_intra
    on = o * lax.rsqrt(jnp.mean(o * o, axis=-1, keepdims=True) + EPS) * nw
    return on * _silu(ag), st_new


def _hgrn_kernel(*refs, n_sub, chunk, n_valid, has_s0):
    aq_ref, af_ref, ai_ref, ag_ref, lb_ref, nw_ref = refs[:6]
    s0_ref = refs[6] if has_s0 else None
    y_ref, s_ref, st_ref = refs[6 + has_s0:]
    c = pl.program_id(2)

    @pl.when(c == 0)
    def _():
        st_ref[...] = s0_ref[...].T if has_s0 else jnp.zeros_like(st_ref)

    lb = lb_ref[...]
    nw = nw_ref[...]
    for j in range(n_sub):
        rows = pl.ds(j * chunk, chunk)
        y, st_new = _hgrn_chunk(aq_ref[rows, :], af_ref[rows, :], ai_ref[rows, :], ag_ref[rows, :],
                                lb, nw, st_ref[...], n_valid)
        st_ref[...] = st_new
        y_ref[rows, :] = y.astype(y_ref.dtype)

    @pl.when(c == pl.num_programs(2) - 1)
    def _():
        s_ref[...] = st_ref[...].T


def hgrn(za, lb, nw, s0, B, L, H, chunk, n_valid=None):
    dk = LANES
    assert L % chunk == 0 and (n_valid is None or L == chunk)
    n_sub = _pick(L // chunk, (4, 2, 1))
    tc = n_sub * chunk
    nc = L // tc
    row = lambda off: pl.BlockSpec((tc, dk), lambda b, h, c: (b * nc + c, off + h))
    state = pl.BlockSpec((None, None, dk, dk), lambda b, h, c: (b, h, 0, 0))
    has_s0 = s0 is not None
    return pl.pallas_call(
        functools.partial(_hgrn_kernel, n_sub=n_sub, chunk=chunk,
                          n_valid=chunk if n_valid is None else n_valid, has_s0=has_s0),
        grid=(B, H, nc),
        in_specs=[row(0), row(H), row(2 * H), row(3 * H),
                  pl.BlockSpec((1, dk), lambda b, h, c: (0, h)),
                  pl.BlockSpec((1, dk), lambda b, h, c: (0, 0))] + ([state] if has_s0 else []),
        out_specs=[pl.BlockSpec((tc, dk), lambda b, h, c: (b * nc + c, h)), state],
        out_shape=[jax.ShapeDtypeStruct((B * L, H * dk), BF16),
                   jax.ShapeDtypeStruct((B, H, dk, dk), F32)],
        scratch_shapes=[pltpu.VMEM((dk, dk), F32)],
        compiler_params=_cparams(("parallel", "parallel", "arbitrary")),
        name="hgrn",
    )(za, za, za, za, lb.reshape(1, -1), nw.reshape(1, dk), *([s0] if has_s0 else []))


def hgrn_prompt(za, lb, nw, B, L, H):
    return hgrn(za, lb, nw, None, B, L, H, HGRN_CHUNK)


def _pool_rows(x, wl, nw):
    wl = wl - jnp.max(wl, axis=0, keepdims=True)
    e = jnp.exp(wl)
    w = e / jnp.sum(e, axis=0, keepdims=True)
    R = x.shape[0] // BLOCK
    pooled = jnp.sum(x.reshape(R, BLOCK, x.shape[1]) * w[None], axis=1)
    half = x.shape[1] // 2
    outs = []
    for c in range(x.shape[1] // HEAD_DIM):
        z = pooled[:, c * HEAD_DIM:(c + 1) * HEAD_DIM]
        if c * HEAD_DIM < half:
            z = z * lax.rsqrt(jnp.mean(z * z, axis=-1, keepdims=True) + EPS) * nw
        outs.append(z)
    return jnp.concatenate(outs, axis=1)


def _pool_kernel(x_ref, wl_ref, nw_ref, o_ref):
    o_ref[...] = _pool_rows(x_ref[...], wl_ref[...], nw_ref[...])


def pool_prompt(kv, pos_logits, kn_w):
    M = kv.shape[0]
    kvw2 = kv.shape[1] // 2
    rb = SUBLANES * BLOCK
    assert M % rb == 0
    return pl.pallas_call(
        _pool_kernel,
        grid=(M // rb,),
        in_specs=[pl.BlockSpec((rb, kvw2), lambda i: (i, 0)),
                  pl.BlockSpec((BLOCK, 1), lambda i: (0, 0)),
                  pl.BlockSpec((1, HEAD_DIM), lambda i: (0, 0))],
        out_specs=pl.BlockSpec((SUBLANES, kvw2), lambda i: (i, 0)),
        out_shape=jax.ShapeDtypeStruct((M // BLOCK, kvw2), F32),
        compiler_params=_cparams(("parallel",)),
        name="pool_prompt",
    )(kv, pos_logits.reshape(BLOCK, 1), kn_w.reshape(1, HEAD_DIM))


NSA_TQ = 256


def _select_blocks(imp, qpos, n_cand, score_ref):
    cand = lax.broadcasted_iota(jnp.int32, imp.shape, 0)
    cur = qpos // BLOCK
    valid = cand <= cur
    forced = (cand == 0) | (cand > cur - N_LOCAL)
    score = jnp.where(valid, imp + jnp.where(forced, FORCE_BONUS, 0.0), -jnp.inf)
    score_ref[...] = score

    def body(m, rank):
        row = score_ref[pl.ds(m, 1), :]
        beats = (row > score) | ((row == score) & (cand > m))
        return rank + jnp.where(beats, 1.0, 0.0)

    rank = lax.fori_loop(0, n_cand, body, jnp.zeros(imp.shape, F32), unroll=n_cand <= 32)
    return valid & (rank < float(N_SEL))


class _Flash:
    def __init__(self, m_ref, l_ref, acc_ref):
        self.m_ref, self.l_ref, self.acc_ref = m_ref, l_ref, acc_ref

    def init(self):
        self.m_ref[...] = jnp.full(self.m_ref.shape, MASK_VALUE, F32)
        self.l_ref[...] = jnp.zeros(self.l_ref.shape, F32)
        self.acc_ref[...] = jnp.zeros(self.acc_ref.shape, F32)

    def step(self, k, v, qt_b, *, scale=None, bias=None, mask=None):
        s = jnp.dot(k.astype(BF16), qt_b, preferred_element_type=F32)
        if scale is not None:
            s = s * scale
        if bias is not None:
            s = s + bias
        if mask is not None:
            s = jnp.where(mask, s, MASK_VALUE)
        m_old = self.m_ref[...]
        m_new = jnp.maximum(m_old, jnp.max(s, axis=0, keepdims=True))
        alpha = jnp.exp(m_old - m_new)
        p = jnp.exp(s - m_new)
        self.l_ref[...] = alpha * self.l_ref[...] + jnp.sum(p, axis=0, keepdims=True)
        pv = lax.dot_general(v.astype(BF16), p.astype(BF16), (((0,), (0,)), ((), ())),
                             preferred_element_type=F32)
        self.acc_ref[...] = alpha * self.acc_ref[...] + pv
        self.m_ref[...] = m_new

    def result(self):
        return self.acc_ref[...] * (1.0 / self.l_ref[...])


def _nsa_prompt_kernel(q_ref, ks_ref, vs_ref, kw_ref, vw_ref, kc_ref, vc_ref, g_ref, o_ref,
                       sel_ref, score_ref, ms_ref, ls_ref, accs_ref, mw_ref, lw_ref, accw_ref, *, nb):
    tq = NSA_TQ
    qt = pl.program_id(2)
    nh = NSA_GROUP
    q = q_ref[...] * SCALE
    qT = jnp.concatenate([q[:, h * HEAD_DIM:(h + 1) * HEAD_DIM].T for h in range(nh)], axis=1)
    qt_b = qT.astype(BF16)
    lane = lax.broadcasted_iota(jnp.int32, (1, nh * tq), 1)
    qpos = qt * tq + (lane & (tq - 1))

    s = jnp.dot(kc_ref[...], qT, preferred_element_type=F32, precision=lax.Precision.HIGHEST)
    n_io = lax.broadcasted_iota(jnp.int32, s.shape, 0)
    ready = ((n_io + 1) * BLOCK - 1) <= qpos
    s = jnp.where(ready, s, MASK_VALUE)
    e = jnp.exp(s - jnp.max(s, axis=0, keepdims=True))
    p = e / jnp.sum(e, axis=0, keepdims=True) * jnp.where(ready, 1.0, 0.0)
    o_cmp = lax.dot_general(vc_ref[...].astype(BF16), p.astype(BF16), (((0,), (0,)), ((), ())),
                            preferred_element_type=F32)
    imp = p[:, 0:tq]
    for h in range(1, nh):
        imp = imp + p[:, h * tq:(h + 1) * tq]
    sel = _select_blocks(imp, qpos[:, 0:tq], nb, score_ref)
    sel_bias = jnp.where(sel, 0.0, MASK_VALUE)
    bpt = tq // BLOCK
    for n in range(nb):
        sel_ref[n // bpt, n % bpt:n % bpt + 1, :] = sel_bias[n:n + 1, :]

    krow = lax.broadcasted_iota(jnp.int32, (tq, nh * tq), 0)
    sel_acc = _Flash(ms_ref, ls_ref, accs_ref)
    win_acc = _Flash(mw_ref, lw_ref, accw_ref)
    sel_acc.init()
    win_acc.init()

    def tile_bias(kt):
        blk = sel_ref[kt]
        rows = [jnp.broadcast_to(jnp.concatenate([blk[j:j + 1, :]] * nh, axis=1), (BLOCK, nh * tq))
                for j in range(bpt)]
        return jnp.concatenate(rows, axis=0)

    def sel_step(kt, mask=None):
        rows = pl.ds(pl.multiple_of(kt * tq, tq), tq)
        sel_acc.step(ks_ref[rows, :], vs_ref[rows, :], qt_b, bias=tile_bias(kt), mask=mask)

    def win_step(kt, mask):
        rows = pl.ds(pl.multiple_of(kt * tq, tq), tq)
        win_acc.step(kw_ref[rows, :], vw_ref[rows, :], qt_b, mask=mask)

    w_lo = jnp.maximum(qt - WINDOW // tq, 0)

    def far_body(kt, carry):
        sel_step(kt)
        return carry

    lax.fori_loop(0, w_lo, far_body, 0)

    def near_body(kt, carry):
        sel_step(kt)
        win_step(kt, (qpos - (kt * tq + krow)) < WINDOW)
        return carry

    lax.fori_loop(w_lo, qt, near_body, 0)

    causal = (qt * tq + krow) <= qpos
    sel_step(qt, causal)
    win_step(qt, causal)
    o_sel = sel_acc.result()
    o_win = win_acc.result()

    gT = jax.nn.sigmoid(g_ref[...]).T
    for h in range(nh):
        sl = slice(h * tq, (h + 1) * tq)
        y = (gT[3 * h:3 * h + 1, :] * o_cmp[:, sl] + gT[3 * h + 1:3 * h + 2, :] * o_sel[:, sl]
             + gT[3 * h + 2:3 * h + 3, :] * o_win[:, sl])
        o_ref[:, h * HEAD_DIM:(h + 1) * HEAD_DIM] = y.T.astype(o_ref.dtype)


def nsa_prompt(q, kv, win, pooled, gates, B, L):
    tq = NSA_TQ
    G, nh, hd = NSA_KV, NSA_GROUP, HEAD_DIM
    assert L % tq == 0 and tq % BLOCK == 0 and WINDOW % tq == 0
    nq = L // tq
    nb = L // BLOCK
    full = lambda off: pl.BlockSpec((L, hd), lambda b, g, t: (b, off + g))
    return pl.pallas_call(
        functools.partial(_nsa_prompt_kernel, nb=nb),
        grid=(B, G, nq),
        in_specs=[pl.BlockSpec((tq, nh * hd), lambda b, g, t: (b * nq + t, g)),
                  full(2 * G), full(3 * G),
                  pl.BlockSpec((L, hd), lambda b, g, t: (b, g)),
                  pl.BlockSpec((L, hd), lambda b, g, t: (b, G + g)),
                  pl.BlockSpec((nb, hd), lambda b, g, t: (b, g)),
                  pl.BlockSpec((nb, hd), lambda b, g, t: (b, G + g)),
                  pl.BlockSpec((tq, LANES), lambda b, g, t: (b * nq + t, g))],
        out_specs=pl.BlockSpec((tq, nh * hd), lambda b, g, t: (b * nq + t, g)),
        out_shape=jax.ShapeDtypeStruct((B * L, G * nh * hd), BF16),
        scratch_shapes=[pltpu.VMEM((nb * BLOCK // tq, tq // BLOCK, tq), F32),
                        pltpu.VMEM((nb, tq), F32)]
        + 2 * [pltpu.VMEM((1, nh * tq), F32), pltpu.VMEM((1, nh * tq), F32), pltpu.VMEM((hd, nh * tq), F32)],
        compiler_params=_cparams(("parallel", "parallel", "arbitrary")),
        name="nsa_prompt",
    )(q, kv, kv, win, win, pooled, pooled, gates)


DEC_PAGES_PER_STEP = 4
DEC_ROWS = 8


def _dec_qpos(past_len, n_new, shape):
    lane = lax.broadcasted_iota(jnp.int32, shape, len(shape) - 1)
    return past_len + lax.rem(lane & (LANES - 1), n_new)


def _page_spec(cache, layer, half, i, npg):
    return pl.BlockSpec((None, None, cache.shape[2], None) + cache.shape[4:],
                        lambda b, s, pt: (layer, pt[b, s * npg + i], 0, half, 0, 0))


def _page_rows(page_ref, r):
    n, rows, hd = page_ref.shape
    return page_ref.reshape(n * rows, hd)[pl.ds(r, n, stride=rows), :]


def _pool_paged_kernel(pt_ref, *refs):
    del pt_ref
    npg = DEC_PAGES_PER_STEP
    wl_ref, nw_ref, o_ref = refs[npg:npg + 3]
    wl = wl_ref[...]
    e = jnp.exp(wl - jnp.max(wl, axis=0, keepdims=True))
    w = e / jnp.sum(e, axis=0, keepdims=True)
    bpp = refs[0].shape[0] // BLOCK
    for i in range(npg):
        for r in range(2 * NSA_KV):
            x = _page_rows(refs[i], r)
            z = jnp.sum(x.reshape(bpp, BLOCK, HEAD_DIM) * w[None], axis=1)
            if r < NSA_KV:
                z = z * lax.rsqrt(jnp.mean(z * z, axis=-1, keepdims=True) + EPS) * nw_ref[...]
            o_ref[r, i * bpp:(i + 1) * bpp, :] = z


def pool_paged(cache, layer, page_table, pos_logits, kn_w):
    B, n_pages = page_table.shape
    page = cache.shape[2]
    npg = DEC_PAGES_PER_STEP
    assert n_pages % npg == 0 and page % BLOCK == 0
    bpp = page // BLOCK
    assert (npg * bpp) % SUBLANES == 0
    return pl.pallas_call(
        _pool_paged_kernel,
        grid_spec=pltpu.PrefetchScalarGridSpec(
            num_scalar_prefetch=1,
            grid=(B, n_pages // npg),
            in_specs=[_page_spec(cache, layer, 0, i, npg) for i in range(npg)]
            + [pl.BlockSpec((BLOCK, 1), lambda b, s, pt: (0, 0)),
               pl.BlockSpec((1, HEAD_DIM), lambda b, s, pt: (0, 0))],
            out_specs=pl.BlockSpec((None, 2 * NSA_KV, npg * bpp, HEAD_DIM), lambda b, s, pt: (b, 0, s, 0)),
        ),
        out_shape=jax.ShapeDtypeStruct((B, 2 * NSA_KV, n_pages * bpp, HEAD_DIM), F32),
        compiler_params=_cparams(("parallel", "arbitrary")),
        name="pool_paged",
    )(page_table, *([cache] * npg), pos_logits.reshape(BLOCK, 1), kn_w.reshape(1, HEAD_DIM))


def _nsa_dec_front_kernel(qbd_ref, pooled_ref, cwin_ref, nwin_ref, gl_ref, part_ref, sel_ref,
                          score_ref, m_ref, l_ref, acc_ref, *, past_len, n_new, n_cand):
    kvw = qbd_ref.shape[0]
    nl = qbd_ref.shape[1]
    qbd = qbd_ref[...]
    qbd_b = qbd.astype(BF16)
    qpos = _dec_qpos(past_len, n_new, (1, nl))

    nf = pooled_ref.shape[1]
    kc = jnp.concatenate([pooled_ref[g] for g in range(NSA_KV)], axis=1)
    vc = jnp.concatenate([pooled_ref[NSA_KV + g] for g in range(NSA_KV)], axis=1)
    s = jnp.dot(kc, qbd, preferred_element_type=F32, precision=lax.Precision.HIGHEST) * SCALE
    n_io = lax.broadcasted_iota(jnp.int32, s.shape, 0)
    ready = ((n_io + 1) * BLOCK - 1) <= qpos
    s = jnp.where(ready, s, MASK_VALUE)
    e = jnp.exp(s - jnp.max(s, axis=0, keepdims=True))
    p = e / jnp.sum(e, axis=0, keepdims=True) * jnp.where(ready, 1.0, 0.0)
    o_cmp = lax.dot_general(vc.astype(BF16), p.astype(BF16), (((0,), (0,)), ((), ())),
                            preferred_element_type=F32)
    imp = p[:, 0:LANES]
    for h in range(1, NSA_GROUP):
        imp = imp + p[:, h * LANES:(h + 1) * LANES]
    imp = jnp.concatenate([imp, jnp.zeros((score_ref.shape[0] - nf, LANES), F32)], axis=0)
    sel = _select_blocks(imp, qpos[:, 0:LANES], n_cand, score_ref)
    sel_ref[...] = jnp.where(sel, 0.0, MASK_VALUE)

    win_acc = _Flash(m_ref, l_ref, acc_ref)
    win_acc.init()
    w_rows = cwin_ref.shape[0]
    krow = lax.broadcasted_iota(jnp.int32, (w_rows, nl), 0)
    d = qpos - (past_len - w_rows + krow)
    win_acc.step(cwin_ref[:, 0:kvw], cwin_ref[:, kvw:2 * kvw], qbd_b, scale=SCALE,
                 mask=(d >= 0) & (d < WINDOW))
    nrow = lax.broadcasted_iota(jnp.int32, (nwin_ref.shape[0], nl), 0)
    d = qpos - (past_len + nrow)
    win_acc.step(nwin_ref[:, 0:kvw], nwin_ref[:, kvw:2 * kvw], qbd_b, scale=SCALE,
                 mask=(d >= 0) & (d < WINDOW) & (nrow < n_new))
    g = jax.nn.sigmoid(gl_ref[...])
    part_ref[...] = g[0:1, :] * o_cmp + g[2:3, :] * win_acc.result()


def _nsa_dec_sel_kernel(pt_ref, *refs, past_len, n_new, page):
    del pt_ref
    npg = DEC_PAGES_PER_STEP
    page_refs = refs[:npg]
    qg_ref, sel_ref, sel_new_ref, nkv_ref, part_ref, g1_ref, y_ref, m_ref, l_ref, acc_ref = refs[npg:]
    G, hd = NSA_KV, HEAD_DIM
    step = pl.program_id(1)
    chains = [_Flash(m_ref.at[g], l_ref.at[g], acc_ref.at[g]) for g in range(G)]
    qg_b = [qg_ref[g].astype(BF16) for g in range(G)]

    @pl.when(step == 0)
    def _():
        for c in chains:
            c.init()

    for g in range(G):
        bias = jnp.concatenate([jnp.broadcast_to(sel_ref[g, i][j:j + 1, :], (BLOCK, LANES))
                                for i in range(npg) for j in range(page // BLOCK)], axis=0)
        k = jnp.concatenate([_page_rows(page_refs[i], g) for i in range(npg)], axis=0)
        v = jnp.concatenate([_page_rows(page_refs[i], G + g) for i in range(npg)], axis=0)
        chains[g].step(k, v, qg_b[g], scale=SCALE, bias=bias)

    @pl.when(step == pl.num_programs(1) - 1)
    def _():
        nrow = lax.broadcasted_iota(jnp.int32, (nkv_ref.shape[0], LANES), 0)
        lane = lax.broadcasted_iota(jnp.int32, (1, LANES), 1)
        qpos = past_len + lax.rem(lane, n_new)
        mask = (past_len + nrow <= qpos) & (nrow < n_new)
        kvw = G * hd
        for g in range(G):
            chains[g].step(nkv_ref[:, g * hd:(g + 1) * hd], nkv_ref[:, kvw + g * hd:kvw + (g + 1) * hd],
                           qg_b[g], scale=SCALE, bias=sel_new_ref[g, 0][0:1, :], mask=mask)
            y = part_ref[g] + jax.nn.sigmoid(g1_ref[g][0:1, :]) * chains[g].result()
            y_ref[g] = y.T


def nsa_decode(q, kv_new, win_new, gates, cache, layer, page_table, cache_win, pos_logits, kn_w, n_new):
    B, n_pages = page_table.shape
    page = cache.shape[2]
    G, nh, hd = NSA_KV, NSA_GROUP, HEAD_DIM
    kvw = G * hd
    nl = nh * LANES
    past_len = n_pages * page
    npg = DEC_PAGES_PER_STEP
    bpp = page // BLOCK
    n_cand = -(-(past_len + n_new) // BLOCK)
    n_rows = -(-(n_cand + bpp) // SUBLANES) * SUBLANES // bpp * bpp
    assert G * n_new <= LANES and n_new <= DEC_ROWS and n_new <= BLOCK and past_len % BLOCK == 0

    q5 = q.reshape(B, n_new, G, nh, hd)
    qt = jnp.transpose(q5, (0, 2, 4, 3, 1))
    qbd = qt[:, :, :, :, None, :] * jnp.eye(G, dtype=F32)[None, :, None, None, :, None]
    qbd = jnp.pad(qbd.reshape(B, kvw, nh, G * n_new), ((0, 0), (0, 0), (0, 0), (0, LANES - G * n_new)))
    qbd = qbd.reshape(B, kvw, nl)
    gl = gates.reshape(B, n_new, G, LANES)[..., :3 * nh].reshape(B, n_new, G, nh, 3)
    gl = jnp.transpose(gl, (0, 4, 3, 2, 1)).reshape(B, 3, nh, G * n_new)
    gl = jnp.pad(gl, ((0, 0), (0, DEC_ROWS - 3), (0, 0), (0, LANES - G * n_new))).reshape(B, DEC_ROWS, nl)
    pad_rows = lambda t: jnp.pad(t.reshape(B, n_new, -1), ((0, 0), (0, DEC_ROWS - n_new), (0, 0)))
    nkv = pad_rows(kv_new)
    nwin = pad_rows(win_new)

    pooled = pool_paged(cache, layer, page_table, pos_logits, kn_w)
    nf = pooled.shape[2]
    per_b = lambda *shape: pl.BlockSpec((None,) + shape, lambda b: (b,) + (0,) * len(shape))
    part, sel = pl.pallas_call(
        functools.partial(_nsa_dec_front_kernel, past_len=past_len, n_new=n_new, n_cand=n_cand),
        grid=(B,),
        in_specs=[per_b(kvw, nl), per_b(2 * G, nf, hd), per_b(cache_win.shape[1], 2 * kvw),
                  per_b(DEC_ROWS, 2 * kvw), per_b(DEC_ROWS, nl)],
        out_specs=[per_b(kvw, nl), per_b(n_rows, LANES)],
        out_shape=[jax.ShapeDtypeStruct((B, kvw, nl), F32), jax.ShapeDtypeStruct((B, n_rows, LANES), F32)],
        scratch_shapes=[pltpu.VMEM((n_rows, LANES), F32), pltpu.VMEM((1, nl), F32),
                        pltpu.VMEM((1, nl), F32), pltpu.VMEM((kvw, nl), F32)],
        compiler_params=_cparams(("parallel",)),
        name="nsa_dec_front",
    )(qbd, pooled, cache_win, nwin, gl)

    nu = nh * n_new
    lane_pad = lambda t: jnp.pad(t, [(0, 0)] * (t.ndim - 1) + [(0, LANES - nu)])
    qg = lane_pad(qt.reshape(B, G, hd, nu))
    p6 = part.reshape(B, G, hd, nh, LANES)[..., :G * n_new].reshape(B, G, hd, nh, G, n_new)
    partg = lane_pad(jnp.stack([p6[:, g, :, :, g, :] for g in range(G)], axis=1).reshape(B, G, hd, nu))
    selg = jnp.transpose(sel[:, :, :G * n_new].reshape(B, n_rows, G, n_new), (0, 2, 1, 3))
    selg = lane_pad(jnp.tile(selg, (1, 1, 1, nh))).reshape(B, G, n_rows // bpp, bpp, LANES)
    g1 = gates.reshape(B, n_new, G, LANES)[..., :3 * nh].reshape(B, n_new, G, nh, 3)[..., 1]
    g1g = lane_pad(jnp.transpose(g1, (0, 2, 3, 1)).reshape(B, G, 1, nu))
    g1g = jnp.pad(g1g, ((0, 0), (0, 0), (0, SUBLANES - 1), (0, 0)))

    page_spec = lambda i: _page_spec(cache, layer, 1, i, npg)
    per_bs = lambda *shape: pl.BlockSpec((None,) + shape, lambda b, s, pt: (b,) + (0,) * len(shape))
    y = pl.pallas_call(
        functools.partial(_nsa_dec_sel_kernel, past_len=past_len, n_new=n_new, page=page),
        grid_spec=pltpu.PrefetchScalarGridSpec(
            num_scalar_prefetch=1,
            grid=(B, n_pages // npg),
            in_specs=[page_spec(i) for i in range(npg)]
            + [per_bs(G, hd, LANES),
               pl.BlockSpec((None, G, npg, bpp, LANES), lambda b, s, pt: (b, 0, s, 0, 0)),
               pl.BlockSpec((None, G, 1, bpp, LANES), lambda b, s, pt: (b, 0, n_pages, 0, 0)),
               pl.BlockSpec((None, DEC_ROWS, 2 * kvw), lambda b, s, pt: (b, 0, 1)),
               per_bs(G, hd, LANES),
               per_bs(G, SUBLANES, LANES)],
            out_specs=per_bs(G, LANES, hd),
            scratch_shapes=[pltpu.VMEM((G, 1, LANES), F32), pltpu.VMEM((G, 1, LANES), F32),
                            pltpu.VMEM((G, hd, LANES), F32)],
        ),
        out_shape=jax.ShapeDtypeStruct((B, G, LANES, hd), F32),
        compiler_params=_cparams(("parallel", "arbitrary")),
        name="nsa_dec_sel",
    )(page_table, *([cache] * npg), qg, selg, selg, nkv, partg, g1g)
    y = jnp.transpose(y[:, :, :nu].reshape(B, G, nh, n_new, hd), (0, 3, 1, 2, 4))
    return y.reshape(B * n_new, G * nh * hd).astype(BF16)


def _lower_bounds_kernel(x_ref, o_ref):
    x = x_ref[...]
    e = jnp.exp(x - jnp.max(x, axis=0, keepdims=True))
    p = e / jnp.sum(e, axis=0, keepdims=True)
    c = p[0:1, :]
    o_ref[0:1, :] = jnp.zeros_like(c)
    for i in range(1, x.shape[0]):
        c = c + p[i:i + 1, :]
        o_ref[i:i + 1, :] = c - p[0:1, :]


def hgrn_lower_bounds(lb_logits):
    return pl.pallas_call(
        _lower_bounds_kernel,
        out_shape=jax.ShapeDtypeStruct(lb_logits.shape, F32),
        name="hgrn_lower_bounds",
    )(lb_logits)


GATE_W = NSA_KV * LANES


def _bf16_weights(w_in, w_branch_a, w_branch_b, w_out, w_up, w_down, o_bg):
    n_bg = 3 * NSA_KV * NSA_GROUP
    w_bg = w_in[:, :, o_bg:o_bg + n_bg].reshape(w_in.shape[:2] + (NSA_KV, 3 * NSA_GROUP))
    w_bg = jnp.pad(w_bg, ((0, 0), (0, 0), (0, 0), (0, LANES - 3 * NSA_GROUP))).reshape(w_in.shape[:2] + (GATE_W,))
    w_gm = jnp.concatenate([w_bg, w_in[:, :, o_bg + n_bg:]], axis=2)
    c = lambda t: t.astype(BF16)
    return dict(inp=c(w_in), gm=c(w_gm), a=c(w_branch_a), b=c(w_branch_b), out=c(w_out), up=c(w_up),
                down=c(w_down))


def _project_in(x, wts, layer, norm1_w, q_norm_w, k_norm_w, hq, nq, kvw):
    h = rmsnorm_cast(x, norm1_w)
    o_q = 4 * hq
    o_kv = o_q + nq
    o_win = o_kv + 4 * kvw
    tile = lambda v, n: jnp.tile(v, n // HEAD_DIM).reshape(1, n)
    ones = lambda n: jnp.ones((1, n), F32)
    zeros = lambda n: jnp.zeros((1, n), F32)
    mm = functools.partial(matmul, h, wts["inp"], layer=layer, out_dtype=F32)
    za = mm(col_off=0, n=o_q, name="in_hgrn")
    q = mm(col_off=o_q, n=nq, epilogue=_headnorm_epilogue, row_extras=(tile(q_norm_w, nq), ones(nq)),
           name="in_q")
    kv_flag = jnp.concatenate([zeros(2 * kvw), ones(kvw), zeros(kvw)], axis=1)
    kv = mm(col_off=o_kv, n=4 * kvw, epilogue=_headnorm_epilogue,
            row_extras=(tile(k_norm_w[1], 4 * kvw), kv_flag), name="in_kv")
    win_flag = jnp.concatenate([ones(kvw), zeros(kvw)], axis=1)
    win = mm(col_off=o_win, n=2 * kvw, epilogue=_headnorm_epilogue,
             row_extras=(tile(k_norm_w[2], 2 * kvw), win_flag), name="in_win")
    gm = matmul(h, wts["gm"], layer=layer, out_dtype=F32, name="in_gates")
    return za, q, kv, win, gm


def _finish_layer(x, ya, yb, gm, wts, layer, norm2_w):
    mix = gated_merge(ya, yb, wts["a"], wts["b"], layer, gm, gate_off=GATE_W)
    x1 = matmul(mix, wts["out"], layer=layer, out_dtype=F32, epilogue=_residual_epilogue, tile_extras=(x,),
                name="out_proj")
    h2 = rmsnorm_cast(x1, norm2_w)
    u = matmul(h2, wts["up"], layer=layer, out_dtype=BF16, epilogue=_relu2_epilogue, name="mlp_up")
    return matmul(u, wts["down"], layer=layer, out_dtype=F32, epilogue=_residual_epilogue, tile_extras=(x1,),
                  name="mlp_down")


def kernel(x_prompt, x_sample, cache_kv, cache_win, state_hgrn, page_table, norm1_w, w_in, hgrn_lb_logits,
           hgrn_norm_w, q_norm_w, k_norm_w, cmp_pos_logits, w_branch_a, w_branch_b, w_out, norm2_w, w_up,
           w_down):
    depth = w_in.shape[0]
    B, L, D = x_prompt.shape
    Bs, Ls, _ = x_sample.shape
    H = state_hgrn.shape[2]
    hq = H * state_hgrn.shape[3]
    G, hd = cache_kv.shape[4], cache_kv.shape[5]
    kvw = G * hd
    nq = w_branch_b.shape[1]
    assert (G, hd) == (NSA_KV, HEAD_DIM) and nq == NSA_KV * NSA_GROUP * HEAD_DIM
    assert state_hgrn.shape[3] == LANES and state_hgrn.shape[4] == LANES
    lbs = hgrn_lower_bounds(hgrn_lb_logits)
    cache = cache_kv.reshape(cache_kv.shape[:3] + (2, 2 * G, hd))
    dec_chunk = 2 * SUBLANES
    assert Ls <= dec_chunk

    xp = x_prompt.reshape(B * L, D)
    xs = x_sample.reshape(Bs * Ls, D)
    kv_p, kv_s, win_p, win_s, st_p, st_s = [], [], [], [], [], []
    wts = _bf16_weights(w_in, w_branch_a, w_branch_b, w_out, w_up, w_down, 4 * hq + nq + 6 * kvw)
    for l in range(depth):
        za, q, kv, win, gm = _project_in(xp, wts, l, norm1_w[l], q_norm_w[l], k_norm_w[l], hq, nq, kvw)
        ya, st = hgrn(za, lbs[l], hgrn_norm_w[l], None, B, L, H, HGRN_CHUNK)
        pooled = pool_prompt(kv, cmp_pos_logits[l], k_norm_w[l, 0])
        yb = nsa_prompt(q, kv, win, pooled, gm, B, L)
        xp = _finish_layer(xp, ya, yb, gm, wts, l, norm2_w[l])
        kv_p.append(kv.reshape(B, L, 4, G, hd))
        wk = min(WINDOW, L)
        win_p.append(win.reshape(B, L, 2, G, hd)[:, L - wk:])
        st_p.append(st)
        za, q, kv, win, gm = _project_in(xs, wts, l, norm1_w[l], q_norm_w[l], k_norm_w[l], hq, nq, kvw)
        za_pad = jnp.pad(za.reshape(Bs, Ls, -1), ((0, 0), (0, dec_chunk - Ls), (0, 0)))
        ya, st = hgrn(za_pad.reshape(Bs * dec_chunk, -1), lbs[l], hgrn_norm_w[l], state_hgrn[l],
                      Bs, dec_chunk, H, dec_chunk, n_valid=Ls)
        ya = ya.reshape(Bs, dec_chunk, -1)[:, :Ls].reshape(Bs * Ls, -1)
        cwin = cache_win[l].reshape(Bs, cache_win.shape[2], 2 * kvw)
        yb = nsa_decode(q, kv, win, gm[:, :GATE_W], cache, l, page_table, cwin, cmp_pos_logits[l],
                        k_norm_w[l, 0], Ls)
        xs = _finish_layer(xs, ya, yb, gm, wts, l, norm2_w[l])
        kv_s.append(kv.reshape(Bs, Ls, 4, G, hd))
        win_all = jnp.concatenate([cwin, win.reshape(Bs, Ls, 2 * kvw)], axis=1)
        ws = min(WINDOW, win_all.shape[1])
        win_s.append(win_all[:, win_all.shape[1] - ws:].reshape(Bs, ws, 2, G, hd))
        st_s.append(st)
    return (xp.reshape(B, L, D), xs.reshape(Bs, Ls, D), jnp.stack(kv_p), jnp.stack(kv_s),
            jnp.stack(win_p), jnp.stack(win_s), jnp.stack(st_p).astype(state_hgrn.dtype),
            jnp.stack(st_s).astype(state_hgrn.dtype))
```

```python
import functools
import math

import jax
import jax.numpy as jnp
from jax import lax
from jax.experimental import pallas as pl
from jax.experimental.pallas import tpu as pltpu

F32 = jnp.float32
BF16 = jnp.bfloat16

LANES = 128
SUBLANES = 8
VMEM_LIMIT = 56 * 1024 * 1024

HEAD_DIM = 128
NSA_KV = 4
NSA_GROUP = 4
BLOCK = 64
N_SEL = 16
N_LOCAL = 2
WINDOW = 512
FORCE_BONUS = float(NSA_GROUP + 1)
SCALE = HEAD_DIM ** -0.5
EPS = 1e-6
MASK_VALUE = -1e30
LOG2E = math.log2(math.e)
HGRN_CHUNK = 128


def _cparams(sem):
    return pltpu.CompilerParams(dimension_semantics=sem, vmem_limit_bytes=VMEM_LIMIT)


def _pick(n, prefs):
    for p in prefs:
        if n % p == 0:
            return p
    return n


def _rmsnorm_kernel(x_ref, w_ref, o_ref):
    x = x_ref[...]
    y = x * lax.rsqrt(jnp.mean(x * x, axis=-1, keepdims=True) + EPS)
    o_ref[...] = (y * w_ref[...]).astype(o_ref.dtype)


def rmsnorm_cast(x, w):
    M, D = x.shape
    tm = _pick(M, (256, 32))
    return pl.pallas_call(
        _rmsnorm_kernel,
        grid=(M // tm,),
        in_specs=[pl.BlockSpec((tm, D), lambda i: (i, 0)),
                  pl.BlockSpec((1, D), lambda i: (0, 0))],
        out_specs=pl.BlockSpec((tm, D), lambda i: (i, 0)),
        out_shape=jax.ShapeDtypeStruct((M, D), BF16),
        compiler_params=_cparams(("parallel",)),
        name="rmsnorm_cast",
    )(x, w.reshape(1, D))


def _mm_kernel(*refs, nk, n_tile, n_row, epilogue, w_t):
    x_ref, w_ref = refs[0], refs[1]
    tile_refs = refs[2:2 + n_tile]
    row_refs = refs[2 + n_tile:2 + n_tile + n_row]
    o_ref = refs[2 + n_tile + n_row]

    def finish(acc):
        extras = [r[...] for r in tile_refs] + [r[...] for r in row_refs]
        o_ref[...] = epilogue(acc, *extras).astype(o_ref.dtype)

    if w_t:
        part = lax.dot_general(x_ref[...], w_ref[...], (((1,), (1,)), ((), ())), preferred_element_type=F32)
    else:
        part = jnp.dot(x_ref[...], w_ref[...], preferred_element_type=F32)
    if nk == 1:
        finish(part)
    else:
        acc_ref = refs[3 + n_tile + n_row]
        k = pl.program_id(2)

        @pl.when(k == 0)
        def _():
            acc_ref[...] = part

        @pl.when(k > 0)
        def _():
            acc_ref[...] += part

        @pl.when(k == nk - 1)
        def _():
            finish(acc_ref[...])


def matmul(x, w, *, out_dtype, layer=None, col_off=0, n=None, epilogue=None, tile_extras=(), row_extras=(),
           tm=None, tn=None, tk=None, w_t=False, name="matmul"):
    M, K = x.shape
    kax, nax = (-1, -2) if w_t else (-2, -1)
    assert w.shape[kax] == K and (layer is None) == (w.ndim == 2)
    N = n or w.shape[nax]
    tm = tm or _pick(M, (1024, 512, 256, 32))
    tn = tn or _pick(N, (1024, 512, 256, 128))
    tk = tk or (K if K <= 4096 else _pick(K, (2048,)))
    nk = K // tk
    assert M % tm == 0 and N % tn == 0 and K % tk == 0 and col_off % tn == 0
    joff = col_off // tn
    if epilogue is None:
        epilogue = lambda acc: acc
    w_block = (tn, tk) if w_t else (tk, tn)
    w_index = (lambda j, k: (joff + j, k)) if w_t else (lambda j, k: (k, joff + j))
    if layer is None:
        w_spec = pl.BlockSpec(w_block, lambda i, j, k: w_index(j, k))
    else:
        w_spec = pl.BlockSpec((None,) + w_block, lambda i, j, k: (layer,) + w_index(j, k))
    in_specs = [pl.BlockSpec((tm, tk), lambda i, j, k: (i, k)), w_spec]
    in_specs += [pl.BlockSpec((tm, tn), lambda i, j, k: (i, j)) for _ in tile_extras]
    in_specs += [pl.BlockSpec((1, tn), lambda i, j, k: (0, j)) for _ in row_extras]
    scratch = [pltpu.VMEM((tm, tn), F32)] if nk > 1 else []
    return pl.pallas_call(
        functools.partial(_mm_kernel, nk=nk, n_tile=len(tile_extras), n_row=len(row_extras),
                          epilogue=epilogue, w_t=w_t),
        grid=(M // tm, N // tn, nk),
        in_specs=in_specs,
        out_specs=pl.BlockSpec((tm, tn), lambda i, j, k: (i, j)),
        out_shape=jax.ShapeDtypeStruct((M, N), out_dtype),
        scratch_shapes=scratch,
        compiler_params=_cparams(("parallel", "parallel", "arbitrary")),
        name=name,
    )(x, w, *tile_extras, *row_extras)


def _mm_deep_kernel(x_ref, w_ref, r_ref, o_ref, acc_ref, *, nk):
    k = pl.program_id(1)
    j = pl.program_id(2)
    part = jnp.dot(x_ref[...], w_ref[...], preferred_element_type=F32)

    @pl.when(k == 0)
    def _():
        acc_ref[j] = part

    @pl.when((k > 0) & (k < nk - 1))
    def _():
        acc_ref[j] += part

    @pl.when(k == nk - 1)
    def _():
        o_ref[...] = r_ref[...] + (acc_ref[j] + part)


def matmul_deep_residual(x, w, layer, res):
    M, K = x.shape
    N = w.shape[2]
    tm = _pick(M, (1024, 512, 256, 32))
    tn = _pick(N, (1024, 512, 256, 128))
    tk = _pick(K, (2048, 1024, 512))
    nk = K // tk
    assert w.shape[1] == K and M % tm == 0 and N % tn == 0 and nk >= 2
    last = lambda k, j: jnp.where(k == nk - 1, j, 0)
    return pl.pallas_call(
        functools.partial(_mm_deep_kernel, nk=nk),
        grid=(M // tm, nk, N // tn),
        in_specs=[pl.BlockSpec((tm, tk), lambda i, k, j: (i, k)),
                  pl.BlockSpec((None, tk, tn), lambda i, k, j: (layer, k, j)),
                  pl.BlockSpec((tm, tn), lambda i, k, j: (i, last(k, j)))],
        out_specs=pl.BlockSpec((tm, tn), lambda i, k, j: (i, last(k, j))),
        out_shape=jax.ShapeDtypeStruct((M, N), F32),
        scratch_shapes=[pltpu.VMEM((N // tn, tm, tn), F32)],
        compiler_params=_cparams(("parallel", "arbitrary", "arbitrary")),
        name="mlp_down",
    )(x, w, res)


def _headnorm_epilogue(acc, nw, flag):
    outs = []
    for c in range(acc.shape[1] // HEAD_DIM):
        sl = slice(c * HEAD_DIM, (c + 1) * HEAD_DIM)
        z = acc[:, sl]
        zn = z * lax.rsqrt(jnp.mean(z * z, axis=-1, keepdims=True) + EPS) * nw[:, sl]
        outs.append(jnp.where(flag[:, sl] != 0.0, zn, z))
    return jnp.concatenate(outs, axis=1)


def _residual_epilogue(acc, res):
    return res + acc


def _relu2_epilogue(acc):
    r = jnp.maximum(acc, 0.0)
    return r * r


def _merge_kernel(ya_ref, yb_ref, wa_ref, wb_ref, g0_ref, g1_ref, o_ref):
    pa = jnp.dot(ya_ref[...], wa_ref[...], preferred_element_type=F32)
    pb = jnp.dot(yb_ref[...], wb_ref[...], preferred_element_type=F32)
    mix = jax.nn.sigmoid(g0_ref[...]) * pa + jax.nn.sigmoid(g1_ref[...]) * pb
    o_ref[...] = mix.astype(o_ref.dtype)


def gated_merge(ya, yb, wa, wb, layer, m_gate, gate_off=0):
    M, Ka = ya.shape
    Kb = yb.shape[1]
    D = wa.shape[2]
    tm = _pick(M, (1024, 512, 256, 32))
    tn = _pick(D, (512, 256, 128))
    nj = D // tn
    assert gate_off % tn == 0
    goff = gate_off // tn
    return pl.pallas_call(
        _merge_kernel,
        grid=(M // tm, nj),
        in_specs=[pl.BlockSpec((tm, Ka), lambda i, j: (i, 0)),
                  pl.BlockSpec((tm, Kb), lambda i, j: (i, 0)),
                  pl.BlockSpec((None, Ka, tn), lambda i, j: (layer, 0, j)),
                  pl.BlockSpec((None, Kb, tn), lambda i, j: (layer, 0, j)),
                  pl.BlockSpec((tm, tn), lambda i, j: (i, goff + j)),
                  pl.BlockSpec((tm, tn), lambda i, j: (i, goff + nj + j))],
        out_specs=pl.BlockSpec((tm, tn), lambda i, j: (i, j)),
        out_shape=jax.ShapeDtypeStruct((M, D), BF16),
        compiler_params=_cparams(("parallel", "parallel")),
        name="gated_merge",
    )(ya, yb, wa, wb, m_gate, m_gate)


def _silu(x):
    return x * jax.nn.sigmoid(x)


def _pair_total(cm, m, t_io):
    C = cm.shape[0]
    if m == 1:
        return jnp.where((t_io & 1) != 0, pltpu.roll(cm, 1, 0), cm)
    if m == 2:
        j = t_io & 3
        return jnp.where(j == 0, pltpu.roll(cm, C - 1, 0),
                         jnp.where(j == 1, cm,
                                   jnp.where(j == 2, pltpu.roll(cm, 1, 0), pltpu.roll(cm, 2, 0))))
    x3 = cm.reshape(C // (2 * m), 2 * m, cm.shape[1])
    return jnp.broadcast_to(x3[:, m - 1:m, :], x3.shape).reshape(cm.shape)


def _hgrn_chunk(aq, af, ai, ag, lb, nw, st, n_valid):
    C = aq.shape[0]
    t_io = lax.broadcasted_iota(jnp.int32, (C, LANES), 0)
    row_io = lax.broadcasted_iota(jnp.int32, (C, C), 0)
    col_io = lax.broadcasted_iota(jnp.int32, (C, C), 1)
    split = jnp.where(row_io > col_io, row_io ^ col_io, 0)
    for sh in (1, 2, 4, 8, 16):
        split = split | (split >> sh)
    split = split - (split >> 1)
    f = lb + (1.0 - lb) * jax.nn.sigmoid(af)
    if n_valid < C:
        f = jnp.where(t_io < n_valid, f, 1.0)
    g = jnp.log(f)
    kk = 1.0 - f
    qq = _silu(aq)
    v = ai.astype(BF16)
    a = jnp.zeros((C, C), F32)
    cm = g
    m = 1
    while m < C:
        tot = _pair_total(cm, m, t_io)
        odd = (t_io & m) != 0
        z = (jnp.where(odd, qq, kk) * jnp.exp(jnp.where(odd, cm, tot - cm))).astype(BF16)
        p = lax.dot_general(z, z, (((1,), (1,)), ((), ())), preferred_element_type=F32)
        a = a + jnp.where(split == m, p, 0.0)
        cm = cm + jnp.where(odd, tot, 0.0)
        m *= 2
    b = cm
    d = jnp.sum(qq * kk, axis=-1, keepdims=True)
    a = jnp.where(row_io == col_io, d, a)
    o_intra = jnp.dot(a.astype(BF16), v, preferred_element_type=F32)
    qe = (qq * jnp.exp(b)).astype(BF16)
    o_inter = lax.dot_general(qe, st.astype(BF16), (((1,), (1,)), ((), ())), preferred_element_type=F32)
    b_end = b[C - 1:C, :]
    ku = (kk * jnp.exp(b_end - b)).astype(BF16)
    ut = lax.dot_general(v, ku, (((0,), (0,)), ((), ())), preferred_element_type=F32)
    st_new = st * jnp.exp(b_end) + ut
    o = o_inter + o_intra
    on = o * lax.rsqrt(jnp.mean(o * o, axis=-1, keepdims=True) + EPS) * nw
    return on * _silu(ag), st_new


def _hgrn_kernel(*refs, n_sub, chunk, n_valid, has_s0):
    aq_ref, af_ref, ai_ref, ag_ref, lb_ref, nw_ref = refs[:6]
    s0_ref = refs[6] if has_s0 else None
    y_ref, s_ref, st_ref = refs[6 + has_s0:]
    c = pl.program_id(2)

    @pl.when(c == 0)
    def _():
        st_ref[...] = s0_ref[...].T if has_s0 else jnp.zeros_like(st_ref)

    lb = lb_ref[...]
    nw = nw_ref[...]
    for j in range(n_sub):
        rows = pl.ds(j * chunk, chunk)
        y, st_new = _hgrn_chunk(aq_ref[rows, :], af_ref[rows, :], ai_ref[rows, :], ag_ref[rows, :],
                                lb, nw, st_ref[...], n_valid)
        st_ref[...] = st_new
        y_ref[rows, :] = y.astype(y_ref.dtype)

    @pl.when(c == pl.num_programs(2) - 1)
    def _():
        s_ref[...] = st_ref[...].T


def hgrn(za, lb, nw, s0, B, L, H, chunk, n_valid=None):
    dk = LANES
    assert L % chunk == 0 and (n_valid is None or L == chunk)
    n_sub = _pick(L // chunk, (4, 2, 1))
    tc = n_sub * chunk
    nc = L // tc
    row = lambda off: pl.BlockSpec((tc, dk), lambda b, h, c: (b * nc + c, off + h))
    state = pl.BlockSpec((None, None, dk, dk), lambda b, h, c: (b, h, 0, 0))
    has_s0 = s0 is not None
    return pl.pallas_call(
        functools.partial(_hgrn_kernel, n_sub=n_sub, chunk=chunk,
                          n_valid=chunk if n_valid is None else n_valid, has_s0=has_s0),
        grid=(B, H, nc),
        in_specs=[row(0), row(H), row(2 * H), row(3 * H),
                  pl.BlockSpec((1, dk), lambda b, h, c: (0, h)),
                  pl.BlockSpec((1, dk), lambda b, h, c: (0, 0))] + ([state] if has_s0 else []),
        out_specs=[pl.BlockSpec((tc, dk), lambda b, h, c: (b * nc + c, h)), state],
        out_shape=[jax.ShapeDtypeStruct((B * L, H * dk), BF16),
                   jax.ShapeDtypeStruct((B, H, dk, dk), F32)],
        scratch_shapes=[pltpu.VMEM((dk, dk), F32)],
        compiler_params=_cparams(("parallel", "parallel", "arbitrary")),
        name="hgrn",
    )(za, za, za, za, lb.reshape(1, -1), nw.reshape(1, dk), *([s0] if has_s0 else []))


def hgrn_prompt(za, lb, nw, B, L, H):
    return hgrn(za, lb, nw, None, B, L, H, HGRN_CHUNK)


def _pool_rows(x, wl, nw):
    wl = wl - jnp.max(wl, axis=0, keepdims=True)
    e = jnp.exp(wl)
    w = e / jnp.sum(e, axis=0, keepdims=True)
    R = x.shape[0] // BLOCK
    pooled = jnp.sum(x.reshape(R, BLOCK, x.shape[1]) * w[None], axis=1)
    half = x.shape[1] // 2
    outs = []
    for c in range(x.shape[1] // HEAD_DIM):
        z = pooled[:, c * HEAD_DIM:(c + 1) * HEAD_DIM]
        if c * HEAD_DIM < half:
            z = z * lax.rsqrt(jnp.mean(z * z, axis=-1, keepdims=True) + EPS) * nw
        outs.append(z)
    return jnp.concatenate(outs, axis=1)


def _pool_kernel(x_ref, wl_ref, nw_ref, o_ref):
    o_ref[...] = _pool_rows(x_ref[...], wl_ref[...], nw_ref[...])


def pool_prompt(kv, pos_logits, kn_w):
    M = kv.shape[0]
    kvw2 = kv.shape[1] // 2
    rb = SUBLANES * BLOCK
    assert M % rb == 0
    return pl.pallas_call(
        _pool_kernel,
        grid=(M // rb,),
        in_specs=[pl.BlockSpec((rb, kvw2), lambda i: (i, 0)),
                  pl.BlockSpec((BLOCK, 1), lambda i: (0, 0)),
                  pl.BlockSpec((1, HEAD_DIM), lambda i: (0, 0))],
        out_specs=pl.BlockSpec((SUBLANES, kvw2), lambda i: (i, 0)),
        out_shape=jax.ShapeDtypeStruct((M // BLOCK, kvw2), F32),
        compiler_params=_cparams(("parallel",)),
        name="pool_prompt",
    )(kv, pos_logits.reshape(BLOCK, 1), kn_w.reshape(1, HEAD_DIM))


NSA_TQ = 256


def _select_blocks(imp, qpos, n_cand, score_ref):
    cand = lax.broadcasted_iota(jnp.int32, imp.shape, 0)
    cur = qpos // BLOCK
    valid = cand <= cur
    forced = (cand == 0) | (cand > cur - N_LOCAL)
    score = jnp.where(valid, imp + jnp.where(forced, FORCE_BONUS, 0.0), -jnp.inf)
    score_ref[...] = score

    def body(m, rank):
        row = score_ref[pl.ds(m, 1), :]
        beats = (row > score) | ((row == score) & (cand > m))
        return rank + jnp.where(beats, 1.0, 0.0)

    rank = lax.fori_loop(0, n_cand, body, jnp.zeros(imp.shape, F32), unroll=n_cand <= 32)
    return valid & (rank < float(N_SEL))


class _Flash:
    def __init__(self, m_ref, l_ref, acc_ref):
        self.m_ref, self.l_ref, self.acc_ref = m_ref, l_ref, acc_ref

    def init(self):
        self.m_ref[...] = jnp.full(self.m_ref.shape, MASK_VALUE, F32)
        self.l_ref[...] = jnp.zeros(self.l_ref.shape, F32)
        self.acc_ref[...] = jnp.zeros(self.acc_ref.shape, F32)

    def step(self, k, v, qt_b, *, scale=None, bias=None, mask=None):
        s = jnp.dot(k.astype(BF16), qt_b, preferred_element_type=F32)
        if scale is not None:
            s = s * scale
        if bias is not None:
            s = s + bias
        if mask is not None:
            s = jnp.where(mask, s, MASK_VALUE)
        m_old = self.m_ref[...]
        m_new = jnp.maximum(m_old, jnp.max(s, axis=0, keepdims=True))
        alpha = jnp.exp2(m_old - m_new)
        p = jnp.exp2(s - m_new)
        self.l_ref[...] = alpha * self.l_ref[...] + jnp.sum(p, axis=0, keepdims=True)
        pv = lax.dot_general(v.astype(BF16), p.astype(BF16), (((0,), (0,)), ((), ())),
                             preferred_element_type=F32)
        self.acc_ref[...] = alpha * self.acc_ref[...] + pv
        self.m_ref[...] = m_new

    def result(self):
        return self.acc_ref[...] * (1.0 / self.l_ref[...])


def _nsa_prompt_kernel(q_ref, ks_ref, vs_ref, kw_ref, vw_ref, kc_ref, vc_ref, g_ref, o_ref,
                       sel_ref, score_ref, ms_ref, ls_ref, accs_ref, mw_ref, lw_ref, accw_ref, *, nb):
    tq = NSA_TQ
    qt = pl.program_id(2)
    nh = NSA_GROUP
    q = q_ref[...] * SCALE
    qT = jnp.concatenate([q[:, h * HEAD_DIM:(h + 1) * HEAD_DIM].T for h in range(nh)], axis=1)
    qt_b = (qT * LOG2E).astype(BF16)
    lane = lax.broadcasted_iota(jnp.int32, (1, nh * tq), 1)
    qpos = qt * tq + (lane & (tq - 1))

    s = jnp.dot(kc_ref[...], qT, preferred_element_type=F32, precision=lax.Precision.HIGHEST)
    n_io = lax.broadcasted_iota(jnp.int32, s.shape, 0)
    ready = ((n_io + 1) * BLOCK - 1) <= qpos
    s = jnp.where(ready, s, MASK_VALUE)
    e = jnp.exp(s - jnp.max(s, axis=0, keepdims=True))
    p = e / jnp.sum(e, axis=0, keepdims=True) * jnp.where(ready, 1.0, 0.0)
    o_cmp = lax.dot_general(vc_ref[...].astype(BF16), p.astype(BF16), (((0,), (0,)), ((), ())),
                            preferred_element_type=F32)
    imp = p[:, 0:tq]
    for h in range(1, nh):
        imp = imp + p[:, h * tq:(h + 1) * tq]
    sel = _select_blocks(imp, qpos[:, 0:tq], nb, score_ref)
    sel_bias = jnp.where(sel, 0.0, MASK_VALUE)
    bpt = tq // BLOCK
    for n in range(nb):
        sel_ref[n // bpt, n % bpt:n % bpt + 1, :] = sel_bias[n:n + 1, :]

    krow = lax.broadcasted_iota(jnp.int32, (tq, nh * tq), 0)
    sel_acc = _Flash(ms_ref, ls_ref, accs_ref)
    win_acc = _Flash(mw_ref, lw_ref, accw_ref)
    sel_acc.init()
    win_acc.init()

    def tile_bias(kt):
        blk = sel_ref[kt]
        rows = [jnp.broadcast_to(jnp.concatenate([blk[j:j + 1, :]] * nh, axis=1), (BLOCK, nh * tq))
                for j in range(bpt)]
        return jnp.concatenate(rows, axis=0)

    def sel_step(kt, mask=None):
        rows = pl.ds(pl.multiple_of(kt * tq, tq), tq)
        sel_acc.step(ks_ref[rows, :], vs_ref[rows, :], qt_b, bias=tile_bias(kt), mask=mask)

    def win_step(kt, mask):
        rows = pl.ds(pl.multiple_of(kt * tq, tq), tq)
        win_acc.step(kw_ref[rows, :], vw_ref[rows, :], qt_b, mask=mask)

    w_lo = jnp.maximum(qt - WINDOW // tq, 0)

    def far_body(kt, carry):
        sel_step(kt)
        return carry

    lax.fori_loop(0, w_lo, far_body, 0)

    def near_body(kt, carry):
        sel_step(kt)
        win_step(kt, (qpos - (kt * tq + krow)) < WINDOW)
        return carry

    lax.fori_loop(w_lo, qt, near_body, 0)

    causal = (qt * tq + krow) <= qpos
    sel_step(qt, causal)
    win_step(qt, causal)
    o_sel = sel_acc.result()
    o_win = win_acc.result()

    gT = jax.nn.sigmoid(g_ref[...]).T
    for h in range(nh):
        sl = slice(h * tq, (h + 1) * tq)
        y = (gT[3 * h:3 * h + 1, :] * o_cmp[:, sl] + gT[3 * h + 1:3 * h + 2, :] * o_sel[:, sl]
             + gT[3 * h + 2:3 * h + 3, :] * o_win[:, sl])
        o_ref[:, h * HEAD_DIM:(h + 1) * HEAD_DIM] = y.T.astype(o_ref.dtype)


def nsa_prompt(q, kv, win, pooled, gates, B, L):
    tq = NSA_TQ
    G, nh, hd = NSA_KV, NSA_GROUP, HEAD_DIM
    assert L % tq == 0 and tq % BLOCK == 0 and WINDOW % tq == 0
    nq = L // tq
    nb = L // BLOCK
    full = lambda off: pl.BlockSpec((L, hd), lambda b, g, t: (b, off + g))
    return pl.pallas_call(
        functools.partial(_nsa_prompt_kernel, nb=nb),
        grid=(B, G, nq),
        in_specs=[pl.BlockSpec((tq, nh * hd), lambda b, g, t: (b * nq + t, g)),
                  full(2 * G), full(3 * G),
                  pl.BlockSpec((L, hd), lambda b, g, t: (b, g)),
                  pl.BlockSpec((L, hd), lambda b, g, t: (b, G + g)),
                  pl.BlockSpec((nb, hd), lambda b, g, t: (b, g)),
                  pl.BlockSpec((nb, hd), lambda b, g, t: (b, G + g)),
                  pl.BlockSpec((tq, LANES), lambda b, g, t: (b * nq + t, g))],
        out_specs=pl.BlockSpec((tq, nh * hd), lambda b, g, t: (b * nq + t, g)),
        out_shape=jax.ShapeDtypeStruct((B * L, G * nh * hd), BF16),
        scratch_shapes=[pltpu.VMEM((nb * BLOCK // tq, tq // BLOCK, tq), F32),
                        pltpu.VMEM((nb, tq), F32)]
        + 2 * [pltpu.VMEM((1, nh * tq), F32), pltpu.VMEM((1, nh * tq), F32), pltpu.VMEM((hd, nh * tq), F32)],
        compiler_params=_cparams(("parallel", "parallel", "arbitrary")),
        name="nsa_prompt",
    )(q, kv, kv, win, win, pooled, pooled, gates)


DEC_PAGES_PER_STEP = 8
DEC_ROWS = 8


def _dec_qpos(past_len, n_new, shape):
    lane = lax.broadcasted_iota(jnp.int32, shape, len(shape) - 1)
    return past_len + lax.rem(lane & (LANES - 1), n_new)


def _page_spec(cache, layer, half, i, npg):
    return pl.BlockSpec((None, None, cache.shape[2], None) + cache.shape[4:],
                        lambda b, s, pt: (layer, pt[b, s * npg + i], 0, half, 0, 0))


def _page_rows(page_ref, r):
    n, rows, hd = page_ref.shape
    return page_ref.reshape(n * rows, hd)[pl.ds(r, n, stride=rows), :]


def _pool_paged_kernel(pt_ref, *refs):
    del pt_ref
    npg = DEC_PAGES_PER_STEP
    wl_ref, nw_ref, o_ref = refs[npg:npg + 3]
    wl = wl_ref[...]
    e = jnp.exp(wl - jnp.max(wl, axis=0, keepdims=True))
    w = e / jnp.sum(e, axis=0, keepdims=True)
    bpp = refs[0].shape[0] // BLOCK
    for i in range(npg):
        for r in range(2 * NSA_KV):
            x = _page_rows(refs[i], r)
            z = jnp.sum(x.reshape(bpp, BLOCK, HEAD_DIM) * w[None], axis=1)
            if r < NSA_KV:
                z = z * lax.rsqrt(jnp.mean(z * z, axis=-1, keepdims=True) + EPS) * nw_ref[...]
            o_ref[r, i * bpp:(i + 1) * bpp, :] = z


def pool_paged(cache, layer, page_table, pos_logits, kn_w):
    B, n_pages = page_table.shape
    page = cache.shape[2]
    npg = DEC_PAGES_PER_STEP
    assert n_pages % npg == 0 and page % BLOCK == 0
    bpp = page // BLOCK
    assert (npg * bpp) % SUBLANES == 0
    return pl.pallas_call(
        _pool_paged_kernel,
        grid_spec=pltpu.PrefetchScalarGridSpec(
            num_scalar_prefetch=1,
            grid=(B, n_pages // npg),
            in_specs=[_page_spec(cache, layer, 0, i, npg) for i in range(npg)]
            + [pl.BlockSpec((BLOCK, 1), lambda b, s, pt: (0, 0)),
               pl.BlockSpec((1, HEAD_DIM), lambda b, s, pt: (0, 0))],
            out_specs=pl.BlockSpec((None, 2 * NSA_KV, npg * bpp, HEAD_DIM), lambda b, s, pt: (b, 0, s, 0)),
        ),
        out_shape=jax.ShapeDtypeStruct((B, 2 * NSA_KV, n_pages * bpp, HEAD_DIM), F32),
        compiler_params=_cparams(("parallel", "arbitrary")),
        name="pool_paged",
    )(page_table, *([cache] * npg), pos_logits.reshape(BLOCK, 1), kn_w.reshape(1, HEAD_DIM))


def _nsa_dec_front_kernel(qbd_ref, pooled_ref, cwin_ref, nwin_ref, gl_ref, part_ref, sel_ref,
                          score_ref, m_ref, l_ref, acc_ref, *, past_len, n_new, n_cand):
    kvw = qbd_ref.shape[0]
    nl = qbd_ref.shape[1]
    qbd = qbd_ref[...]
    qbd_b = qbd.astype(BF16)
    qpos = _dec_qpos(past_len, n_new, (1, nl))

    nf = pooled_ref.shape[1]
    kc = jnp.concatenate([pooled_ref[g] for g in range(NSA_KV)], axis=1)
    vc = jnp.concatenate([pooled_ref[NSA_KV + g] for g in range(NSA_KV)], axis=1)
    s = jnp.dot(kc, qbd, preferred_element_type=F32, precision=lax.Precision.HIGHEST) * SCALE
    n_io = lax.broadcasted_iota(jnp.int32, s.shape, 0)
    ready = ((n_io + 1) * BLOCK - 1) <= qpos
    s = jnp.where(ready, s, MASK_VALUE)
    e = jnp.exp(s - jnp.max(s, axis=0, keepdims=True))
    p = e / jnp.sum(e, axis=0, keepdims=True) * jnp.where(ready, 1.0, 0.0)
    o_cmp = lax.dot_general(vc.astype(BF16), p.astype(BF16), (((0,), (0,)), ((), ())),
                            preferred_element_type=F32)
    imp = p[:, 0:LANES]
    for h in range(1, NSA_GROUP):
        imp = imp + p[:, h * LANES:(h + 1) * LANES]
    imp = jnp.concatenate([imp, jnp.zeros((score_ref.shape[0] - nf, LANES), F32)], axis=0)
    sel = _select_blocks(imp, qpos[:, 0:LANES], n_cand, score_ref)
    sel_ref[...] = jnp.where(sel, 0.0, MASK_VALUE)

    win_acc = _Flash(m_ref, l_ref, acc_ref)
    win_acc.init()
    w_rows = cwin_ref.shape[0]
    krow = lax.broadcasted_iota(jnp.int32, (w_rows, nl), 0)
    d = qpos - (past_len - w_rows + krow)
    win_acc.step(cwin_ref[:, 0:kvw], cwin_ref[:, kvw:2 * kvw], qbd_b, scale=SCALE * LOG2E,
                 mask=(d >= 0) & (d < WINDOW))
    nrow = lax.broadcasted_iota(jnp.int32, (nwin_ref.shape[0], nl), 0)
    d = qpos - (past_len + nrow)
    win_acc.step(nwin_ref[:, 0:kvw], nwin_ref[:, kvw:2 * kvw], qbd_b, scale=SCALE * LOG2E,
                 mask=(d >= 0) & (d < WINDOW) & (nrow < n_new))
    g = jax.nn.sigmoid(gl_ref[...])
    part_ref[...] = g[0:1, :] * o_cmp + g[2:3, :] * win_acc.result()


def _nsa_dec_sel_kernel(pt_ref, *refs, past_len, n_new, page):
    del pt_ref
    npg = DEC_PAGES_PER_STEP
    page_refs = refs[:npg]
    qg_ref, sel_ref, sel_new_ref, nkv_ref, part_ref, g1_ref, y_ref, m_ref, l_ref, acc_ref = refs[npg:]
    G, hd = NSA_KV, HEAD_DIM
    step = pl.program_id(1)
    chains = [_Flash(m_ref.at[g], l_ref.at[g], acc_ref.at[g]) for g in range(G)]
    qg_b = [qg_ref[g].astype(BF16) for g in range(G)]

    @pl.when(step == 0)
    def _():
        for c in chains:
            c.init()

    for g in range(G):
        bias = jnp.concatenate([jnp.broadcast_to(sel_ref[g, i][j:j + 1, :], (BLOCK, LANES))
                                for i in range(npg) for j in range(page // BLOCK)], axis=0)
        k = jnp.concatenate([_page_rows(page_refs[i], g) for i in range(npg)], axis=0)
        v = jnp.concatenate([_page_rows(page_refs[i], G + g) for i in range(npg)], axis=0)
        chains[g].step(k, v, qg_b[g], scale=SCALE * LOG2E, bias=bias)

    @pl.when(step == pl.num_programs(1) - 1)
    def _():
        nrow = lax.broadcasted_iota(jnp.int32, (nkv_ref.shape[0], LANES), 0)
        lane = lax.broadcasted_iota(jnp.int32, (1, LANES), 1)
        qpos = past_len + lax.rem(lane, n_new)
        mask = (past_len + nrow <= qpos) & (nrow < n_new)
        kvw = G * hd
        for g in range(G):
            chains[g].step(nkv_ref[:, g * hd:(g + 1) * hd], nkv_ref[:, kvw + g * hd:kvw + (g + 1) * hd],
                           qg_b[g], scale=SCALE * LOG2E, bias=sel_new_ref[g, 0][0:1, :], mask=mask)
            y = part_ref[g] + jax.nn.sigmoid(g1_ref[g][0:1, :]) * chains[g].result()
            y_ref[g] = y.T


def nsa_decode(q, kv_new, win_new, gates, cache, layer, page_table, cache_win, pos_logits, kn_w, n_new):
    B, n_pages = page_table.shape
    page = cache.shape[2]
    G, nh, hd = NSA_KV, NSA_GROUP, HEAD_DIM
    kvw = G * hd
    nl = nh * LANES
    past_len = n_pages * page
    npg = DEC_PAGES_PER_STEP
    bpp = page // BLOCK
    n_cand = -(-(past_len + n_new) // BLOCK)
    n_rows = -(-(n_cand + bpp) // SUBLANES) * SUBLANES // bpp * bpp
    assert G * n_new <= LANES and n_new <= DEC_ROWS and n_new <= BLOCK and past_len % BLOCK == 0

    q5 = q.reshape(B, n_new, G, nh, hd)
    qt = jnp.transpose(q5, (0, 2, 4, 3, 1))
    qbd = qt[:, :, :, :, None, :] * jnp.eye(G, dtype=F32)[None, :, None, None, :, None]
    qbd = jnp.pad(qbd.reshape(B, kvw, nh, G * n_new), ((0, 0), (0, 0), (0, 0), (0, LANES - G * n_new)))
    qbd = qbd.reshape(B, kvw, nl)
    gl = gates.reshape(B, n_new, G, LANES)[..., :3 * nh].reshape(B, n_new, G, nh, 3)
    gl = jnp.transpose(gl, (0, 4, 3, 2, 1)).reshape(B, 3, nh, G * n_new)
    gl = jnp.pad(gl, ((0, 0), (0, DEC_ROWS - 3), (0, 0), (0, LANES - G * n_new))).reshape(B, DEC_ROWS, nl)
    pad_rows = lambda t: jnp.pad(t.reshape(B, n_new, -1), ((0, 0), (0, DEC_ROWS - n_new), (0, 0)))
    nkv = pad_rows(kv_new)
    nwin = pad_rows(win_new)

    pooled = pool_paged(cache, layer, page_table, pos_logits, kn_w)
    nf = pooled.shape[2]
    per_b = lambda *shape: pl.BlockSpec((None,) + shape, lambda b: (b,) + (0,) * len(shape))
    part, sel = pl.pallas_call(
        functools.partial(_nsa_dec_front_kernel, past_len=past_len, n_new=n_new, n_cand=n_cand),
        grid=(B,),
        in_specs=[per_b(kvw, nl), per_b(2 * G, nf, hd), per_b(cache_win.shape[1], 2 * kvw),
                  per_b(DEC_ROWS, 2 * kvw), per_b(DEC_ROWS, nl)],
        out_specs=[per_b(kvw, nl), per_b(n_rows, LANES)],
        out_shape=[jax.ShapeDtypeStruct((B, kvw, nl), F32), jax.ShapeDtypeStruct((B, n_rows, LANES), F32)],
        scratch_shapes=[pltpu.VMEM((n_rows, LANES), F32), pltpu.VMEM((1, nl), F32),
                        pltpu.VMEM((1, nl), F32), pltpu.VMEM((kvw, nl), F32)],
        compiler_params=_cparams(("parallel",)),
        name="nsa_dec_front",
    )(qbd, pooled, cache_win, nwin, gl)

    nu = nh * n_new
    lane_pad = lambda t: jnp.pad(t, [(0, 0)] * (t.ndim - 1) + [(0, LANES - nu)])
    qg = lane_pad(qt.reshape(B, G, hd, nu))
    p6 = part.reshape(B, G, hd, nh, LANES)[..., :G * n_new].reshape(B, G, hd, nh, G, n_new)
    partg = lane_pad(jnp.stack([p6[:, g, :, :, g, :] for g in range(G)], axis=1).reshape(B, G, hd, nu))
    selg = jnp.transpose(sel[:, :, :G * n_new].reshape(B, n_rows, G, n_new), (0, 2, 1, 3))
    selg = lane_pad(jnp.tile(selg, (1, 1, 1, nh))).reshape(B, G, n_rows // bpp, bpp, LANES)
    g1 = gates.reshape(B, n_new, G, LANES)[..., :3 * nh].reshape(B, n_new, G, nh, 3)[..., 1]
    g1g = lane_pad(jnp.transpose(g1, (0, 2, 3, 1)).reshape(B, G, 1, nu))
    g1g = jnp.pad(g1g, ((0, 0), (0, 0), (0, SUBLANES - 1), (0, 0)))

    page_spec = lambda i: _page_spec(cache, layer, 1, i, npg)
    per_bs = lambda *shape: pl.BlockSpec((None,) + shape, lambda b, s, pt: (b,) + (0,) * len(shape))
    y = pl.pallas_call(
        functools.partial(_nsa_dec_sel_kernel, past_len=past_len, n_new=n_new, page=page),
        grid_spec=pltpu.PrefetchScalarGridSpec(
            num_scalar_prefetch=1,
            grid=(B, n_pages // npg),
            in_specs=[page_spec(i) for i in range(npg)]
            + [per_bs(G, hd, LANES),
               pl.BlockSpec((None, G, npg, bpp, LANES), lambda b, s, pt: (b, 0, s, 0, 0)),
               pl.BlockSpec((None, G, 1, bpp, LANES), lambda b, s, pt: (b, 0, n_pages, 0, 0)),
               pl.BlockSpec((None, DEC_ROWS, 2 * kvw), lambda b, s, pt: (b, 0, 1)),
               per_bs(G, hd, LANES),
               per_bs(G, SUBLANES, LANES)],
            out_specs=per_bs(G, LANES, hd),
            scratch_shapes=[pltpu.VMEM((G, 1, LANES), F32), pltpu.VMEM((G, 1, LANES), F32),
                            pltpu.VMEM((G, hd, LANES), F32)],
        ),
        out_shape=jax.ShapeDtypeStruct((B, G, LANES, hd), F32),
        compiler_params=_cparams(("parallel", "arbitrary")),
        name="nsa_dec_sel",
    )(page_table, *([cache] * npg), qg, selg, selg, nkv, partg, g1g)
    y = jnp.transpose(y[:, :, :nu].reshape(B, G, nh, n_new, hd), (0, 3, 1, 2, 4))
    return y.reshape(B * n_new, G * nh * hd).astype(BF16)


def _lower_bounds_kernel(x_ref, o_ref):
    x = x_ref[...]
    e = jnp.exp(x - jnp.max(x, axis=0, keepdims=True))
    p = e / jnp.sum(e, axis=0, keepdims=True)
    c = p[0:1, :]
    o_ref[0:1, :] = jnp.zeros_like(c)
    for i in range(1, x.shape[0]):
        c = c + p[i:i + 1, :]
        o_ref[i:i + 1, :] = c - p[0:1, :]


def hgrn_lower_bounds(lb_logits):
    return pl.pallas_call(
        _lower_bounds_kernel,
        out_shape=jax.ShapeDtypeStruct(lb_logits.shape, F32),
        name="hgrn_lower_bounds",
    )(lb_logits)


GATE_W = NSA_KV * LANES


def _bf16_weights(w_in, w_branch_a, w_branch_b, w_out, w_up, w_down, o_bg):
    n_bg = 3 * NSA_KV * NSA_GROUP
    depth, K, _ = w_in.shape
    w_t = jnp.swapaxes(w_in, 1, 2)
    w_bg = w_t[:, o_bg:o_bg + n_bg].reshape(depth, NSA_KV, 3 * NSA_GROUP, K)
    w_bg = jnp.pad(w_bg, ((0, 0), (0, 0), (0, LANES - 3 * NSA_GROUP), (0, 0))).reshape(depth, GATE_W, K)
    w_gm = jnp.concatenate([w_bg, w_t[:, o_bg + n_bg:]], axis=1)
    c = lambda t: t.astype(BF16)
    return dict(inp=c(w_t), gm=c(w_gm), a=c(w_branch_a), b=c(w_branch_b), out=c(w_out), up=c(w_up),
                down=c(w_down))


def _project_in(x, wts, layer, norm1_w, q_norm_w, k_norm_w, hq, nq, kvw):
    h = rmsnorm_cast(x, norm1_w)
    o_q = 4 * hq
    o_kv = o_q + nq
    o_win = o_kv + 4 * kvw
    tile = lambda v, n: jnp.tile(v, n // HEAD_DIM).reshape(1, n)
    ones = lambda n: jnp.ones((1, n), F32)
    zeros = lambda n: jnp.zeros((1, n), F32)
    mm = functools.partial(matmul, h, wts["inp"], layer=layer, out_dtype=F32, w_t=True)
    za = mm(col_off=0, n=o_q, name="in_hgrn")
    q = mm(col_off=o_q, n=nq, epilogue=_headnorm_epilogue, row_extras=(tile(q_norm_w, nq), ones(nq)),
           name="in_q")
    kv_flag = jnp.concatenate([zeros(2 * kvw), ones(kvw), zeros(kvw)], axis=1)
    kv = mm(col_off=o_kv, n=4 * kvw, epilogue=_headnorm_epilogue,
            row_extras=(tile(k_norm_w[1], 4 * kvw), kv_flag), name="in_kv")
    win_flag = jnp.concatenate([ones(kvw), zeros(kvw)], axis=1)
    win = mm(col_off=o_win, n=2 * kvw, epilogue=_headnorm_epilogue,
             row_extras=(tile(k_norm_w[2], 2 * kvw), win_flag), name="in_win")
    gm = matmul(h, wts["gm"], layer=layer, out_dtype=F32, w_t=True, name="in_gates")
    return za, q, kv, win, gm


def _finish_layer(x, ya, yb, gm, wts, layer, norm2_w):
    mix = gated_merge(ya, yb, wts["a"], wts["b"], layer, gm, gate_off=GATE_W)
    x1 = matmul(mix, wts["out"], layer=layer, out_dtype=F32, epilogue=_residual_epilogue, tile_extras=(x,),
                name="out_proj")
    h2 = rmsnorm_cast(x1, norm2_w)
    u = matmul(h2, wts["up"], layer=layer, out_dtype=BF16, epilogue=_relu2_epilogue, name="mlp_up")
    return matmul_deep_residual(u, wts["down"], layer, x1)


def kernel(x_prompt, x_sample, cache_kv, cache_win, state_hgrn, page_table, norm1_w, w_in, hgrn_lb_logits,
           hgrn_norm_w, q_norm_w, k_norm_w, cmp_pos_logits, w_branch_a, w_branch_b, w_out, norm2_w, w_up,
           w_down):
    depth = w_in.shape[0]
    B, L, D = x_prompt.shape
    Bs, Ls, _ = x_sample.shape
    H = state_hgrn.shape[2]
    hq = H * state_hgrn.shape[3]
    G, hd = cache_kv.shape[4], cache_kv.shape[5]
    kvw = G * hd
    nq = w_branch_b.shape[1]
    assert (G, hd) == (NSA_KV, HEAD_DIM) and nq == NSA_KV * NSA_GROUP * HEAD_DIM
    assert state_hgrn.shape[3] == LANES and state_hgrn.shape[4] == LANES
    lbs = hgrn_lower_bounds(hgrn_lb_logits)
    cache = cache_kv.reshape(cache_kv.shape[:3] + (2, 2 * G, hd))
    dec_chunk = 2 * SUBLANES
    assert Ls <= dec_chunk

    xp = x_prompt.reshape(B * L, D)
    xs = x_sample.reshape(Bs * Ls, D)
    kv_p, kv_s, win_p, win_s, st_p, st_s = [], [], [], [], [], []
    wts = _bf16_weights(w_in, w_branch_a, w_branch_b, w_out, w_up, w_down, 4 * hq + nq + 6 * kvw)
    for l in range(depth):
        za, q, kv, win, gm = _project_in(xp, wts, l, norm1_w[l], q_norm_w[l], k_norm_w[l], hq, nq, kvw)
        ya, st = hgrn(za, lbs[l], hgrn_norm_w[l], None, B, L, H, HGRN_CHUNK)
        pooled = pool_prompt(kv, cmp_pos_logits[l], k_norm_w[l, 0])
        yb = nsa_prompt(q, kv, win, pooled, gm, B, L)
        xp = _finish_layer(xp, ya, yb, gm, wts, l, norm2_w[l])
        kv_p.append(kv.reshape(B, L, 4, G, hd))
        wk = min(WINDOW, L)
        win_p.append(win.reshape(B, L, 2, G, hd)[:, L - wk:])
        st_p.append(st)
        za, q, kv, win, gm = _project_in(xs, wts, l, norm1_w[l], q_norm_w[l], k_norm_w[l], hq, nq, kvw)
        za_pad = jnp.pad(za.reshape(Bs, Ls, -1), ((0, 0), (0, dec_chunk - Ls), (0, 0)))
        ya, st = hgrn(za_pad.reshape(Bs * dec_chunk, -1), lbs[l], hgrn_norm_w[l], state_hgrn[l],
                      Bs, dec_chunk, H, dec_chunk, n_valid=Ls)
        ya = ya.reshape(Bs, dec_chunk, -1)[:, :Ls].reshape(Bs * Ls, -1)
        cwin = cache_win[l].reshape(Bs, cache_win.shape[2], 2 * kvw)
        yb = nsa_decode(q, kv, win, gm[:, :GATE_W], cache, l, page_table, cwin, cmp_pos_logits[l],
                        k_norm_w[l, 0], Ls)
        xs = _finish_layer(xs, ya, yb, gm, wts, l, norm2_w[l])
        kv_s.append(kv.reshape(Bs, Ls, 4, G, hd))
        win_all = jnp.concatenate([cwin, win.reshape(Bs, Ls, 2 * kvw)], axis=1)
        ws = min(WINDOW, win_all.shape[1])
        win_s.append(win_all[:, win_all.shape[1] - ws:].reshape(Bs, ws, 2, G, hd))
        st_s.append(st)
    return (xp.reshape(B, L, D), xs.reshape(Bs, Ls, D), jnp.stack(kv_p), jnp.stack(kv_s),
            jnp.stack(win_p), jnp.stack(win_s), jnp.stack(st_p).astype(state_hgrn.dtype),
            jnp.stack(st_s).astype(state_hgrn.dtype))
```

```python
import functools
import math

import jax
import jax.numpy as jnp
from jax import lax
from jax.experimental import pallas as pl
from jax.experimental.pallas import tpu as pltpu

F32 = jnp.float32
BF16 = jnp.bfloat16

LANES = 128
SUBLANES = 8
VMEM_LIMIT = 56 * 1024 * 1024

HEAD_DIM = 128
NSA_KV = 4
NSA_GROUP = 4
BLOCK = 64
N_SEL = 16
N_LOCAL = 2
WINDOW = 512
FORCE_BONUS = float(NSA_GROUP + 1)
SCALE = HEAD_DIM ** -0.5
EPS = 1e-6
MASK_VALUE = -1e30
LOG2E = math.log2(math.e)
HGRN_CHUNK = 128


def _cparams(sem):
    return pltpu.CompilerParams(dimension_semantics=sem, vmem_limit_bytes=VMEM_LIMIT)


def _pick(n, prefs):
    for p in prefs:
        if n % p == 0:
            return p
    return n


def _rmsnorm_kernel(x_ref, w_ref, o_ref):
    x = x_ref[...]
    y = x * lax.rsqrt(jnp.mean(x * x, axis=-1, keepdims=True) + EPS)
    o_ref[...] = (y * w_ref[...]).astype(o_ref.dtype)


def rmsnorm_cast(x, w):
    M, D = x.shape
    tm = _pick(M, (256, 32))
    return pl.pallas_call(
        _rmsnorm_kernel,
        grid=(M // tm,),
        in_specs=[pl.BlockSpec((tm, D), lambda i: (i, 0)),
                  pl.BlockSpec((1, D), lambda i: (0, 0))],
        out_specs=pl.BlockSpec((tm, D), lambda i: (i, 0)),
        out_shape=jax.ShapeDtypeStruct((M, D), BF16),
        compiler_params=_cparams(("parallel",)),
        name="rmsnorm_cast",
    )(x, w.reshape(1, D))


def _mm_kernel(*refs, nk, n_tile, n_row, epilogue, w_t, emit_bf16):
    x_ref, w_ref = refs[0], refs[1]
    tile_refs = refs[2:2 + n_tile]
    row_refs = refs[2 + n_tile:2 + n_tile + n_row]
    o_ref = refs[2 + n_tile + n_row]
    n_out = 2 if emit_bf16 else 1

    def finish(acc):
        extras = [r[...] for r in tile_refs] + [r[...] for r in row_refs]
        o_ref[...] = epilogue(acc, *extras).astype(o_ref.dtype)

    w = w_ref[...]
    if emit_bf16:
        w = w.astype(BF16)
        refs[3 + n_tile + n_row][...] = w
    if w_t:
        part = lax.dot_general(x_ref[...], w, (((1,), (1,)), ((), ())), preferred_element_type=F32)
    else:
        part = jnp.dot(x_ref[...], w, preferred_element_type=F32)
    if nk == 1:
        finish(part)
    else:
        acc_ref = refs[2 + n_out + n_tile + n_row]
        k = pl.program_id(2)

        @pl.when(k == 0)
        def _():
            acc_ref[...] = part

        @pl.when(k > 0)
        def _():
            acc_ref[...] += part

        @pl.when(k == nk - 1)
        def _():
            finish(acc_ref[...])


def matmul(x, w, *, out_dtype, layer=None, col_off=0, n=None, epilogue=None, tile_extras=(), row_extras=(),
           tm=None, tn=None, tk=None, w_t=False, emit_bf16=False, name="matmul"):
    M, K = x.shape
    kax, nax = (-1, -2) if w_t else (-2, -1)
    assert w.shape[kax] == K and (layer is None) == (w.ndim == 2)
    N = n or w.shape[nax]
    tm = tm or _pick(M, (1024, 512, 256, 32))
    tn = tn or _pick(N, (512,) if emit_bf16 else (1024, 512, 256, 128))
    tk = tk or (K if K <= 4096 else _pick(K, (2048,)))
    nk = K // tk
    assert M % tm == 0 and N % tn == 0 and K % tk == 0 and col_off % tn == 0
    assert not emit_bf16 or M == tm
    joff = col_off // tn
    if epilogue is None:
        epilogue = lambda acc: acc
    w_block = (tn, tk) if w_t else (tk, tn)
    w_index = (lambda j, k: (joff + j, k)) if w_t else (lambda j, k: (k, joff + j))
    if layer is None:
        w_spec = pl.BlockSpec(w_block, lambda i, j, k: w_index(j, k))
    else:
        w_spec = pl.BlockSpec((None,) + w_block, lambda i, j, k: (layer,) + w_index(j, k))
    in_specs = [pl.BlockSpec((tm, tk), lambda i, j, k: (i, k)), w_spec]
    in_specs += [pl.BlockSpec((tm, tn), lambda i, j, k: (i, j)) for _ in tile_extras]
    in_specs += [pl.BlockSpec((1, tn), lambda i, j, k: (0, j)) for _ in row_extras]
    scratch = [pltpu.VMEM((tm, tn), F32)] if nk > 1 else []
    out_specs = [pl.BlockSpec((tm, tn), lambda i, j, k: (i, j))]
    out_shape = [jax.ShapeDtypeStruct((M, N), out_dtype)]
    if emit_bf16:
        out_specs.append(pl.BlockSpec(w_block, (lambda i, j, k: (j, k)) if w_t else (lambda i, j, k: (k, j))))
        out_shape.append(jax.ShapeDtypeStruct((N, K) if w_t else (K, N), BF16))
    res = pl.pallas_call(
        functools.partial(_mm_kernel, nk=nk, n_tile=len(tile_extras), n_row=len(row_extras),
                          epilogue=epilogue, w_t=w_t, emit_bf16=emit_bf16),
        grid=(M // tm, N // tn, nk),
        in_specs=in_specs,
        out_specs=out_specs,
        out_shape=out_shape,
        scratch_shapes=scratch,
        compiler_params=_cparams(("parallel", "parallel", "arbitrary")),
        name=name,
    )(x, w, *tile_extras, *row_extras)
    return res if emit_bf16 else res[0]


def _mm_deep_kernel(x_ref, w_ref, r_ref, o_ref, acc_ref, *, nk):
    k = pl.program_id(1)
    j = pl.program_id(2)
    part = jnp.dot(x_ref[...], w_ref[...], preferred_element_type=F32)

    @pl.when(k == 0)
    def _():
        acc_ref[j] = part

    @pl.when((k > 0) & (k < nk - 1))
    def _():
        acc_ref[j] += part

    @pl.when(k == nk - 1)
    def _():
        o_ref[...] = r_ref[...] + (acc_ref[j] + part)


def matmul_deep_residual(x, w, res):
    M, K = x.shape
    N = w.shape[1]
    tm = _pick(M, (1024, 512, 256, 32))
    tn = _pick(N, (512, 256, 128))
    tk = _pick(K, (4096, 2048, 1024, 512))
    nk = K // tk
    assert w.shape[0] == K and M % tm == 0 and N % tn == 0 and nk >= 2
    last = lambda k, j: jnp.where(k == nk - 1, j, 0)
    return pl.pallas_call(
        functools.partial(_mm_deep_kernel, nk=nk),
        grid=(M // tm, nk, N // tn),
        in_specs=[pl.BlockSpec((tm, tk), lambda i, k, j: (i, k)),
                  pl.BlockSpec((tk, tn), lambda i, k, j: (k, j)),
                  pl.BlockSpec((tm, tn), lambda i, k, j: (i, last(k, j)))],
        out_specs=pl.BlockSpec((tm, tn), lambda i, k, j: (i, last(k, j))),
        out_shape=jax.ShapeDtypeStruct((M, N), F32),
        scratch_shapes=[pltpu.VMEM((N // tn, tm, tn), F32)],
        compiler_params=_cparams(("parallel", "arbitrary", "arbitrary")),
        name="mlp_down",
    )(x, w, res)


def _headnorm_epilogue(acc, nw, flag):
    outs = []
    for c in range(acc.shape[1] // HEAD_DIM):
        sl = slice(c * HEAD_DIM, (c + 1) * HEAD_DIM)
        z = acc[:, sl]
        zn = z * lax.rsqrt(jnp.mean(z * z, axis=-1, keepdims=True) + EPS) * nw[:, sl]
        outs.append(jnp.where(flag[:, sl] != 0.0, zn, z))
    return jnp.concatenate(outs, axis=1)


def _residual_epilogue(acc, res):
    return res + acc


def _relu2_epilogue(acc):
    r = jnp.maximum(acc, 0.0)
    return r * r


def _merge_kernel(ya_ref, yb_ref, wa_ref, wb_ref, g0_ref, g1_ref, o_ref, *wq_refs):
    wa = wa_ref[...].astype(BF16)
    wb = wb_ref[...].astype(BF16)
    if wq_refs:
        wq_refs[0][...] = wa
        wq_refs[1][...] = wb
    pa = jnp.dot(ya_ref[...], wa, preferred_element_type=F32)
    pb = jnp.dot(yb_ref[...], wb, preferred_element_type=F32)
    mix = jax.nn.sigmoid(g0_ref[...]) * pa + jax.nn.sigmoid(g1_ref[...]) * pb
    o_ref[...] = mix.astype(o_ref.dtype)


def gated_merge(ya, yb, wa, wb, layer, m_gate, gate_off=0, emit_bf16=False):
    M, Ka = ya.shape
    Kb = yb.shape[1]
    D = wa.shape[-1]
    tm = _pick(M, (1024, 512, 256, 32))
    tn = _pick(D, (512, 256, 128))
    nj = D // tn
    assert gate_off % tn == 0 and (not emit_bf16 or M == tm)
    goff = gate_off // tn
    if layer is None:
        w_spec = lambda k: pl.BlockSpec((k, tn), lambda i, j: (0, j))
    else:
        w_spec = lambda k: pl.BlockSpec((None, k, tn), lambda i, j: (layer, 0, j))
    out_specs = [pl.BlockSpec((tm, tn), lambda i, j: (i, j))]
    out_shape = [jax.ShapeDtypeStruct((M, D), BF16)]
    if emit_bf16:
        out_specs += [pl.BlockSpec((k, tn), lambda i, j: (0, j)) for k in (Ka, Kb)]
        out_shape += [jax.ShapeDtypeStruct((k, D), BF16) for k in (Ka, Kb)]
    res = pl.pallas_call(
        _merge_kernel,
        grid=(M // tm, nj),
        in_specs=[pl.BlockSpec((tm, Ka), lambda i, j: (i, 0)),
                  pl.BlockSpec((tm, Kb), lambda i, j: (i, 0)),
                  w_spec(Ka), w_spec(Kb),
                  pl.BlockSpec((tm, tn), lambda i, j: (i, goff + j)),
                  pl.BlockSpec((tm, tn), lambda i, j: (i, goff + nj + j))],
        out_specs=out_specs,
        out_shape=out_shape,
        compiler_params=_cparams(("parallel", "parallel")),
        name="gated_merge",
    )(ya, yb, wa, wb, m_gate, m_gate)
    return res if emit_bf16 else res[0]


def _silu(x):
    return x * jax.nn.sigmoid(x)


def _pair_total(cm, m, t_io):
    C = cm.shape[0]
    if m == 1:
        return jnp.where((t_io & 1) != 0, pltpu.roll(cm, 1, 0), cm)
    if m == 2:
        j = t_io & 3
        return jnp.where(j == 0, pltpu.roll(cm, C - 1, 0),
                         jnp.where(j == 1, cm,
                                   jnp.where(j == 2, pltpu.roll(cm, 1, 0), pltpu.roll(cm, 2, 0))))
    x3 = cm.reshape(C // (2 * m), 2 * m, cm.shape[1])
    return jnp.broadcast_to(x3[:, m - 1:m, :], x3.shape).reshape(cm.shape)


def _hgrn_chunk(aq, af, ai, ag, lb, nw, st, n_valid):
    C = aq.shape[0]
    t_io = lax.broadcasted_iota(jnp.int32, (C, LANES), 0)
    row_io = lax.broadcasted_iota(jnp.int32, (C, C), 0)
    col_io = lax.broadcasted_iota(jnp.int32, (C, C), 1)
    split = jnp.where(row_io > col_io, row_io ^ col_io, 0)
    for sh in (1, 2, 4, 8, 16):
        split = split | (split >> sh)
    split = split - (split >> 1)
    f = lb + (1.0 - lb) * jax.nn.sigmoid(af)
    if n_valid < C:
        f = jnp.where(t_io < n_valid, f, 1.0)
    g = jnp.log(f)
    kk = 1.0 - f
    qq = _silu(aq)
    v = ai.astype(BF16)
    a = jnp.zeros((C, C), F32)
    cm = g
    m = 1
    while m < C:
        tot = _pair_total(cm, m, t_io)
        odd = (t_io & m) != 0
        z = (jnp.where(odd, qq, kk) * jnp.exp(jnp.where(odd, cm, tot - cm))).astype(BF16)
        p = lax.dot_general(z, z, (((1,), (1,)), ((), ())), preferred_element_type=F32)
        a = a + jnp.where(split == m, p, 0.0)
        cm = cm + jnp.where(odd, tot, 0.0)
        m *= 2
    b = cm
    d = jnp.sum(qq * kk, axis=-1, keepdims=True)
    a = jnp.where(row_io == col_io, d, a)
    o_intra = jnp.dot(a.astype(BF16), v, preferred_element_type=F32)
    qe = (qq * jnp.exp(b)).astype(BF16)
    o_inter = lax.dot_general(qe, st.astype(BF16), (((1,), (1,)), ((), ())), preferred_element_type=F32)
    b_end = b[C - 1:C, :]
    ku = (kk * jnp.exp(b_end - b)).astype(BF16)
    ut = lax.dot_general(v, ku, (((0,), (0,)), ((), ())), preferred_element_type=F32)
    st_new = st * jnp.exp(b_end) + ut
    o = o_inter + o_intra
    on = o * lax.rsqrt(jnp.mean(o * o, axis=-1, keepdims=True) + EPS) * nw
    return on * _silu(ag), st_new


def _hgrn_kernel(*refs, n_sub, chunk, n_valid, has_s0):
    aq_ref, af_ref, ai_ref, ag_ref, lb_ref, nw_ref = refs[:6]
    s0_ref = refs[6] if has_s0 else None
    y_ref, s_ref, st_ref = refs[6 + has_s0:]
    c = pl.program_id(2)

    @pl.when(c == 0)
    def _():
        st_ref[...] = s0_ref[...].T if has_s0 else jnp.zeros_like(st_ref)

    lb = lb_ref[...]
    nw = nw_ref[...]
    for j in range(n_sub):
        rows = pl.ds(j * chunk, chunk)
        y, st_new = _hgrn_chunk(aq_ref[rows, :], af_ref[rows, :], ai_ref[rows, :], ag_ref[rows, :],
                                lb, nw, st_ref[...], n_valid)
        st_ref[...] = st_new
        y_ref[rows, :] = y.astype(y_ref.dtype)

    @pl.when(c == pl.num_programs(2) - 1)
    def _():
        s_ref[...] = st_ref[...].T


def hgrn(za, lb, nw, s0, B, L, H, chunk, n_valid=None):
    dk = LANES
    assert L % chunk == 0 and (n_valid is None or L == chunk)
    n_sub = _pick(L // chunk, (4, 2, 1))
    tc = n_sub * chunk
    nc = L // tc
    row = lambda off: pl.BlockSpec((tc, dk), lambda b, h, c: (b * nc + c, off + h))
    state = pl.BlockSpec((None, None, dk, dk), lambda b, h, c: (b, h, 0, 0))
    has_s0 = s0 is not None
    return pl.pallas_call(
        functools.partial(_hgrn_kernel, n_sub=n_sub, chunk=chunk,
                          n_valid=chunk if n_valid is None else n_valid, has_s0=has_s0),
        grid=(B, H, nc),
        in_specs=[row(0), row(H), row(2 * H), row(3 * H),
                  pl.BlockSpec((1, dk), lambda b, h, c: (0, h)),
                  pl.BlockSpec((1, dk), lambda b, h, c: (0, 0))] + ([state] if has_s0 else []),
        out_specs=[pl.BlockSpec((tc, dk), lambda b, h, c: (b * nc + c, h)), state],
        out_shape=[jax.ShapeDtypeStruct((B * L, H * dk), BF16),
                   jax.ShapeDtypeStruct((B, H, dk, dk), F32)],
        scratch_shapes=[pltpu.VMEM((dk, dk), F32)],
        compiler_params=_cparams(("parallel", "parallel", "arbitrary")),
        name="hgrn",
    )(za, za, za, za, lb.reshape(1, -1), nw.reshape(1, dk), *([s0] if has_s0 else []))


def hgrn_prompt(za, lb, nw, B, L, H):
    return hgrn(za, lb, nw, None, B, L, H, HGRN_CHUNK)


def _pool_rows(x, wl, nw):
    wl = wl - jnp.max(wl, axis=0, keepdims=True)
    e = jnp.exp(wl)
    w = e / jnp.sum(e, axis=0, keepdims=True)
    R = x.shape[0] // BLOCK
    pooled = jnp.sum(x.reshape(R, BLOCK, x.shape[1]) * w[None], axis=1)
    half = x.shape[1] // 2
    outs = []
    for c in range(x.shape[1] // HEAD_DIM):
        z = pooled[:, c * HEAD_DIM:(c + 1) * HEAD_DIM]
        if c * HEAD_DIM < half:
            z = z * lax.rsqrt(jnp.mean(z * z, axis=-1, keepdims=True) + EPS) * nw
        outs.append(z)
    return jnp.concatenate(outs, axis=1)


def _pool_kernel(x_ref, wl_ref, nw_ref, o_ref):
    o_ref[...] = _pool_rows(x_ref[...], wl_ref[...], nw_ref[...])


def pool_prompt(kv, pos_logits, kn_w):
    M = kv.shape[0]
    kvw2 = kv.shape[1] // 2
    rb = SUBLANES * BLOCK
    assert M % rb == 0
    return pl.pallas_call(
        _pool_kernel,
        grid=(M // rb,),
        in_specs=[pl.BlockSpec((rb, kvw2), lambda i: (i, 0)),
                  pl.BlockSpec((BLOCK, 1), lambda i: (0, 0)),
                  pl.BlockSpec((1, HEAD_DIM), lambda i: (0, 0))],
        out_specs=pl.BlockSpec((SUBLANES, kvw2), lambda i: (i, 0)),
        out_shape=jax.ShapeDtypeStruct((M // BLOCK, kvw2), F32),
        compiler_params=_cparams(("parallel",)),
        name="pool_prompt",
    )(kv, pos_logits.reshape(BLOCK, 1), kn_w.reshape(1, HEAD_DIM))


NSA_TQ = 256


def _select_blocks(imp, qpos, n_cand, score_ref):
    cand = lax.broadcasted_iota(jnp.int32, imp.shape, 0)
    cur = qpos // BLOCK
    valid = cand <= cur
    forced = (cand == 0) | (cand > cur - N_LOCAL)
    score = jnp.where(valid, imp + jnp.where(forced, FORCE_BONUS, 0.0), -jnp.inf)
    score_ref[...] = score

    def body(m, rank):
        row = score_ref[pl.ds(m, 1), :]
        beats = (row > score) | ((row == score) & (cand > m))
        return rank + jnp.where(beats, 1.0, 0.0)

    rank = lax.fori_loop(0, n_cand, body, jnp.zeros(imp.shape, F32), unroll=n_cand <= 32)
    return valid & (rank < float(N_SEL))


class _Flash:
    def __init__(self, m_ref, l_ref, acc_ref):
        self.m_ref, self.l_ref, self.acc_ref = m_ref, l_ref, acc_ref

    def init(self):
        self.m_ref[...] = jnp.full(self.m_ref.shape, MASK_VALUE, F32)
        self.l_ref[...] = jnp.zeros(self.l_ref.shape, F32)
        self.acc_ref[...] = jnp.zeros(self.acc_ref.shape, F32)

    def step(self, k, v, qt_b, *, scale=None, bias=None, mask=None):
        s = jnp.dot(k.astype(BF16), qt_b, preferred_element_type=F32)
        if scale is not None:
            s = s * scale
        if bias is not None:
            s = s + bias
        if mask is not None:
            s = jnp.where(mask, s, MASK_VALUE)
        m_old = self.m_ref[...]
        m_new = jnp.maximum(m_old, jnp.max(s, axis=0, keepdims=True))
        alpha = jnp.exp2(m_old - m_new)
        p = jnp.exp2(s - m_new)
        self.l_ref[...] = alpha * self.l_ref[...] + jnp.sum(p, axis=0, keepdims=True)
        pv = lax.dot_general(v.astype(BF16), p.astype(BF16), (((0,), (0,)), ((), ())),
                             preferred_element_type=F32)
        self.acc_ref[...] = alpha * self.acc_ref[...] + pv
        self.m_ref[...] = m_new

    def result(self):
        return self.acc_ref[...] * (1.0 / self.l_ref[...])


def _nsa_prompt_kernel(q_ref, ks_ref, vs_ref, kw_ref, vw_ref, kc_ref, vc_ref, g_ref, o_ref,
                       sel_ref, score_ref, ms_ref, ls_ref, accs_ref, mw_ref, lw_ref, accw_ref, *, nb):
    tq = NSA_TQ
    qt = pl.program_id(2)
    nh = NSA_GROUP
    q = q_ref[...] * SCALE
    qT = jnp.concatenate([q[:, h * HEAD_DIM:(h + 1) * HEAD_DIM].T for h in range(nh)], axis=1)
    qt_b = (qT * LOG2E).astype(BF16)
    lane = lax.broadcasted_iota(jnp.int32, (1, nh * tq), 1)
    qpos = qt * tq + (lane & (tq - 1))

    s = jnp.dot(kc_ref[...], qT, preferred_element_type=F32, precision=lax.Precision.HIGHEST)
    n_io = lax.broadcasted_iota(jnp.int32, s.shape, 0)
    ready = ((n_io + 1) * BLOCK - 1) <= qpos
    s = jnp.where(ready, s, MASK_VALUE)
    e = jnp.exp(s - jnp.max(s, axis=0, keepdims=True))
    p = e / jnp.sum(e, axis=0, keepdims=True) * jnp.where(ready, 1.0, 0.0)
    o_cmp = lax.dot_general(vc_ref[...].astype(BF16), p.astype(BF16), (((0,), (0,)), ((), ())),
                            preferred_element_type=F32)
    imp = p[:, 0:tq]
    for h in range(1, nh):
        imp = imp + p[:, h * tq:(h + 1) * tq]
    sel = _select_blocks(imp, qpos[:, 0:tq], nb, score_ref)
    sel_bias = jnp.where(sel, 0.0, MASK_VALUE)
    bpt = tq // BLOCK
    for n in range(nb):
        sel_ref[n // bpt, n % bpt:n % bpt + 1, :] = sel_bias[n:n + 1, :]

    krow = lax.broadcasted_iota(jnp.int32, (tq, nh * tq), 0)
    sel_acc = _Flash(ms_ref, ls_ref, accs_ref)
    win_acc = _Flash(mw_ref, lw_ref, accw_ref)
    sel_acc.init()
    win_acc.init()

    def tile_bias(kt):
        blk = sel_ref[kt]
        rows = [jnp.broadcast_to(jnp.concatenate([blk[j:j + 1, :]] * nh, axis=1), (BLOCK, nh * tq))
                for j in range(bpt)]
        return jnp.concatenate(rows, axis=0)

    def sel_step(kt, mask=None):
        rows = pl.ds(pl.multiple_of(kt * tq, tq), tq)
        sel_acc.step(ks_ref[rows, :], vs_ref[rows, :], qt_b, bias=tile_bias(kt), mask=mask)

    def win_step(kt, mask):
        rows = pl.ds(pl.multiple_of(kt * tq, tq), tq)
        win_acc.step(kw_ref[rows, :], vw_ref[rows, :], qt_b, mask=mask)

    w_lo = jnp.maximum(qt - WINDOW // tq, 0)

    def far_body(kt, carry):
        sel_step(kt)
        return carry

    lax.fori_loop(0, w_lo, far_body, 0)

    def near_body(kt, carry):
        sel_step(kt)
        win_step(kt, (qpos - (kt * tq + krow)) < WINDOW)
        return carry

    lax.fori_loop(w_lo, qt, near_body, 0)

    causal = (qt * tq + krow) <= qpos
    sel_step(qt, causal)
    win_step(qt, causal)
    o_sel = sel_acc.result()
    o_win = win_acc.result()

    gT = jax.nn.sigmoid(g_ref[...]).T
    for h in range(nh):
        sl = slice(h * tq, (h + 1) * tq)
        y = (gT[3 * h:3 * h + 1, :] * o_cmp[:, sl] + gT[3 * h + 1:3 * h + 2, :] * o_sel[:, sl]
             + gT[3 * h + 2:3 * h + 3, :] * o_win[:, sl])
        o_ref[:, h * HEAD_DIM:(h + 1) * HEAD_DIM] = y.T.astype(o_ref.dtype)


def nsa_prompt(q, kv, win, pooled, gates, B, L):
    tq = NSA_TQ
    G, nh, hd = NSA_KV, NSA_GROUP, HEAD_DIM
    assert L % tq == 0 and tq % BLOCK == 0 and WINDOW % tq == 0
    nq = L // tq
    nb = L // BLOCK
    full = lambda off: pl.BlockSpec((L, hd), lambda b, g, t: (b, off + g))
    return pl.pallas_call(
        functools.partial(_nsa_prompt_kernel, nb=nb),
        grid=(B, G, nq),
        in_specs=[pl.BlockSpec((tq, nh * hd), lambda b, g, t: (b * nq + t, g)),
                  full(2 * G), full(3 * G),
                  pl.BlockSpec((L, hd), lambda b, g, t: (b, g)),
                  pl.BlockSpec((L, hd), lambda b, g, t: (b, G + g)),
                  pl.BlockSpec((nb, hd), lambda b, g, t: (b, g)),
                  pl.BlockSpec((nb, hd), lambda b, g, t: (b, G + g)),
                  pl.BlockSpec((tq, LANES), lambda b, g, t: (b * nq + t, g))],
        out_specs=pl.BlockSpec((tq, nh * hd), lambda b, g, t: (b * nq + t, g)),
        out_shape=jax.ShapeDtypeStruct((B * L, G * nh * hd), BF16),
        scratch_shapes=[pltpu.VMEM((nb * BLOCK // tq, tq // BLOCK, tq), F32),
                        pltpu.VMEM((nb, tq), F32)]
        + 2 * [pltpu.VMEM((1, nh * tq), F32), pltpu.VMEM((1, nh * tq), F32), pltpu.VMEM((hd, nh * tq), F32)],
        compiler_params=_cparams(("parallel", "parallel", "arbitrary")),
        name="nsa_prompt",
    )(q, kv, kv, win, win, pooled, pooled, gates)


DEC_PAGES_PER_STEP = 16
DEC_ROWS = 8


def _dec_qpos(past_len, n_new, shape):
    lane = lax.broadcasted_iota(jnp.int32, shape, len(shape) - 1)
    return past_len + lax.rem(lane & (LANES - 1), n_new)


def _page_spec(cache, layer, half, i, npg):
    return pl.BlockSpec((None, None, cache.shape[2], None) + cache.shape[4:],
                        lambda b, s, pt: (layer, pt[b, s * npg + i], 0, half, 0, 0))


def _page_rows(page_ref, r):
    n, rows, hd = page_ref.shape
    return page_ref.reshape(n * rows, hd)[pl.ds(r, n, stride=rows), :]


def _pool_paged_kernel(pt_ref, *refs):
    del pt_ref
    npg = DEC_PAGES_PER_STEP
    wl_ref, nw_ref, o_ref = refs[npg:npg + 3]
    wl = wl_ref[...]
    e = jnp.exp(wl - jnp.max(wl, axis=0, keepdims=True))
    w = e / jnp.sum(e, axis=0, keepdims=True)
    bpp = refs[0].shape[0] // BLOCK
    for i in range(npg):
        for r in range(2 * NSA_KV):
            x = _page_rows(refs[i], r)
            z = jnp.sum(x.reshape(bpp, BLOCK, HEAD_DIM) * w[None], axis=1)
            if r < NSA_KV:
                z = z * lax.rsqrt(jnp.mean(z * z, axis=-1, keepdims=True) + EPS) * nw_ref[...]
            o_ref[r, i * bpp:(i + 1) * bpp, :] = z


def pool_paged(cache, layer, page_table, pos_logits, kn_w):
    B, n_pages = page_table.shape
    page = cache.shape[2]
    npg = DEC_PAGES_PER_STEP
    assert n_pages % npg == 0 and page % BLOCK == 0
    bpp = page // BLOCK
    assert (npg * bpp) % SUBLANES == 0
    return pl.pallas_call(
        _pool_paged_kernel,
        grid_spec=pltpu.PrefetchScalarGridSpec(
            num_scalar_prefetch=1,
            grid=(B, n_pages // npg),
            in_specs=[_page_spec(cache, layer, 0, i, npg) for i in range(npg)]
            + [pl.BlockSpec((BLOCK, 1), lambda b, s, pt: (0, 0)),
               pl.BlockSpec((1, HEAD_DIM), lambda b, s, pt: (0, 0))],
            out_specs=pl.BlockSpec((None, 2 * NSA_KV, npg * bpp, HEAD_DIM), lambda b, s, pt: (b, 0, s, 0)),
        ),
        out_shape=jax.ShapeDtypeStruct((B, 2 * NSA_KV, n_pages * bpp, HEAD_DIM), F32),
        compiler_params=_cparams(("parallel", "arbitrary")),
        name="pool_paged",
    )(page_table, *([cache] * npg), pos_logits.reshape(BLOCK, 1), kn_w.reshape(1, HEAD_DIM))


def _nsa_dec_front_kernel(qbd_ref, pooled_ref, cwin_ref, nwin_ref, gl_ref, part_ref, sel_ref,
                          score_ref, m_ref, l_ref, acc_ref, *, past_len, n_new, n_cand):
    kvw = qbd_ref.shape[0]
    nl = qbd_ref.shape[1]
    qbd = qbd_ref[...]
    qbd_b = qbd.astype(BF16)
    qpos = _dec_qpos(past_len, n_new, (1, nl))

    nf = pooled_ref.shape[1]
    kc = jnp.concatenate([pooled_ref[g] for g in range(NSA_KV)], axis=1)
    vc = jnp.concatenate([pooled_ref[NSA_KV + g] for g in range(NSA_KV)], axis=1)
    s = jnp.dot(kc, qbd, preferred_element_type=F32, precision=lax.Precision.HIGHEST) * SCALE
    n_io = lax.broadcasted_iota(jnp.int32, s.shape, 0)
    ready = ((n_io + 1) * BLOCK - 1) <= qpos
    s = jnp.where(ready, s, MASK_VALUE)
    e = jnp.exp(s - jnp.max(s, axis=0, keepdims=True))
    p = e / jnp.sum(e, axis=0, keepdims=True) * jnp.where(ready, 1.0, 0.0)
    o_cmp = lax.dot_general(vc.astype(BF16), p.astype(BF16), (((0,), (0,)), ((), ())),
                            preferred_element_type=F32)
    imp = p[:, 0:LANES]
    for h in range(1, NSA_GROUP):
        imp = imp + p[:, h * LANES:(h + 1) * LANES]
    imp = jnp.concatenate([imp, jnp.zeros((score_ref.shape[0] - nf, LANES), F32)], axis=0)
    sel = _select_blocks(imp, qpos[:, 0:LANES], n_cand, score_ref)
    sel_ref[...] = jnp.where(sel, 0.0, MASK_VALUE)

    win_acc = _Flash(m_ref, l_ref, acc_ref)
    win_acc.init()
    w_rows = cwin_ref.shape[0]
    krow = lax.broadcasted_iota(jnp.int32, (w_rows, nl), 0)
    d = qpos - (past_len - w_rows + krow)
    win_acc.step(cwin_ref[:, 0:kvw], cwin_ref[:, kvw:2 * kvw], qbd_b, scale=SCALE * LOG2E,
                 mask=(d >= 0) & (d < WINDOW))
    nrow = lax.broadcasted_iota(jnp.int32, (nwin_ref.shape[0], nl), 0)
    d = qpos - (past_len + nrow)
    win_acc.step(nwin_ref[:, 0:kvw], nwin_ref[:, kvw:2 * kvw], qbd_b, scale=SCALE * LOG2E,
                 mask=(d >= 0) & (d < WINDOW) & (nrow < n_new))
    g = jax.nn.sigmoid(gl_ref[...])
    part_ref[...] = g[0:1, :] * o_cmp + g[2:3, :] * win_acc.result()


def _nsa_dec_sel_kernel(pt_ref, *refs, past_len, n_new, page):
    del pt_ref
    npg = DEC_PAGES_PER_STEP
    page_refs = refs[:npg]
    qg_ref, sel_ref, sel_new_ref, nkv_ref, part_ref, g1_ref, y_ref, m_ref, l_ref, acc_ref = refs[npg:]
    G, hd = NSA_KV, HEAD_DIM
    step = pl.program_id(1)
    chains = [_Flash(m_ref.at[g], l_ref.at[g], acc_ref.at[g]) for g in range(G)]
    qg_b = [qg_ref[g].astype(BF16) for g in range(G)]

    @pl.when(step == 0)
    def _():
        for c in chains:
            c.init()

    for g in range(G):
        bias = jnp.concatenate([jnp.broadcast_to(sel_ref[g, i][j:j + 1, :], (BLOCK, LANES))
                                for i in range(npg) for j in range(page // BLOCK)], axis=0)
        k = jnp.concatenate([_page_rows(page_refs[i], g) for i in range(npg)], axis=0)
        v = jnp.concatenate([_page_rows(page_refs[i], G + g) for i in range(npg)], axis=0)
        chains[g].step(k, v, qg_b[g], scale=SCALE * LOG2E, bias=bias)

    @pl.when(step == pl.num_programs(1) - 1)
    def _():
        nrow = lax.broadcasted_iota(jnp.int32, (nkv_ref.shape[0], LANES), 0)
        lane = lax.broadcasted_iota(jnp.int32, (1, LANES), 1)
        qpos = past_len + lax.rem(lane, n_new)
        mask = (past_len + nrow <= qpos) & (nrow < n_new)
        kvw = G * hd
        for g in range(G):
            chains[g].step(nkv_ref[:, g * hd:(g + 1) * hd], nkv_ref[:, kvw + g * hd:kvw + (g + 1) * hd],
                           qg_b[g], scale=SCALE * LOG2E, bias=sel_new_ref[g, 0][0:1, :], mask=mask)
            y = part_ref[g] + jax.nn.sigmoid(g1_ref[g][0:1, :]) * chains[g].result()
            y_ref[g] = y.T


def nsa_decode(q, kv_new, win_new, gates, cache, layer, page_table, cache_win, pos_logits, kn_w, n_new):
    B, n_pages = page_table.shape
    page = cache.shape[2]
    G, nh, hd = NSA_KV, NSA_GROUP, HEAD_DIM
    kvw = G * hd
    nl = nh * LANES
    past_len = n_pages * page
    npg = DEC_PAGES_PER_STEP
    bpp = page // BLOCK
    n_cand = -(-(past_len + n_new) // BLOCK)
    n_rows = -(-(n_cand + bpp) // SUBLANES) * SUBLANES // bpp * bpp
    assert G * n_new <= LANES and n_new <= DEC_ROWS and n_new <= BLOCK and past_len % BLOCK == 0

    q5 = q.reshape(B, n_new, G, nh, hd)
    qt = jnp.transpose(q5, (0, 2, 4, 3, 1))
    qbd = qt[:, :, :, :, None, :] * jnp.eye(G, dtype=F32)[None, :, None, None, :, None]
    qbd = jnp.pad(qbd.reshape(B, kvw, nh, G * n_new), ((0, 0), (0, 0), (0, 0), (0, LANES - G * n_new)))
    qbd = qbd.reshape(B, kvw, nl)
    gl = gates.reshape(B, n_new, G, LANES)[..., :3 * nh].reshape(B, n_new, G, nh, 3)
    gl = jnp.transpose(gl, (0, 4, 3, 2, 1)).reshape(B, 3, nh, G * n_new)
    gl = jnp.pad(gl, ((0, 0), (0, DEC_ROWS - 3), (0, 0), (0, LANES - G * n_new))).reshape(B, DEC_ROWS, nl)
    pad_rows = lambda t: jnp.pad(t.reshape(B, n_new, -1), ((0, 0), (0, DEC_ROWS - n_new), (0, 0)))
    nkv = pad_rows(kv_new)
    nwin = pad_rows(win_new)

    pooled = pool_paged(cache, layer, page_table, pos_logits, kn_w)
    nf = pooled.shape[2]
    per_b = lambda *shape: pl.BlockSpec((None,) + shape, lambda b: (b,) + (0,) * len(shape))
    part, sel = pl.pallas_call(
        functools.partial(_nsa_dec_front_kernel, past_len=past_len, n_new=n_new, n_cand=n_cand),
        grid=(B,),
        in_specs=[per_b(kvw, nl), per_b(2 * G, nf, hd), per_b(cache_win.shape[1], 2 * kvw),
                  per_b(DEC_ROWS, 2 * kvw), per_b(DEC_ROWS, nl)],
        out_specs=[per_b(kvw, nl), per_b(n_rows, LANES)],
        out_shape=[jax.ShapeDtypeStruct((B, kvw, nl), F32), jax.ShapeDtypeStruct((B, n_rows, LANES), F32)],
        scratch_shapes=[pltpu.VMEM((n_rows, LANES), F32), pltpu.VMEM((1, nl), F32),
                        pltpu.VMEM((1, nl), F32), pltpu.VMEM((kvw, nl), F32)],
        compiler_params=_cparams(("parallel",)),
        name="nsa_dec_front",
    )(qbd, pooled, cache_win, nwin, gl)

    nu = nh * n_new
    lane_pad = lambda t: jnp.pad(t, [(0, 0)] * (t.ndim - 1) + [(0, LANES - nu)])
    qg = lane_pad(qt.reshape(B, G, hd, nu))
    p6 = part.reshape(B, G, hd, nh, LANES)[..., :G * n_new].reshape(B, G, hd, nh, G, n_new)
    partg = lane_pad(jnp.stack([p6[:, g, :, :, g, :] for g in range(G)], axis=1).reshape(B, G, hd, nu))
    selg = jnp.transpose(sel[:, :, :G * n_new].reshape(B, n_rows, G, n_new), (0, 2, 1, 3))
    selg = lane_pad(jnp.tile(selg, (1, 1, 1, nh))).reshape(B, G, n_rows // bpp, bpp, LANES)
    g1 = gates.reshape(B, n_new, G, LANES)[..., :3 * nh].reshape(B, n_new, G, nh, 3)[..., 1]
    g1g = lane_pad(jnp.transpose(g1, (0, 2, 3, 1)).reshape(B, G, 1, nu))
    g1g = jnp.pad(g1g, ((0, 0), (0, 0), (0, SUBLANES - 1), (0, 0)))

    page_spec = lambda i: _page_spec(cache, layer, 1, i, npg)
    per_bs = lambda *shape: pl.BlockSpec((None,) + shape, lambda b, s, pt: (b,) + (0,) * len(shape))
    y = pl.pallas_call(
        functools.partial(_nsa_dec_sel_kernel, past_len=past_len, n_new=n_new, page=page),
        grid_spec=pltpu.PrefetchScalarGridSpec(
            num_scalar_prefetch=1,
            grid=(B, n_pages // npg),
            in_specs=[page_spec(i) for i in range(npg)]
            + [per_bs(G, hd, LANES),
               pl.BlockSpec((None, G, npg, bpp, LANES), lambda b, s, pt: (b, 0, s, 0, 0)),
               pl.BlockSpec((None, G, 1, bpp, LANES), lambda b, s, pt: (b, 0, n_pages, 0, 0)),
               pl.BlockSpec((None, DEC_ROWS, 2 * kvw), lambda b, s, pt: (b, 0, 1)),
               per_bs(G, hd, LANES),
               per_bs(G, SUBLANES, LANES)],
            out_specs=per_bs(G, LANES, hd),
            scratch_shapes=[pltpu.VMEM((G, 1, LANES), F32), pltpu.VMEM((G, 1, LANES), F32),
                            pltpu.VMEM((G, hd, LANES), F32)],
        ),
        out_shape=jax.ShapeDtypeStruct((B, G, LANES, hd), F32),
        compiler_params=_cparams(("parallel", "arbitrary")),
        name="nsa_dec_sel",
    )(page_table, *([cache] * npg), qg, selg, selg, nkv, partg, g1g)
    y = jnp.transpose(y[:, :, :nu].reshape(B, G, nh, n_new, hd), (0, 3, 1, 2, 4))
    return y.reshape(B * n_new, G * nh * hd).astype(BF16)


def _lower_bounds_kernel(x_ref, o_ref):
    x = x_ref[...]
    e = jnp.exp(x - jnp.max(x, axis=0, keepdims=True))
    p = e / jnp.sum(e, axis=0, keepdims=True)
    c = p[0:1, :]
    o_ref[0:1, :] = jnp.zeros_like(c)
    for i in range(1, x.shape[0]):
        c = c + p[i:i + 1, :]
        o_ref[i:i + 1, :] = c - p[0:1, :]


def hgrn_lower_bounds(lb_logits):
    return pl.pallas_call(
        _lower_bounds_kernel,
        out_shape=jax.ShapeDtypeStruct(lb_logits.shape, F32),
        name="hgrn_lower_bounds",
    )(lb_logits)


GATE_W = NSA_KV * LANES


def _gate_weights(w_t, o_bg):
    n_bg = 3 * NSA_KV * NSA_GROUP
    depth, _, K = w_t.shape
    w_bg = w_t[:, o_bg:o_bg + n_bg].reshape(depth, NSA_KV, 3 * NSA_GROUP, K)
    w_bg = jnp.pad(w_bg, ((0, 0), (0, 0), (0, LANES - 3 * NSA_GROUP), (0, 0))).reshape(depth, GATE_W, K)
    return jnp.concatenate([w_bg, w_t[:, o_bg + n_bg:]], axis=1)


def _project_in(x, w_t, w_gm, wq, layer, norm1_w, q_norm_w, k_norm_w, hq, nq, kvw):
    h = rmsnorm_cast(x, norm1_w)
    o_q = 4 * hq
    o_kv = o_q + nq
    o_win = o_kv + 4 * kvw
    pieces = dict(hgrn=(0, o_q), q=(o_q, nq), kv=(o_kv, 4 * kvw), win=(o_win, 2 * kvw))
    new = {}

    def proj(name, **kw):
        off, n = pieces[name]
        if wq is not None:
            return matmul(h, wq[name], out_dtype=F32, w_t=True, name="in_" + name, **kw)
        out, new[name] = matmul(h, w_t, layer=layer, col_off=off, n=n, out_dtype=F32, w_t=True, emit_bf16=True,
                                name="in_" + name, **kw)
        return out

    tile = lambda v, n: jnp.tile(v, n // HEAD_DIM).reshape(1, n)
    ones = lambda n: jnp.ones((1, n), F32)
    zeros = lambda n: jnp.zeros((1, n), F32)
    za = proj("hgrn")
    q = proj("q", epilogue=_headnorm_epilogue, row_extras=(tile(q_norm_w, nq), ones(nq)))
    kv_flag = jnp.concatenate([zeros(2 * kvw), ones(kvw), zeros(kvw)], axis=1)
    kv = proj("kv", epilogue=_headnorm_epilogue, row_extras=(tile(k_norm_w[1], 4 * kvw), kv_flag))
    win_flag = jnp.concatenate([ones(kvw), zeros(kvw)], axis=1)
    win = proj("win", epilogue=_headnorm_epilogue, row_extras=(tile(k_norm_w[2], 2 * kvw), win_flag))
    if wq is not None:
        gm = matmul(h, wq["gm"], out_dtype=F32, w_t=True, name="in_gates")
    else:
        gm, new["gm"] = matmul(h, w_gm, layer=layer, out_dtype=F32, w_t=True, emit_bf16=True, name="in_gates")
    return za, q, kv, win, gm, (wq if wq is not None else new)


def _finish_layer(x, ya, yb, gm, w_f32, wq, layer, norm2_w):
    if wq is not None:
        mix = gated_merge(ya, yb, wq["a"], wq["b"], None, gm, gate_off=GATE_W)
        x1 = matmul(mix, wq["out"], out_dtype=F32, epilogue=_residual_epilogue, tile_extras=(x,),
                    name="out_proj")
        h2 = rmsnorm_cast(x1, norm2_w)
        u = matmul(h2, wq["up"], out_dtype=BF16, epilogue=_relu2_epilogue, name="mlp_up")
        return matmul_deep_residual(u, wq["down"], x1), wq
    new = {}
    mix, new["a"], new["b"] = gated_merge(ya, yb, w_f32["a"], w_f32["b"], layer, gm, gate_off=GATE_W,
                                          emit_bf16=True)
    cast_mm = functools.partial(matmul, layer=layer, emit_bf16=True)
    x1, new["out"] = cast_mm(mix, w_f32["out"], out_dtype=F32, epilogue=_residual_epilogue, tile_extras=(x,),
                             name="out_proj")
    h2 = rmsnorm_cast(x1, norm2_w)
    u, new["up"] = cast_mm(h2, w_f32["up"], out_dtype=BF16, epilogue=_relu2_epilogue, name="mlp_up")
    x2, new["down"] = cast_mm(u, w_f32["down"], out_dtype=F32, epilogue=_residual_epilogue, tile_extras=(x1,),
                              name="mlp_down")
    return x2, new


def kernel(x_prompt, x_sample, cache_kv, cache_win, state_hgrn, page_table, norm1_w, w_in, hgrn_lb_logits,
           hgrn_norm_w, q_norm_w, k_norm_w, cmp_pos_logits, w_branch_a, w_branch_b, w_out, norm2_w, w_up,
           w_down):
    depth = w_in.shape[0]
    B, L, D = x_prompt.shape
    Bs, Ls, _ = x_sample.shape
    H = state_hgrn.shape[2]
    hq = H * state_hgrn.shape[3]
    G, hd = cache_kv.shape[4], cache_kv.shape[5]
    kvw = G * hd
    nq = w_branch_b.shape[1]
    assert (G, hd) == (NSA_KV, HEAD_DIM) and nq == NSA_KV * NSA_GROUP * HEAD_DIM
    assert state_hgrn.shape[3] == LANES and state_hgrn.shape[4] == LANES
    lbs = hgrn_lower_bounds(hgrn_lb_logits)
    cache = cache_kv.reshape(cache_kv.shape[:3] + (2, 2 * G, hd))
    dec_chunk = 2 * SUBLANES
    assert Ls <= dec_chunk

    xp = x_prompt.reshape(B * L, D)
    xs = x_sample.reshape(Bs * Ls, D)
    kv_p, kv_s, win_p, win_s, st_p, st_s = [], [], [], [], [], []
    w_t = jnp.swapaxes(w_in, 1, 2)
    w_gm = _gate_weights(w_t, 4 * hq + nq + 6 * kvw)
    w_f32 = dict(a=w_branch_a, b=w_branch_b, out=w_out, up=w_up, down=w_down)
    for l in range(depth):
        norms = (norm1_w[l], q_norm_w[l], k_norm_w[l], hq, nq, kvw)
        za, q, kv, win, gm, wq_in = _project_in(xs, w_t, w_gm, None, l, *norms)
        za_pad = jnp.pad(za.reshape(Bs, Ls, -1), ((0, 0), (0, dec_chunk - Ls), (0, 0)))
        ya, st = hgrn(za_pad.reshape(Bs * dec_chunk, -1), lbs[l], hgrn_norm_w[l], state_hgrn[l],
                      Bs, dec_chunk, H, dec_chunk, n_valid=Ls)
        ya = ya.reshape(Bs, dec_chunk, -1)[:, :Ls].reshape(Bs * Ls, -1)
        cwin = cache_win[l].reshape(Bs, cache_win.shape[2], 2 * kvw)
        yb = nsa_decode(q, kv, win, gm[:, :GATE_W], cache, l, page_table, cwin, cmp_pos_logits[l],
                        k_norm_w[l, 0], Ls)
        xs, wq_out = _finish_layer(xs, ya, yb, gm, w_f32, None, l, norm2_w[l])
        kv_s.append(kv.reshape(Bs, Ls, 4, G, hd))
        win_all = jnp.concatenate([cwin, win.reshape(Bs, Ls, 2 * kvw)], axis=1)
        ws = min(WINDOW, win_all.shape[1])
        win_s.append(win_all[:, win_all.shape[1] - ws:].reshape(Bs, ws, 2, G, hd))
        st_s.append(st)
        za, q, kv, win, gm, _ = _project_in(xp, w_t, w_gm, wq_in, l, *norms)
        ya, st = hgrn(za, lbs[l], hgrn_norm_w[l], None, B, L, H, HGRN_CHUNK)
        pooled = pool_prompt(kv, cmp_pos_logits[l], k_norm_w[l, 0])
        yb = nsa_prompt(q, kv, win, pooled, gm, B, L)
        xp, _ = _finish_layer(xp, ya, yb, gm, w_f32, wq_out, l, norm2_w[l])
        kv_p.append(kv.reshape(B, L, 4, G, hd))
        wk = min(WINDOW, L)
        win_p.append(win.reshape(B, L, 2, G, hd)[:, L - wk:])
        st_p.append(st)
    return (xp.reshape(B, L, D), xs.reshape(Bs, Ls, D), jnp.stack(kv_p), jnp.stack(kv_s),
            jnp.stack(win_p), jnp.stack(win_s), jnp.stack(st_p).astype(state_hgrn.dtype),
            jnp.stack(st_s).astype(state_hgrn.dtype))
```

```python
import functools
import math

import jax
import jax.numpy as jnp
from jax import lax
from jax.experimental import pallas as pl
from jax.experimental.pallas import tpu as pltpu

F32 = jnp.float32
BF16 = jnp.bfloat16

LANES = 128
SUBLANES = 8
VMEM_LIMIT = 56 * 1024 * 1024

HEAD_DIM = 128
NSA_KV = 4
NSA_GROUP = 4
BLOCK = 64
N_SEL = 16
N_LOCAL = 2
WINDOW = 512
FORCE_BONUS = float(NSA_GROUP + 1)
SCALE = HEAD_DIM ** -0.5
EPS = 1e-6
MASK_VALUE = -1e30
LOG2E = math.log2(math.e)
HGRN_CHUNK = 128


def _cparams(sem):
    return pltpu.CompilerParams(dimension_semantics=sem, vmem_limit_bytes=VMEM_LIMIT)


def _pick(n, prefs):
    for p in prefs:
        if n % p == 0:
            return p
    return n


def _rmsnorm_kernel(x_ref, w_ref, o_ref):
    x = x_ref[...]
    y = x * lax.rsqrt(jnp.mean(x * x, axis=-1, keepdims=True) + EPS)
    o_ref[...] = (y * w_ref[...]).astype(o_ref.dtype)


def rmsnorm_cast(x, w):
    M, D = x.shape
    tm = _pick(M, (256, 32))
    return pl.pallas_call(
        _rmsnorm_kernel,
        grid=(M // tm,),
        in_specs=[pl.BlockSpec((tm, D), lambda i: (i, 0)),
                  pl.BlockSpec((1, D), lambda i: (0, 0))],
        out_specs=pl.BlockSpec((tm, D), lambda i: (i, 0)),
        out_shape=jax.ShapeDtypeStruct((M, D), BF16),
        compiler_params=_cparams(("parallel",)),
        name="rmsnorm_cast",
    )(x, w.reshape(1, D))


def _mm_kernel(*refs, nk, n_tile, n_row, epilogue, w_t, emit_bf16):
    x_ref, w_ref = refs[0], refs[1]
    tile_refs = refs[2:2 + n_tile]
    row_refs = refs[2 + n_tile:2 + n_tile + n_row]
    o_ref = refs[2 + n_tile + n_row]
    n_out = 2 if emit_bf16 else 1

    def finish(acc):
        extras = [r[...] for r in tile_refs] + [r[...] for r in row_refs]
        o_ref[...] = epilogue(acc, *extras).astype(o_ref.dtype)

    w = w_ref[0] if len(w_ref.shape) == 3 else w_ref[...]
    if emit_bf16:
        w = w.astype(BF16)
        refs[3 + n_tile + n_row][...] = w
    if w_t:
        part = lax.dot_general(x_ref[...], w, (((1,), (1,)), ((), ())), preferred_element_type=F32)
    else:
        part = jnp.dot(x_ref[...], w, preferred_element_type=F32)
    if nk == 1:
        finish(part)
    else:
        acc_ref = refs[2 + n_out + n_tile + n_row]
        k = pl.program_id(2)

        @pl.when(k == 0)
        def _():
            acc_ref[...] = part

        @pl.when(k > 0)
        def _():
            acc_ref[...] += part

        @pl.when(k == nk - 1)
        def _():
            finish(acc_ref[...])


def matmul(x, w, *, out_dtype, layer=None, col_off=0, n=None, epilogue=None, tile_extras=(), row_extras=(),
           tm=None, tn=None, tk=None, w_t=False, emit_bf16=False, name="matmul"):
    M, K = x.shape
    kax, nax = (-1, -2) if w_t else (-2, -1)
    assert w.shape[kax] == K and (layer is None) == (w.ndim == 2)
    N = n or w.shape[nax]
    tm = tm or _pick(M, (1024, 512, 256, 32))
    tn = tn or _pick(N, (512,) if emit_bf16 else (1024, 512, 256, 128))
    tk = tk or (K if K <= 4096 else _pick(K, (2048,)))
    nk = K // tk
    assert M % tm == 0 and N % tn == 0 and K % tk == 0
    assert not emit_bf16 or M == tm
    if epilogue is None:
        epilogue = lambda acc: acc
    w_block = (tn, tk) if w_t else (tk, tn)
    if col_off % tn == 0:
        joff = col_off // tn
        w_index = (lambda j, k: (joff + j, k)) if w_t else (lambda j, k: (k, joff + j))
    else:
        assert w_t and col_off % (2 * SUBLANES) == 0 and layer is not None
        w_block = None
    if w_block is None:
        w_spec = pl.BlockSpec((pl.Element(1), pl.Element(tn), pl.Element(tk)),
                              lambda i, j, k: (layer, pl.multiple_of(col_off + j * tn, 2 * SUBLANES),
                                               pl.multiple_of(k * tk, LANES)))
    elif layer is None:
        w_spec = pl.BlockSpec(w_block, lambda i, j, k: w_index(j, k))
    else:
        w_spec = pl.BlockSpec((None,) + w_block, lambda i, j, k: (layer,) + w_index(j, k))
    in_specs = [pl.BlockSpec((tm, tk), lambda i, j, k: (i, k)), w_spec]
    in_specs += [pl.BlockSpec((tm, tn), lambda i, j, k: (i, j)) for _ in tile_extras]
    in_specs += [pl.BlockSpec((1, tn), lambda i, j, k: (0, j)) for _ in row_extras]
    scratch = [pltpu.VMEM((tm, tn), F32)] if nk > 1 else []
    out_specs = [pl.BlockSpec((tm, tn), lambda i, j, k: (i, j))]
    out_shape = [jax.ShapeDtypeStruct((M, N), out_dtype)]
    if emit_bf16:
        out_specs.append(pl.BlockSpec((tn, tk), lambda i, j, k: (j, k)) if w_t
                         else pl.BlockSpec((tk, tn), lambda i, j, k: (k, j)))
        out_shape.append(jax.ShapeDtypeStruct((N, K) if w_t else (K, N), BF16))
    res = pl.pallas_call(
        functools.partial(_mm_kernel, nk=nk, n_tile=len(tile_extras), n_row=len(row_extras),
                          epilogue=epilogue, w_t=w_t, emit_bf16=emit_bf16),
        grid=(M // tm, N // tn, nk),
        in_specs=in_specs,
        out_specs=out_specs,
        out_shape=out_shape,
        scratch_shapes=scratch,
        compiler_params=_cparams(("parallel", "parallel", "arbitrary")),
        name=name,
    )(x, w, *tile_extras, *row_extras)
    return res if emit_bf16 else res[0]


def _mm_deep_kernel(x_ref, w_ref, r_ref, o_ref, acc_ref, *, nk):
    k = pl.program_id(1)
    j = pl.program_id(2)
    part = jnp.dot(x_ref[...], w_ref[...], preferred_element_type=F32)

    @pl.when(k == 0)
    def _():
        acc_ref[j] = part

    @pl.when((k > 0) & (k < nk - 1))
    def _():
        acc_ref[j] += part

    @pl.when(k == nk - 1)
    def _():
        o_ref[...] = r_ref[...] + (acc_ref[j] + part)


def matmul_deep_residual(x, w, res):
    M, K = x.shape
    N = w.shape[1]
    tm = _pick(M, (1024, 512, 256, 32))
    tn = _pick(N, (512, 256, 128))
    tk = _pick(K, (4096, 2048, 1024, 512))
    nk = K // tk
    assert w.shape[0] == K and M % tm == 0 and N % tn == 0 and nk >= 2
    last = lambda k, j: jnp.where(k == nk - 1, j, 0)
    return pl.pallas_call(
        functools.partial(_mm_deep_kernel, nk=nk),
        grid=(M // tm, nk, N // tn),
        in_specs=[pl.BlockSpec((tm, tk), lambda i, k, j: (i, k)),
                  pl.BlockSpec((tk, tn), lambda i, k, j: (k, j)),
                  pl.BlockSpec((tm, tn), lambda i, k, j: (i, last(k, j)))],
        out_specs=pl.BlockSpec((tm, tn), lambda i, k, j: (i, last(k, j))),
        out_shape=jax.ShapeDtypeStruct((M, N), F32),
        scratch_shapes=[pltpu.VMEM((N // tn, tm, tn), F32)],
        compiler_params=_cparams(("parallel", "arbitrary", "arbitrary")),
        name="mlp_down",
    )(x, w, res)


def _headnorm_epilogue(acc, nw, flag):
    outs = []
    for c in range(acc.shape[1] // HEAD_DIM):
        sl = slice(c * HEAD_DIM, (c + 1) * HEAD_DIM)
        z = acc[:, sl]
        zn = z * lax.rsqrt(jnp.mean(z * z, axis=-1, keepdims=True) + EPS) * nw[:, sl]
        outs.append(jnp.where(flag[:, sl] != 0.0, zn, z))
    return jnp.concatenate(outs, axis=1)


def _residual_epilogue(acc, res):
    return res + acc


def _relu2_epilogue(acc):
    r = jnp.maximum(acc, 0.0)
    return r * r


def _merge_kernel(ya_ref, yb_ref, wa_ref, wb_ref, g0_ref, g1_ref, o_ref, *wq_refs):
    wa = wa_ref[...].astype(BF16)
    wb = wb_ref[...].astype(BF16)
    if wq_refs:
        wq_refs[0][...] = wa
        wq_refs[1][...] = wb
    pa = jnp.dot(ya_ref[...], wa, preferred_element_type=F32)
    pb = jnp.dot(yb_ref[...], wb, preferred_element_type=F32)
    mix = jax.nn.sigmoid(g0_ref[...]) * pa + jax.nn.sigmoid(g1_ref[...]) * pb
    o_ref[...] = mix.astype(o_ref.dtype)


def gated_merge(ya, yb, wa, wb, layer, m_gate, gate_off=0, emit_bf16=False):
    M, Ka = ya.shape
    Kb = yb.shape[1]
    D = wa.shape[-1]
    tm = _pick(M, (1024, 512, 256, 32))
    tn = _pick(D, (512, 256, 128))
    nj = D // tn
    assert gate_off % tn == 0 and (not emit_bf16 or M == tm)
    goff = gate_off // tn
    if layer is None:
        w_spec = lambda k: pl.BlockSpec((k, tn), lambda i, j: (0, j))
    else:
        w_spec = lambda k: pl.BlockSpec((None, k, tn), lambda i, j: (layer, 0, j))
    out_specs = [pl.BlockSpec((tm, tn), lambda i, j: (i, j))]
    out_shape = [jax.ShapeDtypeStruct((M, D), BF16)]
    if emit_bf16:
        out_specs += [pl.BlockSpec((k, tn), lambda i, j: (0, j)) for k in (Ka, Kb)]
        out_shape += [jax.ShapeDtypeStruct((k, D), BF16) for k in (Ka, Kb)]
    res = pl.pallas_call(
        _merge_kernel,
        grid=(M // tm, nj),
        in_specs=[pl.BlockSpec((tm, Ka), lambda i, j: (i, 0)),
                  pl.BlockSpec((tm, Kb), lambda i, j: (i, 0)),
                  w_spec(Ka), w_spec(Kb),
                  pl.BlockSpec((tm, tn), lambda i, j: (i, goff + j)),
                  pl.BlockSpec((tm, tn), lambda i, j: (i, goff + nj + j))],
        out_specs=out_specs,
        out_shape=out_shape,
        compiler_params=_cparams(("parallel", "parallel")),
        name="gated_merge",
    )(ya, yb, wa, wb, m_gate, m_gate)
    return res if emit_bf16 else res[0]


def _silu(x):
    return x * jax.nn.sigmoid(x)


def _pair_total(cm, m, t_io):
    C = cm.shape[0]
    if m == 1:
        return jnp.where((t_io & 1) != 0, pltpu.roll(cm, 1, 0), cm)
    if m == 2:
        j = t_io & 3
        return jnp.where(j == 0, pltpu.roll(cm, C - 1, 0),
                         jnp.where(j == 1, cm,
                                   jnp.where(j == 2, pltpu.roll(cm, 1, 0), pltpu.roll(cm, 2, 0))))
    x3 = cm.reshape(C // (2 * m), 2 * m, cm.shape[1])
    return jnp.broadcast_to(x3[:, m - 1:m, :], x3.shape).reshape(cm.shape)


def _hgrn_chunk(aq, af, ai, ag, lb, nw, st, n_valid):
    C = aq.shape[0]
    t_io = lax.broadcasted_iota(jnp.int32, (C, LANES), 0)
    row_io = lax.broadcasted_iota(jnp.int32, (C, C), 0)
    col_io = lax.broadcasted_iota(jnp.int32, (C, C), 1)
    split = jnp.where(row_io > col_io, row_io ^ col_io, 0)
    for sh in (1, 2, 4, 8, 16):
        split = split | (split >> sh)
    split = split - (split >> 1)
    f = lb + (1.0 - lb) * jax.nn.sigmoid(af)
    if n_valid < C:
        f = jnp.where(t_io < n_valid, f, 1.0)
    g = jnp.log(f)
    kk = 1.0 - f
    qq = _silu(aq)
    v = ai.astype(BF16)
    a = jnp.zeros((C, C), F32)
    cm = g
    m = 1
    while m < C:
        tot = _pair_total(cm, m, t_io)
        odd = (t_io & m) != 0
        z = (jnp.where(odd, qq, kk) * jnp.exp(jnp.where(odd, cm, tot - cm))).astype(BF16)
        p = lax.dot_general(z, z, (((1,), (1,)), ((), ())), preferred_element_type=F32)
        a = a + jnp.where(split == m, p, 0.0)
        cm = cm + jnp.where(odd, tot, 0.0)
        m *= 2
    b = cm
    d = jnp.sum(qq * kk, axis=-1, keepdims=True)
    a = jnp.where(row_io == col_io, d, a)
    o_intra = jnp.dot(a.astype(BF16), v, preferred_element_type=F32)
    qe = (qq * jnp.exp(b)).astype(BF16)
    o_inter = lax.dot_general(qe, st.astype(BF16), (((1,), (1,)), ((), ())), preferred_element_type=F32)
    b_end = b[C - 1:C, :]
    ku = (kk * jnp.exp(b_end - b)).astype(BF16)
    ut = lax.dot_general(v, ku, (((0,), (0,)), ((), ())), preferred_element_type=F32)
    st_new = st * jnp.exp(b_end) + ut
    o = o_inter + o_intra
    on = o * lax.rsqrt(jnp.mean(o * o, axis=-1, keepdims=True) + EPS) * nw
    return on * _silu(ag), st_new


def _hgrn_kernel(*refs, n_sub, chunk, n_valid, has_s0):
    aq_ref, af_ref, ai_ref, ag_ref, lb_ref, nw_ref = refs[:6]
    s0_ref = refs[6] if has_s0 else None
    y_ref, s_ref, st_ref = refs[6 + has_s0:]
    c = pl.program_id(2)

    @pl.when(c == 0)
    def _():
        st_ref[...] = s0_ref[...].T if has_s0 else jnp.zeros_like(st_ref)

    lb = lb_ref[...]
    nw = nw_ref[...]
    for j in range(n_sub):
        rows = pl.ds(j * chunk, chunk)
        y, st_new = _hgrn_chunk(aq_ref[rows, :], af_ref[rows, :], ai_ref[rows, :], ag_ref[rows, :],
                                lb, nw, st_ref[...], n_valid)
        st_ref[...] = st_new
        y_ref[rows, :] = y.astype(y_ref.dtype)

    @pl.when(c == pl.num_programs(2) - 1)
    def _():
        s_ref[...] = st_ref[...].T


def hgrn(za, lb, nw, s0, B, L, H, chunk, n_valid=None):
    dk = LANES
    assert L % chunk == 0 and (n_valid is None or L == chunk)
    n_sub = _pick(L // chunk, (4, 2, 1))
    tc = n_sub * chunk
    nc = L // tc
    row = lambda off: pl.BlockSpec((tc, dk), lambda b, h, c: (b * nc + c, off + h))
    state = pl.BlockSpec((None, None, dk, dk), lambda b, h, c: (b, h, 0, 0))
    has_s0 = s0 is not None
    return pl.pallas_call(
        functools.partial(_hgrn_kernel, n_sub=n_sub, chunk=chunk,
                          n_valid=chunk if n_valid is None else n_valid, has_s0=has_s0),
        grid=(B, H, nc),
        in_specs=[row(0), row(H), row(2 * H), row(3 * H),
                  pl.BlockSpec((1, dk), lambda b, h, c: (0, h)),
                  pl.BlockSpec((1, dk), lambda b, h, c: (0, 0))] + ([state] if has_s0 else []),
        out_specs=[pl.BlockSpec((tc, dk), lambda b, h, c: (b * nc + c, h)), state],
        out_shape=[jax.ShapeDtypeStruct((B * L, H * dk), BF16),
                   jax.ShapeDtypeStruct((B, H, dk, dk), F32)],
        scratch_shapes=[pltpu.VMEM((dk, dk), F32)],
        compiler_params=_cparams(("parallel", "parallel", "arbitrary")),
        name="hgrn",
    )(za, za, za, za, lb.reshape(1, -1), nw.reshape(1, dk), *([s0] if has_s0 else []))


def hgrn_prompt(za, lb, nw, B, L, H):
    return hgrn(za, lb, nw, None, B, L, H, HGRN_CHUNK)


def _pool_rows(x, wl, nw):
    wl = wl - jnp.max(wl, axis=0, keepdims=True)
    e = jnp.exp(wl)
    w = e / jnp.sum(e, axis=0, keepdims=True)
    R = x.shape[0] // BLOCK
    pooled = jnp.sum(x.reshape(R, BLOCK, x.shape[1]) * w[None], axis=1)
    half = x.shape[1] // 2
    outs = []
    for c in range(x.shape[1] // HEAD_DIM):
        z = pooled[:, c * HEAD_DIM:(c + 1) * HEAD_DIM]
        if c * HEAD_DIM < half:
            z = z * lax.rsqrt(jnp.mean(z * z, axis=-1, keepdims=True) + EPS) * nw
        outs.append(z)
    return jnp.concatenate(outs, axis=1)


def _pool_kernel(x_ref, wl_ref, nw_ref, o_ref):
    o_ref[...] = _pool_rows(x_ref[...], wl_ref[...], nw_ref[...])


def pool_prompt(kv, pos_logits, kn_w):
    M = kv.shape[0]
    kvw2 = kv.shape[1] // 2
    rb = SUBLANES * BLOCK
    assert M % rb == 0
    return pl.pallas_call(
        _pool_kernel,
        grid=(M // rb,),
        in_specs=[pl.BlockSpec((rb, kvw2), lambda i: (i, 0)),
                  pl.BlockSpec((BLOCK, 1), lambda i: (0, 0)),
                  pl.BlockSpec((1, HEAD_DIM), lambda i: (0, 0))],
        out_specs=pl.BlockSpec((SUBLANES, kvw2), lambda i: (i, 0)),
        out_shape=jax.ShapeDtypeStruct((M // BLOCK, kvw2), F32),
        compiler_params=_cparams(("parallel",)),
        name="pool_prompt",
    )(kv, pos_logits.reshape(BLOCK, 1), kn_w.reshape(1, HEAD_DIM))


NSA_TQ = 256


def _select_blocks(imp, qpos, n_cand, score_ref):
    cand = lax.broadcasted_iota(jnp.int32, imp.shape, 0)
    cur = qpos // BLOCK
    valid = cand <= cur
    forced = (cand == 0) | (cand > cur - N_LOCAL)
    score = jnp.where(valid, imp + jnp.where(forced, FORCE_BONUS, 0.0), -jnp.inf)
    score_ref[...] = score

    def body(m, rank):
        row = score_ref[pl.ds(m, 1), :]
        beats = (row > score) | ((row == score) & (cand > m))
        return rank + jnp.where(beats, 1.0, 0.0)

    rank = lax.fori_loop(0, n_cand, body, jnp.zeros(imp.shape, F32), unroll=n_cand <= 32)
    return valid & (rank < float(N_SEL))


class _Flash:
    def __init__(self, m_ref, l_ref, acc_ref):
        self.m_ref, self.l_ref, self.acc_ref = m_ref, l_ref, acc_ref

    def init(self):
        self.m_ref[...] = jnp.full(self.m_ref.shape, MASK_VALUE, F32)
        self.l_ref[...] = jnp.zeros(self.l_ref.shape, F32)
        self.acc_ref[...] = jnp.zeros(self.acc_ref.shape, F32)

    def step(self, k, v, qt_b, *, scale=None, bias=None, mask=None):
        s = jnp.dot(k.astype(BF16), qt_b, preferred_element_type=F32)
        if scale is not None:
            s = s * scale
        if bias is not None:
            s = s + bias
        if mask is not None:
            s = jnp.where(mask, s, MASK_VALUE)
        m_old = self.m_ref[...]
        m_new = jnp.maximum(m_old, jnp.max(s, axis=0, keepdims=True))
        alpha = jnp.exp2(m_old - m_new)
        p = jnp.exp2(s - m_new)
        self.l_ref[...] = alpha * self.l_ref[...] + jnp.sum(p, axis=0, keepdims=True)
        pv = lax.dot_general(v.astype(BF16), p.astype(BF16), (((0,), (0,)), ((), ())),
                             preferred_element_type=F32)
        self.acc_ref[...] = alpha * self.acc_ref[...] + pv
        self.m_ref[...] = m_new

    def result(self):
        return self.acc_ref[...] * (1.0 / self.l_ref[...])


def _nsa_prompt_kernel(q_ref, ks_ref, vs_ref, kw_ref, vw_ref, kc_ref, vc_ref, g_ref, o_ref,
                       sel_ref, score_ref, ms_ref, ls_ref, accs_ref, mw_ref, lw_ref, accw_ref, *, nb):
    tq = NSA_TQ
    qt = pl.program_id(2)
    nh = NSA_GROUP
    q = q_ref[...] * SCALE
    qT = jnp.concatenate([q[:, h * HEAD_DIM:(h + 1) * HEAD_DIM].T for h in range(nh)], axis=1)
    qt_b = (qT * LOG2E).astype(BF16)
    lane = lax.broadcasted_iota(jnp.int32, (1, nh * tq), 1)
    qpos = qt * tq + (lane & (tq - 1))

    s = jnp.dot(kc_ref[...], qT, preferred_element_type=F32, precision=lax.Precision.HIGHEST)
    n_io = lax.broadcasted_iota(jnp.int32, s.shape, 0)
    ready = ((n_io + 1) * BLOCK - 1) <= qpos
    s = jnp.where(ready, s, MASK_VALUE)
    e = jnp.exp(s - jnp.max(s, axis=0, keepdims=True))
    p = e / jnp.sum(e, axis=0, keepdims=True) * jnp.where(ready, 1.0, 0.0)
    o_cmp = lax.dot_general(vc_ref[...].astype(BF16), p.astype(BF16), (((0,), (0,)), ((), ())),
                            preferred_element_type=F32)
    imp = p[:, 0:tq]
    for h in range(1, nh):
        imp = imp + p[:, h * tq:(h + 1) * tq]
    sel = _select_blocks(imp, qpos[:, 0:tq], nb, score_ref)
    sel_bias = jnp.where(sel, 0.0, MASK_VALUE)
    bpt = tq // BLOCK
    for n in range(nb):
        sel_ref[n // bpt, n % bpt:n % bpt + 1, :] = sel_bias[n:n + 1, :]

    krow = lax.broadcasted_iota(jnp.int32, (tq, nh * tq), 0)
    sel_acc = _Flash(ms_ref, ls_ref, accs_ref)
    win_acc = _Flash(mw_ref, lw_ref, accw_ref)
    sel_acc.init()
    win_acc.init()

    def tile_bias(kt):
        blk = sel_ref[kt]
        rows = [jnp.broadcast_to(jnp.concatenate([blk[j:j + 1, :]] * nh, axis=1), (BLOCK, nh * tq))
                for j in range(bpt)]
        return jnp.concatenate(rows, axis=0)

    def sel_step(kt, mask=None):
        rows = pl.ds(pl.multiple_of(kt * tq, tq), tq)
        sel_acc.step(ks_ref[rows, :], vs_ref[rows, :], qt_b, bias=tile_bias(kt), mask=mask)

    def win_step(kt, mask):
        rows = pl.ds(pl.multiple_of(kt * tq, tq), tq)
        win_acc.step(kw_ref[rows, :], vw_ref[rows, :], qt_b, mask=mask)

    w_lo = jnp.maximum(qt - WINDOW // tq, 0)

    def far_body(kt, carry):
        sel_step(kt)
        return carry

    lax.fori_loop(0, w_lo, far_body, 0)

    def near_body(kt, carry):
        sel_step(kt)
        win_step(kt, (qpos - (kt * tq + krow)) < WINDOW)
        return carry

    lax.fori_loop(w_lo, qt, near_body, 0)

    causal = (qt * tq + krow) <= qpos
    sel_step(qt, causal)
    win_step(qt, causal)
    o_sel = sel_acc.result()
    o_win = win_acc.result()

    gT = jax.nn.sigmoid(g_ref[...]).T
    for h in range(nh):
        sl = slice(h * tq, (h + 1) * tq)
        y = (gT[3 * h:3 * h + 1, :] * o_cmp[:, sl] + gT[3 * h + 1:3 * h + 2, :] * o_sel[:, sl]
             + gT[3 * h + 2:3 * h + 3, :] * o_win[:, sl])
        o_ref[:, h * HEAD_DIM:(h + 1) * HEAD_DIM] = y.T.astype(o_ref.dtype)


def nsa_prompt(q, kv, win, pooled, gates, B, L):
    tq = NSA_TQ
    G, nh, hd = NSA_KV, NSA_GROUP, HEAD_DIM
    assert L % tq == 0 and tq % BLOCK == 0 and WINDOW % tq == 0
    nq = L // tq
    nb = L // BLOCK
    full = lambda off: pl.BlockSpec((L, hd), lambda b, g, t: (b, off + g))
    return pl.pallas_call(
        functools.partial(_nsa_prompt_kernel, nb=nb),
        grid=(B, G, nq),
        in_specs=[pl.BlockSpec((tq, nh * hd), lambda b, g, t: (b * nq + t, g)),
                  full(2 * G), full(3 * G),
                  pl.BlockSpec((L, hd), lambda b, g, t: (b, g)),
                  pl.BlockSpec((L, hd), lambda b, g, t: (b, G + g)),
                  pl.BlockSpec((nb, hd), lambda b, g, t: (b, g)),
                  pl.BlockSpec((nb, hd), lambda b, g, t: (b, G + g)),
                  pl.BlockSpec((tq, LANES), lambda b, g, t: (b * nq + t, g))],
        out_specs=pl.BlockSpec((tq, nh * hd), lambda b, g, t: (b * nq + t, g)),
        out_shape=jax.ShapeDtypeStruct((B * L, G * nh * hd), BF16),
        scratch_shapes=[pltpu.VMEM((nb * BLOCK // tq, tq // BLOCK, tq), F32),
                        pltpu.VMEM((nb, tq), F32)]
        + 2 * [pltpu.VMEM((1, nh * tq), F32), pltpu.VMEM((1, nh * tq), F32), pltpu.VMEM((hd, nh * tq), F32)],
        compiler_params=_cparams(("parallel", "parallel", "arbitrary")),
        name="nsa_prompt",
    )(q, kv, kv, win, win, pooled, pooled, gates)


DEC_PAGES_PER_STEP = 16
DEC_ROWS = 8


def _dec_qpos(past_len, n_new, shape):
    lane = lax.broadcasted_iota(jnp.int32, shape, len(shape) - 1)
    return past_len + lax.rem(lane & (LANES - 1), n_new)


def _page_spec(cache, layer, half, i, npg):
    return pl.BlockSpec((None, None, cache.shape[2], None) + cache.shape[4:],
                        lambda b, s, pt: (layer, pt[b, s * npg + i], 0, half, 0, 0))


def _page_rows(page_ref, r):
    n, rows, hd = page_ref.shape
    return page_ref.reshape(n * rows, hd)[pl.ds(r, n, stride=rows), :]


def _pool_paged_kernel(pt_ref, *refs):
    del pt_ref
    npg = DEC_PAGES_PER_STEP
    wl_ref, nw_ref, o_ref = refs[npg:npg + 3]
    wl = wl_ref[...]
    e = jnp.exp(wl - jnp.max(wl, axis=0, keepdims=True))
    w = e / jnp.sum(e, axis=0, keepdims=True)
    bpp = refs[0].shape[0] // BLOCK
    for i in range(npg):
        for r in range(2 * NSA_KV):
            x = _page_rows(refs[i], r)
            z = jnp.sum(x.reshape(bpp, BLOCK, HEAD_DIM) * w[None], axis=1)
            if r < NSA_KV:
                z = z * lax.rsqrt(jnp.mean(z * z, axis=-1, keepdims=True) + EPS) * nw_ref[...]
            o_ref[r, i * bpp:(i + 1) * bpp, :] = z


def pool_paged(cache, layer, page_table, pos_logits, kn_w):
    B, n_pages = page_table.shape
    page = cache.shape[2]
    npg = DEC_PAGES_PER_STEP
    assert n_pages % npg == 0 and page % BLOCK == 0
    bpp = page // BLOCK
    assert (npg * bpp) % SUBLANES == 0
    return pl.pallas_call(
        _pool_paged_kernel,
        grid_spec=pltpu.PrefetchScalarGridSpec(
            num_scalar_prefetch=1,
            grid=(B, n_pages // npg),
            in_specs=[_page_spec(cache, layer, 0, i, npg) for i in range(npg)]
            + [pl.BlockSpec((BLOCK, 1), lambda b, s, pt: (0, 0)),
               pl.BlockSpec((1, HEAD_DIM), lambda b, s, pt: (0, 0))],
            out_specs=pl.BlockSpec((None, 2 * NSA_KV, npg * bpp, HEAD_DIM), lambda b, s, pt: (b, 0, s, 0)),
        ),
        out_shape=jax.ShapeDtypeStruct((B, 2 * NSA_KV, n_pages * bpp, HEAD_DIM), F32),
        compiler_params=_cparams(("parallel", "arbitrary")),
        name="pool_paged",
    )(page_table, *([cache] * npg), pos_logits.reshape(BLOCK, 1), kn_w.reshape(1, HEAD_DIM))


def _nsa_dec_front_kernel(qbd_ref, pooled_ref, cwin_ref, nwin_ref, gl_ref, part_ref, sel_ref,
                          score_ref, m_ref, l_ref, acc_ref, *, past_len, n_new, n_cand):
    kvw = qbd_ref.shape[0]
    nl = qbd_ref.shape[1]
    qbd = qbd_ref[...]
    qbd_b = qbd.astype(BF16)
    qpos = _dec_qpos(past_len, n_new, (1, nl))

    nf = pooled_ref.shape[1]
    kc = jnp.concatenate([pooled_ref[g] for g in range(NSA_KV)], axis=1)
    vc = jnp.concatenate([pooled_ref[NSA_KV + g] for g in range(NSA_KV)], axis=1)
    s = jnp.dot(kc, qbd, preferred_element_type=F32, precision=lax.Precision.HIGHEST) * SCALE
    n_io = lax.broadcasted_iota(jnp.int32, s.shape, 0)
    ready = ((n_io + 1) * BLOCK - 1) <= qpos
    s = jnp.where(ready, s, MASK_VALUE)
    e = jnp.exp(s - jnp.max(s, axis=0, keepdims=True))
    p = e / jnp.sum(e, axis=0, keepdims=True) * jnp.where(ready, 1.0, 0.0)
    o_cmp = lax.dot_general(vc.astype(BF16), p.astype(BF16), (((0,), (0,)), ((), ())),
                            preferred_element_type=F32)
    imp = p[:, 0:LANES]
    for h in range(1, NSA_GROUP):
        imp = imp + p[:, h * LANES:(h + 1) * LANES]
    imp = jnp.concatenate([imp, jnp.zeros((score_ref.shape[0] - nf, LANES), F32)], axis=0)
    sel = _select_blocks(imp, qpos[:, 0:LANES], n_cand, score_ref)
    sel_ref[...] = jnp.where(sel, 0.0, MASK_VALUE)

    win_acc = _Flash(m_ref, l_ref, acc_ref)
    win_acc.init()
    w_rows = cwin_ref.shape[0]
    krow = lax.broadcasted_iota(jnp.int32, (w_rows, nl), 0)
    d = qpos - (past_len - w_rows + krow)
    win_acc.step(cwin_ref[:, 0:kvw], cwin_ref[:, kvw:2 * kvw], qbd_b, scale=SCALE * LOG2E,
                 mask=(d >= 0) & (d < WINDOW))
    nrow = lax.broadcasted_iota(jnp.int32, (nwin_ref.shape[0], nl), 0)
    d = qpos - (past_len + nrow)
    win_acc.step(nwin_ref[:, 0:kvw], nwin_ref[:, kvw:2 * kvw], qbd_b, scale=SCALE * LOG2E,
                 mask=(d >= 0) & (d < WINDOW) & (nrow < n_new))
    g = jax.nn.sigmoid(gl_ref[...])
    part_ref[...] = g[0:1, :] * o_cmp + g[2:3, :] * win_acc.result()


def _nsa_dec_sel_kernel(pt_ref, *refs, past_len, n_new, page):
    del pt_ref
    npg = DEC_PAGES_PER_STEP
    page_refs = refs[:npg]
    qg_ref, sel_ref, sel_new_ref, nkv_ref, part_ref, g1_ref, y_ref, m_ref, l_ref, acc_ref = refs[npg:]
    G, hd = NSA_KV, HEAD_DIM
    step = pl.program_id(1)
    chains = [_Flash(m_ref.at[g], l_ref.at[g], acc_ref.at[g]) for g in range(G)]
    qg_b = [qg_ref[g].astype(BF16) for g in range(G)]

    @pl.when(step == 0)
    def _():
        for c in chains:
            c.init()

    for g in range(G):
        bias = jnp.concatenate([jnp.broadcast_to(sel_ref[g, i][j:j + 1, :], (BLOCK, LANES))
                                for i in range(npg) for j in range(page // BLOCK)], axis=0)
        k = jnp.concatenate([_page_rows(page_refs[i], g) for i in range(npg)], axis=0)
        v = jnp.concatenate([_page_rows(page_refs[i], G + g) for i in range(npg)], axis=0)
        chains[g].step(k, v, qg_b[g], scale=SCALE * LOG2E, bias=bias)

    @pl.when(step == pl.num_programs(1) - 1)
    def _():
        nrow = lax.broadcasted_iota(jnp.int32, (nkv_ref.shape[0], LANES), 0)
        lane = lax.broadcasted_iota(jnp.int32, (1, LANES), 1)
        qpos = past_len + lax.rem(lane, n_new)
        mask = (past_len + nrow <= qpos) & (nrow < n_new)
        kvw = G * hd
        for g in range(G):
            chains[g].step(nkv_ref[:, g * hd:(g + 1) * hd], nkv_ref[:, kvw + g * hd:kvw + (g + 1) * hd],
                           qg_b[g], scale=SCALE * LOG2E, bias=sel_new_ref[g, 0][0:1, :], mask=mask)
            y = part_ref[g] + jax.nn.sigmoid(g1_ref[g][0:1, :]) * chains[g].result()
            y_ref[g] = y.T


def nsa_decode(q, kv_new, win_new, gates, cache, layer, page_table, cache_win, pos_logits, kn_w, n_new):
    B, n_pages = page_table.shape
    page = cache.shape[2]
    G, nh, hd = NSA_KV, NSA_GROUP, HEAD_DIM
    kvw = G * hd
    nl = nh * LANES
    past_len = n_pages * page
    npg = DEC_PAGES_PER_STEP
    bpp = page // BLOCK
    n_cand = -(-(past_len + n_new) // BLOCK)
    n_rows = -(-(n_cand + bpp) // SUBLANES) * SUBLANES // bpp * bpp
    assert G * n_new <= LANES and n_new <= DEC_ROWS and n_new <= BLOCK and past_len % BLOCK == 0

    q5 = q.reshape(B, n_new, G, nh, hd)
    qt = jnp.transpose(q5, (0, 2, 4, 3, 1))
    qbd = qt[:, :, :, :, None, :] * jnp.eye(G, dtype=F32)[None, :, None, None, :, None]
    qbd = jnp.pad(qbd.reshape(B, kvw, nh, G * n_new), ((0, 0), (0, 0), (0, 0), (0, LANES - G * n_new)))
    qbd = qbd.reshape(B, kvw, nl)
    gl = gates.reshape(B, n_new, G, LANES)[..., :3 * nh].reshape(B, n_new, G, nh, 3)
    gl = jnp.transpose(gl, (0, 4, 3, 2, 1)).reshape(B, 3, nh, G * n_new)
    gl = jnp.pad(gl, ((0, 0), (0, DEC_ROWS - 3), (0, 0), (0, LANES - G * n_new))).reshape(B, DEC_ROWS, nl)
    pad_rows = lambda t: jnp.pad(t.reshape(B, n_new, -1), ((0, 0), (0, DEC_ROWS - n_new), (0, 0)))
    nkv = pad_rows(kv_new)
    nwin = pad_rows(win_new)

    pooled = pool_paged(cache, layer, page_table, pos_logits, kn_w)
    nf = pooled.shape[2]
    per_b = lambda *shape: pl.BlockSpec((None,) + shape, lambda b: (b,) + (0,) * len(shape))
    part, sel = pl.pallas_call(
        functools.partial(_nsa_dec_front_kernel, past_len=past_len, n_new=n_new, n_cand=n_cand),
        grid=(B,),
        in_specs=[per_b(kvw, nl), per_b(2 * G, nf, hd), per_b(cache_win.shape[1], 2 * kvw),
                  per_b(DEC_ROWS, 2 * kvw), per_b(DEC_ROWS, nl)],
        out_specs=[per_b(kvw, nl), per_b(n_rows, LANES)],
        out_shape=[jax.ShapeDtypeStruct((B, kvw, nl), F32), jax.ShapeDtypeStruct((B, n_rows, LANES), F32)],
        scratch_shapes=[pltpu.VMEM((n_rows, LANES), F32), pltpu.VMEM((1, nl), F32),
                        pltpu.VMEM((1, nl), F32), pltpu.VMEM((kvw, nl), F32)],
        compiler_params=_cparams(("parallel",)),
        name="nsa_dec_front",
    )(qbd, pooled, cache_win, nwin, gl)

    nu = nh * n_new
    lane_pad = lambda t: jnp.pad(t, [(0, 0)] * (t.ndim - 1) + [(0, LANES - nu)])
    qg = lane_pad(qt.reshape(B, G, hd, nu))
    p6 = part.reshape(B, G, hd, nh, LANES)[..., :G * n_new].reshape(B, G, hd, nh, G, n_new)
    partg = lane_pad(jnp.stack([p6[:, g, :, :, g, :] for g in range(G)], axis=1).reshape(B, G, hd, nu))
    selg = jnp.transpose(sel[:, :, :G * n_new].reshape(B, n_rows, G, n_new), (0, 2, 1, 3))
    selg = lane_pad(jnp.tile(selg, (1, 1, 1, nh))).reshape(B, G, n_rows // bpp, bpp, LANES)
    g1 = gates.reshape(B, n_new, G, LANES)[..., :3 * nh].reshape(B, n_new, G, nh, 3)[..., 1]
    g1g = lane_pad(jnp.transpose(g1, (0, 2, 3, 1)).reshape(B, G, 1, nu))
    g1g = jnp.pad(g1g, ((0, 0), (0, 0), (0, SUBLANES - 1), (0, 0)))

    page_spec = lambda i: _page_spec(cache, layer, 1, i, npg)
    per_bs = lambda *shape: pl.BlockSpec((None,) + shape, lambda b, s, pt: (b,) + (0,) * len(shape))
    y = pl.pallas_call(
        functools.partial(_nsa_dec_sel_kernel, past_len=past_len, n_new=n_new, page=page),
        grid_spec=pltpu.PrefetchScalarGridSpec(
            num_scalar_prefetch=1,
            grid=(B, n_pages // npg),
            in_specs=[page_spec(i) for i in range(npg)]
            + [per_bs(G, hd, LANES),
               pl.BlockSpec((None, G, npg, bpp, LANES), lambda b, s, pt: (b, 0, s, 0, 0)),
               pl.BlockSpec((None, G, 1, bpp, LANES), lambda b, s, pt: (b, 0, n_pages, 0, 0)),
               pl.BlockSpec((None, DEC_ROWS, 2 * kvw), lambda b, s, pt: (b, 0, 1)),
               per_bs(G, hd, LANES),
               per_bs(G, SUBLANES, LANES)],
            out_specs=per_bs(G, LANES, hd),
            scratch_shapes=[pltpu.VMEM((G, 1, LANES), F32), pltpu.VMEM((G, 1, LANES), F32),
                            pltpu.VMEM((G, hd, LANES), F32)],
        ),
        out_shape=jax.ShapeDtypeStruct((B, G, LANES, hd), F32),
        compiler_params=_cparams(("parallel", "arbitrary")),
        name="nsa_dec_sel",
    )(page_table, *([cache] * npg), qg, selg, selg, nkv, partg, g1g)
    y = jnp.transpose(y[:, :, :nu].reshape(B, G, nh, n_new, hd), (0, 3, 1, 2, 4))
    return y.reshape(B * n_new, G * nh * hd).astype(BF16)


def _lower_bounds_kernel(x_ref, o_ref):
    x = x_ref[...]
    e = jnp.exp(x - jnp.max(x, axis=0, keepdims=True))
    p = e / jnp.sum(e, axis=0, keepdims=True)
    c = p[0:1, :]
    o_ref[0:1, :] = jnp.zeros_like(c)
    for i in range(1, x.shape[0]):
        c = c + p[i:i + 1, :]
        o_ref[i:i + 1, :] = c - p[0:1, :]


def hgrn_lower_bounds(lb_logits):
    return pl.pallas_call(
        _lower_bounds_kernel,
        out_shape=jax.ShapeDtypeStruct(lb_logits.shape, F32),
        name="hgrn_lower_bounds",
    )(lb_logits)


GATE_W = NSA_KV * LANES


def _gate_weights(w_t, o_bg):
    n_bg = 3 * NSA_KV * NSA_GROUP
    depth, _, K = w_t.shape
    w_bg = w_t[:, o_bg:o_bg + n_bg].reshape(depth, NSA_KV, 3 * NSA_GROUP, K)
    return jnp.pad(w_bg, ((0, 0), (0, 0), (0, LANES - 3 * NSA_GROUP), (0, 0))).reshape(depth, GATE_W, K)


def _project_in(x, w_t, w_bg, wq, layer, norm1_w, q_norm_w, k_norm_w, hq, nq, kvw):
    h = rmsnorm_cast(x, norm1_w)
    o_q = 4 * hq
    o_kv = o_q + nq
    o_win = o_kv + 4 * kvw
    o_mg = o_win + 2 * kvw + 3 * NSA_KV * NSA_GROUP
    pieces = dict(hgrn=(0, o_q), q=(o_q, nq), kv=(o_kv, 4 * kvw), win=(o_win, 2 * kvw),
                  mg=(o_mg, w_t.shape[1] - o_mg))
    new = {}

    def proj(name, **kw):
        off, n = pieces[name]
        if wq is not None:
            return matmul(h, wq[name], out_dtype=F32, w_t=True, name="in_" + name, **kw)
        out, new[name] = matmul(h, w_t, layer=layer, col_off=off, n=n, out_dtype=F32, w_t=True, emit_bf16=True,
                                name="in_" + name, **kw)
        return out

    tile = lambda v, n: jnp.tile(v, n // HEAD_DIM).reshape(1, n)
    ones = lambda n: jnp.ones((1, n), F32)
    zeros = lambda n: jnp.zeros((1, n), F32)
    za = proj("hgrn")
    q = proj("q", epilogue=_headnorm_epilogue, row_extras=(tile(q_norm_w, nq), ones(nq)))
    kv_flag = jnp.concatenate([zeros(2 * kvw), ones(kvw), zeros(kvw)], axis=1)
    kv = proj("kv", epilogue=_headnorm_epilogue, row_extras=(tile(k_norm_w[1], 4 * kvw), kv_flag))
    win_flag = jnp.concatenate([ones(kvw), zeros(kvw)], axis=1)
    win = proj("win", epilogue=_headnorm_epilogue, row_extras=(tile(k_norm_w[2], 2 * kvw), win_flag))
    mg = proj("mg")
    if wq is not None:
        bg = matmul(h, wq["bg"], out_dtype=F32, w_t=True, name="in_bg")
    else:
        bg, new["bg"] = matmul(h, w_bg, layer=layer, out_dtype=F32, w_t=True, emit_bf16=True, name="in_bg")
    return za, q, kv, win, bg, mg, (wq if wq is not None else new)


def _finish_layer(x, ya, yb, mg, w_f32, wq, layer, norm2_w):
    if wq is not None:
        mix = gated_merge(ya, yb, wq["a"], wq["b"], None, mg)
        x1 = matmul(mix, wq["out"], out_dtype=F32, epilogue=_residual_epilogue, tile_extras=(x,),
                    name="out_proj")
        h2 = rmsnorm_cast(x1, norm2_w)
        u = matmul(h2, wq["up"], out_dtype=BF16, epilogue=_relu2_epilogue, name="mlp_up")
        return matmul_deep_residual(u, wq["down"], x1), wq
    new = {}
    mix, new["a"], new["b"] = gated_merge(ya, yb, w_f32["a"], w_f32["b"], layer, mg, emit_bf16=True)
    cast_mm = functools.partial(matmul, layer=layer, emit_bf16=True)
    x1, new["out"] = cast_mm(mix, w_f32["out"], out_dtype=F32, epilogue=_residual_epilogue, tile_extras=(x,),
                             name="out_proj")
    h2 = rmsnorm_cast(x1, norm2_w)
    u, new["up"] = cast_mm(h2, w_f32["up"], out_dtype=BF16, epilogue=_relu2_epilogue, name="mlp_up")
    x2, new["down"] = cast_mm(u, w_f32["down"], out_dtype=F32, epilogue=_residual_epilogue, tile_extras=(x1,),
                              name="mlp_down")
    return x2, new


def kernel(x_prompt, x_sample, cache_kv, cache_win, state_hgrn, page_table, norm1_w, w_in, hgrn_lb_logits,
           hgrn_norm_w, q_norm_w, k_norm_w, cmp_pos_logits, w_branch_a, w_branch_b, w_out, norm2_w, w_up,
           w_down):
    depth = w_in.shape[0]
    B, L, D = x_prompt.shape
    Bs, Ls, _ = x_sample.shape
    H = state_hgrn.shape[2]
    hq = H * state_hgrn.shape[3]
    G, hd = cache_kv.shape[4], cache_kv.shape[5]
    kvw = G * hd
    nq = w_branch_b.shape[1]
    assert (G, hd) == (NSA_KV, HEAD_DIM) and nq == NSA_KV * NSA_GROUP * HEAD_DIM
    assert state_hgrn.shape[3] == LANES and state_hgrn.shape[4] == LANES
    lbs = hgrn_lower_bounds(hgrn_lb_logits)
    cache = cache_kv.reshape(cache_kv.shape[:3] + (2, 2 * G, hd))
    dec_chunk = 2 * SUBLANES
    assert Ls <= dec_chunk

    xp = x_prompt.reshape(B * L, D)
    xs = x_sample.reshape(Bs * Ls, D)
    kv_p, kv_s, win_p, win_s, st_p, st_s = [], [], [], [], [], []
    w_t = jnp.swapaxes(w_in, 1, 2)
    w_bg = _gate_weights(w_t, 4 * hq + nq + 6 * kvw)
    w_f32 = dict(a=w_branch_a, b=w_branch_b, out=w_out, up=w_up, down=w_down)
    for l in range(depth):
        norms = (norm1_w[l], q_norm_w[l], k_norm_w[l], hq, nq, kvw)
        za, q, kv, win, bg, mg, wq_in = _project_in(xs, w_t, w_bg, None, l, *norms)
        za_pad = jnp.pad(za.reshape(Bs, Ls, -1), ((0, 0), (0, dec_chunk - Ls), (0, 0)))
        ya, st = hgrn(za_pad.reshape(Bs * dec_chunk, -1), lbs[l], hgrn_norm_w[l], state_hgrn[l],
                      Bs, dec_chunk, H, dec_chunk, n_valid=Ls)
        ya = ya.reshape(Bs, dec_chunk, -1)[:, :Ls].reshape(Bs * Ls, -1)
        cwin = cache_win[l].reshape(Bs, cache_win.shape[2], 2 * kvw)
        yb = nsa_decode(q, kv, win, bg, cache, l, page_table, cwin, cmp_pos_logits[l],
                        k_norm_w[l, 0], Ls)
        xs, wq_out = _finish_layer(xs, ya, yb, mg, w_f32, None, l, norm2_w[l])
        kv_s.append(kv.reshape(Bs, Ls, 4, G, hd))
        win_all = jnp.concatenate([cwin, win.reshape(Bs, Ls, 2 * kvw)], axis=1)
        ws = min(WINDOW, win_all.shape[1])
        win_s.append(win_all[:, win_all.shape[1] - ws:].reshape(Bs, ws, 2, G, hd))
        st_s.append(st)
        za, q, kv, win, bg, mg, _ = _project_in(xp, w_t, w_bg, wq_in, l, *norms)
        ya, st = hgrn(za, lbs[l], hgrn_norm_w[l], None, B, L, H, HGRN_CHUNK)
        pooled = pool_prompt(kv, cmp_pos_logits[l], k_norm_w[l, 0])
        yb = nsa_prompt(q, kv, win, pooled, bg, B, L)
        xp, _ = _finish_layer(xp, ya, yb, mg, w_f32, wq_out, l, norm2_w[l])
        kv_p.append(kv.reshape(B, L, 4, G, hd))
        wk = min(WINDOW, L)
        win_p.append(win.reshape(B, L, 2, G, hd)[:, L - wk:])
        st_p.append(st)
    return (xp.reshape(B, L, D), xs.reshape(Bs, Ls, D), jnp.stack(kv_p), jnp.stack(kv_s),
            jnp.stack(win_p), jnp.stack(win_s), jnp.stack(st_p).astype(state_hgrn.dtype),
            jnp.stack(st_s).astype(state_hgrn.dtype))
```

```python
import functools
import math

import jax
import jax.numpy as jnp
from jax import lax
from jax.experimental import pallas as pl
from jax.experimental.pallas import tpu as pltpu

F32 = jnp.float32
BF16 = jnp.bfloat16

LANES = 128
SUBLANES = 8
VMEM_LIMIT = 56 * 1024 * 1024

HEAD_DIM = 128
NSA_KV = 4
NSA_GROUP = 4
BLOCK = 64
N_SEL = 16
N_LOCAL = 2
WINDOW = 512
FORCE_BONUS = float(NSA_GROUP + 1)
SCALE = HEAD_DIM ** -0.5
EPS = 1e-6
MASK_VALUE = -1e30
LOG2E = math.log2(math.e)
HGRN_CHUNK = 128


def _cparams(sem):
    return pltpu.CompilerParams(dimension_semantics=sem, vmem_limit_bytes=VMEM_LIMIT)


def _pick(n, prefs):
    for p in prefs:
        if n % p == 0:
            return p
    return n


def _rmsnorm_kernel(x_ref, w_ref, o_ref):
    x = x_ref[...]
    y = x * lax.rsqrt(jnp.mean(x * x, axis=-1, keepdims=True) + EPS)
    o_ref[...] = (y * w_ref[...]).astype(o_ref.dtype)


def rmsnorm_cast(x, w):
    M, D = x.shape
    tm = _pick(M, (256, 32))
    return pl.pallas_call(
        _rmsnorm_kernel,
        grid=(M // tm,),
        in_specs=[pl.BlockSpec((tm, D), lambda i: (i, 0)),
                  pl.BlockSpec((1, D), lambda i: (0, 0))],
        out_specs=pl.BlockSpec((tm, D), lambda i: (i, 0)),
        out_shape=jax.ShapeDtypeStruct((M, D), BF16),
        compiler_params=_cparams(("parallel",)),
        name="rmsnorm_cast",
    )(x, w.reshape(1, D))


def _mm_kernel(*refs, nk, n_tile, n_row, epilogue, w_t, emit_bf16):
    x_ref, w_ref = refs[0], refs[1]
    tile_refs = refs[2:2 + n_tile]
    row_refs = refs[2 + n_tile:2 + n_tile + n_row]
    o_ref = refs[2 + n_tile + n_row]
    n_out = 2 if emit_bf16 else 1

    def finish(acc):
        extras = [r[...] for r in tile_refs] + [r[...] for r in row_refs]
        o_ref[...] = epilogue(acc, *extras).astype(o_ref.dtype)

    w = w_ref[0] if len(w_ref.shape) == 3 else w_ref[...]
    if emit_bf16:
        w = w.astype(BF16)
        refs[3 + n_tile + n_row][...] = w
    if w_t:
        part = lax.dot_general(x_ref[...], w, (((1,), (1,)), ((), ())), preferred_element_type=F32)
    else:
        part = jnp.dot(x_ref[...], w, preferred_element_type=F32)
    if nk == 1:
        finish(part)
    else:
        acc_ref = refs[2 + n_out + n_tile + n_row]
        k = pl.program_id(2)

        @pl.when(k == 0)
        def _():
            acc_ref[...] = part

        @pl.when(k > 0)
        def _():
            acc_ref[...] += part

        @pl.when(k == nk - 1)
        def _():
            finish(acc_ref[...])


def matmul(x, w, *, out_dtype, layer=None, col_off=0, n=None, epilogue=None, tile_extras=(), row_extras=(),
           tm=None, tn=None, tk=None, w_t=False, emit_bf16=False, name="matmul"):
    M, K = x.shape
    kax, nax = (-1, -2) if w_t else (-2, -1)
    assert w.shape[kax] == K and (layer is None) == (w.ndim == 2)
    N = n or w.shape[nax]
    tm = tm or _pick(M, (1024, 512, 256, 32))
    tn = tn or _pick(N, (512,) if emit_bf16 else (1024, 512, 256, 128))
    tk = tk or (K if K <= 4096 else _pick(K, (2048,)))
    nk = K // tk
    assert M % tm == 0 and N % tn == 0 and K % tk == 0
    assert not emit_bf16 or M == tm
    if epilogue is None:
        epilogue = lambda acc: acc
    w_block = (tn, tk) if w_t else (tk, tn)
    if col_off % tn == 0:
        joff = col_off // tn
        w_index = (lambda j, k: (joff + j, k)) if w_t else (lambda j, k: (k, joff + j))
    else:
        assert w_t and col_off % (2 * SUBLANES) == 0 and layer is not None
        w_block = None
    if w_block is None:
        w_spec = pl.BlockSpec((pl.Element(1), pl.Element(tn), pl.Element(tk)),
                              lambda i, j, k: (layer, pl.multiple_of(col_off + j * tn, 2 * SUBLANES),
                                               pl.multiple_of(k * tk, LANES)))
    elif layer is None:
        w_spec = pl.BlockSpec(w_block, lambda i, j, k: w_index(j, k))
    else:
        w_spec = pl.BlockSpec((None,) + w_block, lambda i, j, k: (layer,) + w_index(j, k))
    in_specs = [pl.BlockSpec((tm, tk), lambda i, j, k: (i, k)), w_spec]
    in_specs += [pl.BlockSpec((tm, tn), lambda i, j, k: (i, j)) for _ in tile_extras]
    in_specs += [pl.BlockSpec((1, tn), lambda i, j, k: (0, j)) for _ in row_extras]
    scratch = [pltpu.VMEM((tm, tn), F32)] if nk > 1 else []
    out_specs = [pl.BlockSpec((tm, tn), lambda i, j, k: (i, j))]
    out_shape = [jax.ShapeDtypeStruct((M, N), out_dtype)]
    if emit_bf16:
        out_specs.append(pl.BlockSpec((tn, tk), lambda i, j, k: (j, k)) if w_t
                         else pl.BlockSpec((tk, tn), lambda i, j, k: (k, j)))
        out_shape.append(jax.ShapeDtypeStruct((N, K) if w_t else (K, N), BF16))
    res = pl.pallas_call(
        functools.partial(_mm_kernel, nk=nk, n_tile=len(tile_extras), n_row=len(row_extras),
                          epilogue=epilogue, w_t=w_t, emit_bf16=emit_bf16),
        grid=(M // tm, N // tn, nk),
        in_specs=in_specs,
        out_specs=out_specs,
        out_shape=out_shape,
        scratch_shapes=scratch,
        compiler_params=_cparams(("parallel", "parallel", "arbitrary")),
        name=name,
    )(x, w, *tile_extras, *row_extras)
    return res if emit_bf16 else res[0]


def _mm_deep_kernel(x_ref, w_ref, r_ref, o_ref, acc_ref, *, nk):
    k = pl.program_id(1)
    j = pl.program_id(2)
    part = jnp.dot(x_ref[...], w_ref[...], preferred_element_type=F32)

    @pl.when(k == 0)
    def _():
        acc_ref[j] = part

    @pl.when((k > 0) & (k < nk - 1))
    def _():
        acc_ref[j] += part

    @pl.when(k == nk - 1)
    def _():
        o_ref[...] = r_ref[...] + (acc_ref[j] + part)


def matmul_deep_residual(x, w, res):
    M, K = x.shape
    N = w.shape[1]
    tm = _pick(M, (1024, 512, 256, 32))
    tn = _pick(N, (512, 256, 128))
    tk = _pick(K, (4096, 2048, 1024, 512))
    nk = K // tk
    assert w.shape[0] == K and M % tm == 0 and N % tn == 0 and nk >= 2
    last = lambda k, j: jnp.where(k == nk - 1, j, 0)
    return pl.pallas_call(
        functools.partial(_mm_deep_kernel, nk=nk),
        grid=(M // tm, nk, N // tn),
        in_specs=[pl.BlockSpec((tm, tk), lambda i, k, j: (i, k)),
                  pl.BlockSpec((tk, tn), lambda i, k, j: (k, j)),
                  pl.BlockSpec((tm, tn), lambda i, k, j: (i, last(k, j)))],
        out_specs=pl.BlockSpec((tm, tn), lambda i, k, j: (i, last(k, j))),
        out_shape=jax.ShapeDtypeStruct((M, N), F32),
        scratch_shapes=[pltpu.VMEM((N // tn, tm, tn), F32)],
        compiler_params=_cparams(("parallel", "arbitrary", "arbitrary")),
        name="mlp_down",
    )(x, w, res)


def _headnorm_epilogue(acc, nw, flag):
    outs = []
    for c in range(acc.shape[1] // HEAD_DIM):
        sl = slice(c * HEAD_DIM, (c + 1) * HEAD_DIM)
        z = acc[:, sl]
        zn = z * lax.rsqrt(jnp.mean(z * z, axis=-1, keepdims=True) + EPS) * nw[:, sl]
        outs.append(jnp.where(flag[:, sl] != 0.0, zn, z))
    return jnp.concatenate(outs, axis=1)


def _residual_epilogue(acc, res):
    return res + acc


def _relu2_epilogue(acc):
    r = jnp.maximum(acc, 0.0)
    return r * r


def _merge_kernel(ya_ref, yb_ref, wa_ref, wb_ref, g0_ref, g1_ref, o_ref, *wq_refs):
    wa = wa_ref[...].astype(BF16)
    wb = wb_ref[...].astype(BF16)
    if wq_refs:
        wq_refs[0][...] = wa
        wq_refs[1][...] = wb
    pa = jnp.dot(ya_ref[...], wa, preferred_element_type=F32)
    pb = jnp.dot(yb_ref[...], wb, preferred_element_type=F32)
    mix = jax.nn.sigmoid(g0_ref[...]) * pa + jax.nn.sigmoid(g1_ref[...]) * pb
    o_ref[...] = mix.astype(o_ref.dtype)


def gated_merge(ya, yb, wa, wb, layer, m_gate, gate_off=0, emit_bf16=False):
    M, Ka = ya.shape
    Kb = yb.shape[1]
    D = wa.shape[-1]
    tm = _pick(M, (1024, 512, 256, 32))
    tn = _pick(D, (512, 256, 128))
    nj = D // tn
    assert gate_off % tn == 0 and (not emit_bf16 or M == tm)
    goff = gate_off // tn
    if layer is None:
        w_spec = lambda k: pl.BlockSpec((k, tn), lambda i, j: (0, j))
    else:
        w_spec = lambda k: pl.BlockSpec((None, k, tn), lambda i, j: (layer, 0, j))
    out_specs = [pl.BlockSpec((tm, tn), lambda i, j: (i, j))]
    out_shape = [jax.ShapeDtypeStruct((M, D), BF16)]
    if emit_bf16:
        out_specs += [pl.BlockSpec((k, tn), lambda i, j: (0, j)) for k in (Ka, Kb)]
        out_shape += [jax.ShapeDtypeStruct((k, D), BF16) for k in (Ka, Kb)]
    res = pl.pallas_call(
        _merge_kernel,
        grid=(M // tm, nj),
        in_specs=[pl.BlockSpec((tm, Ka), lambda i, j: (i, 0)),
                  pl.BlockSpec((tm, Kb), lambda i, j: (i, 0)),
                  w_spec(Ka), w_spec(Kb),
                  pl.BlockSpec((tm, tn), lambda i, j: (i, goff + j)),
                  pl.BlockSpec((tm, tn), lambda i, j: (i, goff + nj + j))],
        out_specs=out_specs,
        out_shape=out_shape,
        compiler_params=_cparams(("parallel", "parallel")),
        name="gated_merge",
    )(ya, yb, wa, wb, m_gate, m_gate)
    return res if emit_bf16 else res[0]


def _silu(x):
    return x * jax.nn.sigmoid(x)


def _pair_total(cm, m, t_io):
    C = cm.shape[0]
    if m == 1:
        return jnp.where((t_io & 1) != 0, pltpu.roll(cm, 1, 0), cm)
    if m == 2:
        j = t_io & 3
        return jnp.where(j == 0, pltpu.roll(cm, C - 1, 0),
                         jnp.where(j == 1, cm,
                                   jnp.where(j == 2, pltpu.roll(cm, 1, 0), pltpu.roll(cm, 2, 0))))
    x3 = cm.reshape(C // (2 * m), 2 * m, cm.shape[1])
    return jnp.broadcast_to(x3[:, m - 1:m, :], x3.shape).reshape(cm.shape)


def _hgrn_chunk(aq, af, ai, ag, lb, nw, st, n_valid):
    C = aq.shape[0]
    t_io = lax.broadcasted_iota(jnp.int32, (C, LANES), 0)
    row_io = lax.broadcasted_iota(jnp.int32, (C, C), 0)
    col_io = lax.broadcasted_iota(jnp.int32, (C, C), 1)
    split = jnp.where(row_io > col_io, row_io ^ col_io, 0)
    for sh in (1, 2, 4, 8, 16):
        split = split | (split >> sh)
    split = split - (split >> 1)
    f = lb + (1.0 - lb) * jax.nn.sigmoid(af)
    if n_valid < C:
        f = jnp.where(t_io < n_valid, f, 1.0)
    g = jnp.log(f)
    kk = 1.0 - f
    qq = _silu(aq)
    v = ai.astype(BF16)
    a = jnp.zeros((C, C), F32)
    cm = g
    m = 1
    while m < C:
        tot = _pair_total(cm, m, t_io)
        odd = (t_io & m) != 0
        z = (jnp.where(odd, qq, kk) * jnp.exp(jnp.where(odd, cm, tot - cm))).astype(BF16)
        p = lax.dot_general(z, z, (((1,), (1,)), ((), ())), preferred_element_type=F32)
        a = a + jnp.where(split == m, p, 0.0)
        cm = cm + jnp.where(odd, tot, 0.0)
        m *= 2
    b = cm
    d = jnp.sum(qq * kk, axis=-1, keepdims=True)
    a = jnp.where(row_io == col_io, d, a)
    o_intra = jnp.dot(a.astype(BF16), v, preferred_element_type=F32)
    qe = (qq * jnp.exp(b)).astype(BF16)
    o_inter = lax.dot_general(qe, st.astype(BF16), (((1,), (1,)), ((), ())), preferred_element_type=F32)
    b_end = b[C - 1:C, :]
    ku = (kk * jnp.exp(b_end - b)).astype(BF16)
    ut = lax.dot_general(v, ku, (((0,), (0,)), ((), ())), preferred_element_type=F32)
    st_new = st * jnp.exp(b_end) + ut
    o = o_inter + o_intra
    on = o * lax.rsqrt(jnp.mean(o * o, axis=-1, keepdims=True) + EPS) * nw
    return on * _silu(ag), st_new


def _hgrn_kernel(*refs, n_sub, chunk, n_valid, has_s0, hps):
    aq_ref, af_ref, ai_ref, ag_ref, lb_ref, nw_ref = refs[:6]
    s0_ref = refs[6] if has_s0 else None
    y_ref, s_ref, st_ref = refs[6 + has_s0:]
    c = pl.program_id(2)
    nw = nw_ref[...]
    for hh in range(hps):
        cols = slice(hh * LANES, (hh + 1) * LANES)

        @pl.when(c == 0)
        def _():
            st_ref[hh] = s0_ref[hh].T if has_s0 else jnp.zeros((LANES, LANES), F32)

        lb = lb_ref[:, cols]
        for j in range(n_sub):
            rows = pl.ds(j * chunk, chunk)
            y, st_new = _hgrn_chunk(aq_ref[rows, cols], af_ref[rows, cols], ai_ref[rows, cols],
                                    ag_ref[rows, cols], lb, nw, st_ref[hh], n_valid)
            st_ref[hh] = st_new
            y_ref[rows, cols] = y.astype(y_ref.dtype)

        @pl.when(c == pl.num_programs(2) - 1)
        def _():
            s_ref[hh] = st_ref[hh].T


def hgrn(za, lb, nw, s0, B, L, H, chunk, n_valid=None):
    dk = LANES
    assert L % chunk == 0 and (n_valid is None or L == chunk)
    n_sub = _pick(L // chunk, (4, 2, 1))
    tc = n_sub * chunk
    nc = L // tc
    hps = _pick(H, tuple(h for h in (16, 8, 4, 2, 1) if h * tc <= 2048))
    assert H % hps == 0
    hb = H // hps
    row = lambda k: pl.BlockSpec((tc, hps * dk), lambda b, h, c: (b * nc + c, k * hb + h))
    state = pl.BlockSpec((None, hps, dk, dk), lambda b, h, c: (b, h, 0, 0))
    has_s0 = s0 is not None
    return pl.pallas_call(
        functools.partial(_hgrn_kernel, n_sub=n_sub, chunk=chunk,
                          n_valid=chunk if n_valid is None else n_valid, has_s0=has_s0, hps=hps),
        grid=(B, hb, nc),
        in_specs=[row(0), row(1), row(2), row(3),
                  pl.BlockSpec((1, hps * dk), lambda b, h, c: (0, h)),
                  pl.BlockSpec((1, dk), lambda b, h, c: (0, 0))] + ([state] if has_s0 else []),
        out_specs=[pl.BlockSpec((tc, hps * dk), lambda b, h, c: (b * nc + c, h)), state],
        out_shape=[jax.ShapeDtypeStruct((B * L, H * dk), BF16),
                   jax.ShapeDtypeStruct((B, H, dk, dk), F32)],
        scratch_shapes=[pltpu.VMEM((hps, dk, dk), F32)],
        compiler_params=_cparams(("parallel", "parallel", "arbitrary")),
        name="hgrn",
    )(za, za, za, za, lb.reshape(1, -1), nw.reshape(1, dk), *([s0] if has_s0 else []))


def hgrn_prompt(za, lb, nw, B, L, H):
    return hgrn(za, lb, nw, None, B, L, H, HGRN_CHUNK)


def _pool_rows(x, wl, nw):
    wl = wl - jnp.max(wl, axis=0, keepdims=True)
    e = jnp.exp(wl)
    w = e / jnp.sum(e, axis=0, keepdims=True)
    R = x.shape[0] // BLOCK
    pooled = jnp.sum(x.reshape(R, BLOCK, x.shape[1]) * w[None], axis=1)
    half = x.shape[1] // 2
    outs = []
    for c in range(x.shape[1] // HEAD_DIM):
        z = pooled[:, c * HEAD_DIM:(c + 1) * HEAD_DIM]
        if c * HEAD_DIM < half:
            z = z * lax.rsqrt(jnp.mean(z * z, axis=-1, keepdims=True) + EPS) * nw
        outs.append(z)
    return jnp.concatenate(outs, axis=1)


def _pool_kernel(x_ref, wl_ref, nw_ref, o_ref):
    o_ref[...] = _pool_rows(x_ref[...], wl_ref[...], nw_ref[...])


def pool_prompt(kv, pos_logits, kn_w):
    M = kv.shape[0]
    kvw2 = kv.shape[1] // 2
    rb = SUBLANES * BLOCK
    assert M % rb == 0
    return pl.pallas_call(
        _pool_kernel,
        grid=(M // rb,),
        in_specs=[pl.BlockSpec((rb, kvw2), lambda i: (i, 0)),
                  pl.BlockSpec((BLOCK, 1), lambda i: (0, 0)),
                  pl.BlockSpec((1, HEAD_DIM), lambda i: (0, 0))],
        out_specs=pl.BlockSpec((SUBLANES, kvw2), lambda i: (i, 0)),
        out_shape=jax.ShapeDtypeStruct((M // BLOCK, kvw2), F32),
        compiler_params=_cparams(("parallel",)),
        name="pool_prompt",
    )(kv, pos_logits.reshape(BLOCK, 1), kn_w.reshape(1, HEAD_DIM))


NSA_TQ = 256
NSA_TK = 256


def _select_blocks(imp, qpos, n_cand, score_ref):
    cand = lax.broadcasted_iota(jnp.int32, imp.shape, 0)
    cur = qpos // BLOCK
    valid = cand <= cur
    forced = (cand == 0) | (cand > cur - N_LOCAL)
    score = jnp.where(valid, imp + jnp.where(forced, FORCE_BONUS, 0.0), -jnp.inf)
    score_ref[...] = score

    def body(m, rank):
        row = score_ref[pl.ds(m, 1), :]
        beats = (row > score) | ((row == score) & (cand > m))
        return rank + jnp.where(beats, 1.0, 0.0)

    rank = lax.fori_loop(0, n_cand, body, jnp.zeros(imp.shape, F32), unroll=n_cand <= 32)
    return valid & (rank < float(N_SEL))


class _Flash:
    def __init__(self, m_ref, l_ref, acc_ref):
        self.m_ref, self.l_ref, self.acc_ref = m_ref, l_ref, acc_ref

    def init(self):
        self.m_ref[...] = jnp.full(self.m_ref.shape, MASK_VALUE, F32)
        self.l_ref[...] = jnp.zeros(self.l_ref.shape, F32)
        self.acc_ref[...] = jnp.zeros(self.acc_ref.shape, F32)

    def step(self, k, v, qt_b, *, scale=None, bias=None, mask=None):
        s = jnp.dot(k.astype(BF16), qt_b, preferred_element_type=F32)
        if scale is not None:
            s = s * scale
        if bias is not None:
            s = s + bias
        if mask is not None:
            s = jnp.where(mask, s, MASK_VALUE)
        m_old = self.m_ref[...]
        m_new = jnp.maximum(m_old, jnp.max(s, axis=0, keepdims=True))
        alpha = jnp.exp2(m_old - m_new)
        p = jnp.exp2(s - m_new)
        self.l_ref[...] = alpha * self.l_ref[...] + jnp.sum(p, axis=0, keepdims=True)
        pv = lax.dot_general(v.astype(BF16), p.astype(BF16), (((0,), (0,)), ((), ())),
                             preferred_element_type=F32)
        self.acc_ref[...] = alpha * self.acc_ref[...] + pv
        self.m_ref[...] = m_new

    def result(self):
        return self.acc_ref[...] * (1.0 / self.l_ref[...])


def _nsa_prompt_kernel(q_ref, ks_ref, vs_ref, kw_ref, vw_ref, kc_ref, vc_ref, g_ref, o_ref,
                       sel_ref, score_ref, ms_ref, ls_ref, accs_ref, mw_ref, lw_ref, accw_ref, *, nb):
    tq = NSA_TQ
    qt = pl.program_id(2)
    nh = NSA_GROUP
    q = q_ref[...] * SCALE
    qT = jnp.concatenate([q[:, h * HEAD_DIM:(h + 1) * HEAD_DIM].T for h in range(nh)], axis=1)
    qt_b = (qT * LOG2E).astype(BF16)
    lane = lax.broadcasted_iota(jnp.int32, (1, nh * tq), 1)
    qpos = qt * tq + (lane & (tq - 1))

    s = jnp.dot(kc_ref[...], qT, preferred_element_type=F32, precision=lax.Precision.HIGHEST)
    n_io = lax.broadcasted_iota(jnp.int32, s.shape, 0)
    ready = ((n_io + 1) * BLOCK - 1) <= qpos
    s = jnp.where(ready, s, MASK_VALUE)
    e = jnp.exp(s - jnp.max(s, axis=0, keepdims=True))
    p = e / jnp.sum(e, axis=0, keepdims=True) * jnp.where(ready, 1.0, 0.0)
    o_cmp = lax.dot_general(vc_ref[...].astype(BF16), p.astype(BF16), (((0,), (0,)), ((), ())),
                            preferred_element_type=F32)
    imp = p[:, 0:tq]
    for h in range(1, nh):
        imp = imp + p[:, h * tq:(h + 1) * tq]
    sel = _select_blocks(imp, qpos[:, 0:tq], nb, score_ref)
    sel_bias = jnp.where(sel, 0.0, MASK_VALUE)
    tk = NSA_TK
    bpt = tk // BLOCK
    for n in range(nb):
        sel_ref[n // bpt, n % bpt:n % bpt + 1, :] = sel_bias[n:n + 1, :]

    krow = lax.broadcasted_iota(jnp.int32, (tk, nh * tq), 0)
    sel_acc = _Flash(ms_ref, ls_ref, accs_ref)
    win_acc = _Flash(mw_ref, lw_ref, accw_ref)
    sel_acc.init()
    win_acc.init()

    def tile_bias(kt):
        blk = sel_ref[kt]
        rows = [jnp.broadcast_to(jnp.concatenate([blk[j:j + 1, :]] * nh, axis=1), (BLOCK, nh * tq))
                for j in range(bpt)]
        return jnp.concatenate(rows, axis=0)

    def sel_step(kt, mask=None):
        rows = pl.ds(pl.multiple_of(kt * tk, tk), tk)
        sel_acc.step(ks_ref[rows, :], vs_ref[rows, :], qt_b, bias=tile_bias(kt), mask=mask)

    def win_step(kt, mask):
        rows = pl.ds(pl.multiple_of(kt * tk, tk), tk)
        win_acc.step(kw_ref[rows, :], vw_ref[rows, :], qt_b, mask=mask)

    kd = qt * (tq // tk)
    w_lo = jnp.maximum(kd - WINDOW // tk, 0)

    def far_body(kt, carry):
        sel_step(kt)
        return carry

    lax.fori_loop(0, w_lo, far_body, 0)

    def near_body(kt, carry):
        sel_step(kt)
        win_step(kt, (qpos - (kt * tk + krow)) < WINDOW)
        return carry

    lax.fori_loop(w_lo, kd, near_body, 0)

    for d in range(tq // tk):
        causal = ((kd + d) * tk + krow) <= qpos
        sel_step(kd + d, causal)
        win_step(kd + d, causal)
    o_sel = sel_acc.result()
    o_win = win_acc.result()

    gT = jax.nn.sigmoid(g_ref[...]).T
    for h in range(nh):
        sl = slice(h * tq, (h + 1) * tq)
        y = (gT[3 * h:3 * h + 1, :] * o_cmp[:, sl] + gT[3 * h + 1:3 * h + 2, :] * o_sel[:, sl]
             + gT[3 * h + 2:3 * h + 3, :] * o_win[:, sl])
        o_ref[:, h * HEAD_DIM:(h + 1) * HEAD_DIM] = y.T.astype(o_ref.dtype)


def nsa_prompt(q, kv, win, pooled, gates, B, L):
    tq = NSA_TQ
    G, nh, hd = NSA_KV, NSA_GROUP, HEAD_DIM
    tk = NSA_TK
    assert L % tq == 0 and tq % tk == 0 and tk % BLOCK == 0 and WINDOW % tk == 0
    nq = L // tq
    nb = L // BLOCK
    full = lambda off: pl.BlockSpec((L, hd), lambda b, g, t: (b, off + g))
    return pl.pallas_call(
        functools.partial(_nsa_prompt_kernel, nb=nb),
        grid=(B, G, nq),
        in_specs=[pl.BlockSpec((tq, nh * hd), lambda b, g, t: (b * nq + t, g)),
                  full(2 * G), full(3 * G),
                  pl.BlockSpec((L, hd), lambda b, g, t: (b, g)),
                  pl.BlockSpec((L, hd), lambda b, g, t: (b, G + g)),
                  pl.BlockSpec((nb, hd), lambda b, g, t: (b, g)),
                  pl.BlockSpec((nb, hd), lambda b, g, t: (b, G + g)),
                  pl.BlockSpec((tq, LANES), lambda b, g, t: (b * nq + t, g))],
        out_specs=pl.BlockSpec((tq, nh * hd), lambda b, g, t: (b * nq + t, g)),
        out_shape=jax.ShapeDtypeStruct((B * L, G * nh * hd), BF16),
        scratch_shapes=[pltpu.VMEM((nb * BLOCK // tk, tk // BLOCK, tq), F32),
                        pltpu.VMEM((nb, tq), F32)]
        + 2 * [pltpu.VMEM((1, nh * tq), F32), pltpu.VMEM((1, nh * tq), F32), pltpu.VMEM((hd, nh * tq), F32)],
        compiler_params=_cparams(("parallel", "parallel", "arbitrary")),
        name="nsa_prompt",
    )(q, kv, kv, win, win, pooled, pooled, gates)


DEC_PAGES_PER_STEP = 32
DEC_ROWS = 8


def _dec_qpos(past_len, n_new, shape):
    lane = lax.broadcasted_iota(jnp.int32, shape, len(shape) - 1)
    return past_len + lax.rem(lane & (LANES - 1), n_new)


def _page_spec(cache, layer, half, i, npg):
    return pl.BlockSpec((None, None, cache.shape[2], None) + cache.shape[4:],
                        lambda b, s, pt: (layer, pt[b, s * npg + i], 0, half, 0, 0))


def _page_rows(page_ref, r):
    n, rows, hd = page_ref.shape
    return page_ref.reshape(n * rows, hd)[pl.ds(r, n, stride=rows), :]


def _pool_paged_kernel(pt_ref, *refs):
    del pt_ref
    npg = DEC_PAGES_PER_STEP
    wl_ref, nw_ref, o_ref = refs[npg:npg + 3]
    wl = wl_ref[...]
    e = jnp.exp(wl - jnp.max(wl, axis=0, keepdims=True))
    w = e / jnp.sum(e, axis=0, keepdims=True)
    bpp = refs[0].shape[0] // BLOCK
    for i in range(npg):
        for r in range(2 * NSA_KV):
            x = _page_rows(refs[i], r)
            z = jnp.sum(x.reshape(bpp, BLOCK, HEAD_DIM) * w[None], axis=1)
            if r < NSA_KV:
                z = z * lax.rsqrt(jnp.mean(z * z, axis=-1, keepdims=True) + EPS) * nw_ref[...]
            o_ref[r, i * bpp:(i + 1) * bpp, :] = z


def pool_paged(cache, layer, page_table, pos_logits, kn_w):
    B, n_pages = page_table.shape
    page = cache.shape[2]
    npg = DEC_PAGES_PER_STEP
    assert n_pages % npg == 0 and page % BLOCK == 0
    bpp = page // BLOCK
    assert (npg * bpp) % SUBLANES == 0
    return pl.pallas_call(
        _pool_paged_kernel,
        grid_spec=pltpu.PrefetchScalarGridSpec(
            num_scalar_prefetch=1,
            grid=(B, n_pages // npg),
            in_specs=[_page_spec(cache, layer, 0, i, npg) for i in range(npg)]
            + [pl.BlockSpec((BLOCK, 1), lambda b, s, pt: (0, 0)),
               pl.BlockSpec((1, HEAD_DIM), lambda b, s, pt: (0, 0))],
            out_specs=pl.BlockSpec((None, 2 * NSA_KV, npg * bpp, HEAD_DIM), lambda b, s, pt: (b, 0, s, 0)),
        ),
        out_shape=jax.ShapeDtypeStruct((B, 2 * NSA_KV, n_pages * bpp, HEAD_DIM), F32),
        compiler_params=_cparams(("parallel", "arbitrary")),
        name="pool_paged",
    )(page_table, *([cache] * npg), pos_logits.reshape(BLOCK, 1), kn_w.reshape(1, HEAD_DIM))


def _nsa_dec_front_kernel(qbd_ref, pooled_ref, cwin_ref, nwin_ref, gl_ref, part_ref, sel_ref,
                          score_ref, m_ref, l_ref, acc_ref, *, past_len, n_new, n_cand):
    kvw = qbd_ref.shape[0]
    nl = qbd_ref.shape[1]
    qbd = qbd_ref[...]
    qbd_b = qbd.astype(BF16)
    qpos = _dec_qpos(past_len, n_new, (1, nl))

    nf = pooled_ref.shape[1]
    kc = jnp.concatenate([pooled_ref[g] for g in range(NSA_KV)], axis=1)
    vc = jnp.concatenate([pooled_ref[NSA_KV + g] for g in range(NSA_KV)], axis=1)
    s = jnp.dot(kc, qbd, preferred_element_type=F32, precision=lax.Precision.HIGHEST) * SCALE
    n_io = lax.broadcasted_iota(jnp.int32, s.shape, 0)
    ready = ((n_io + 1) * BLOCK - 1) <= qpos
    s = jnp.where(ready, s, MASK_VALUE)
    e = jnp.exp(s - jnp.max(s, axis=0, keepdims=True))
    p = e / jnp.sum(e, axis=0, keepdims=True) * jnp.where(ready, 1.0, 0.0)
    o_cmp = lax.dot_general(vc.astype(BF16), p.astype(BF16), (((0,), (0,)), ((), ())),
                            preferred_element_type=F32)
    imp = p[:, 0:LANES]
    for h in range(1, NSA_GROUP):
        imp = imp + p[:, h * LANES:(h + 1) * LANES]
    imp = jnp.concatenate([imp, jnp.zeros((score_ref.shape[0] - nf, LANES), F32)], axis=0)
    sel = _select_blocks(imp, qpos[:, 0:LANES], n_cand, score_ref)
    sel_ref[...] = jnp.where(sel, 0.0, MASK_VALUE)

    win_acc = _Flash(m_ref, l_ref, acc_ref)
    win_acc.init()
    w_rows = cwin_ref.shape[0]
    krow = lax.broadcasted_iota(jnp.int32, (w_rows, nl), 0)
    d = qpos - (past_len - w_rows + krow)
    win_acc.step(cwin_ref[:, 0:kvw], cwin_ref[:, kvw:2 * kvw], qbd_b, scale=SCALE * LOG2E,
                 mask=(d >= 0) & (d < WINDOW))
    nrow = lax.broadcasted_iota(jnp.int32, (nwin_ref.shape[0], nl), 0)
    d = qpos - (past_len + nrow)
    win_acc.step(nwin_ref[:, 0:kvw], nwin_ref[:, kvw:2 * kvw], qbd_b, scale=SCALE * LOG2E,
                 mask=(d >= 0) & (d < WINDOW) & (nrow < n_new))
    g = jax.nn.sigmoid(gl_ref[...])
    part_ref[...] = g[0:1, :] * o_cmp + g[2:3, :] * win_acc.result()


def _nsa_dec_sel_kernel(pt_ref, *refs, past_len, n_new, page):
    del pt_ref
    npg = DEC_PAGES_PER_STEP
    page_refs = refs[:npg]
    qg_ref, sel_ref, sel_new_ref, nkv_ref, part_ref, g1_ref, y_ref, m_ref, l_ref, acc_ref = refs[npg:]
    G, hd = NSA_KV, HEAD_DIM
    step = pl.program_id(1)
    chains = [_Flash(m_ref.at[g], l_ref.at[g], acc_ref.at[g]) for g in range(G)]
    qg_b = [qg_ref[g].astype(BF16) for g in range(G)]

    @pl.when(step == 0)
    def _():
        for c in chains:
            c.init()

    for g in range(G):
        bias = jnp.concatenate([jnp.broadcast_to(sel_ref[g, i][j:j + 1, :], (BLOCK, LANES))
                                for i in range(npg) for j in range(page // BLOCK)], axis=0)
        k = jnp.concatenate([_page_rows(page_refs[i], g) for i in range(npg)], axis=0)
        v = jnp.concatenate([_page_rows(page_refs[i], G + g) for i in range(npg)], axis=0)
        chains[g].step(k, v, qg_b[g], scale=SCALE * LOG2E, bias=bias)

    @pl.when(step == pl.num_programs(1) - 1)
    def _():
        nrow = lax.broadcasted_iota(jnp.int32, (nkv_ref.shape[0], LANES), 0)
        lane = lax.broadcasted_iota(jnp.int32, (1, LANES), 1)
        qpos = past_len + lax.rem(lane, n_new)
        mask = (past_len + nrow <= qpos) & (nrow < n_new)
        kvw = G * hd
        for g in range(G):
            chains[g].step(nkv_ref[:, g * hd:(g + 1) * hd], nkv_ref[:, kvw + g * hd:kvw + (g + 1) * hd],
                           qg_b[g], scale=SCALE * LOG2E, bias=sel_new_ref[g, 0][0:1, :], mask=mask)
            y = part_ref[g] + jax.nn.sigmoid(g1_ref[g][0:1, :]) * chains[g].result()
            y_ref[g] = y.T


def nsa_decode(q, kv_new, win_new, gates, cache, layer, page_table, cache_win, pos_logits, kn_w, n_new):
    B, n_pages = page_table.shape
    page = cache.shape[2]
    G, nh, hd = NSA_KV, NSA_GROUP, HEAD_DIM
    kvw = G * hd
    nl = nh * LANES
    past_len = n_pages * page
    npg = DEC_PAGES_PER_STEP
    bpp = page // BLOCK
    n_cand = -(-(past_len + n_new) // BLOCK)
    n_rows = -(-(n_cand + bpp) // SUBLANES) * SUBLANES // bpp * bpp
    assert G * n_new <= LANES and n_new <= DEC_ROWS and n_new <= BLOCK and past_len % BLOCK == 0

    q5 = q.reshape(B, n_new, G, nh, hd)
    qt = jnp.transpose(q5, (0, 2, 4, 3, 1))
    qbd = qt[:, :, :, :, None, :] * jnp.eye(G, dtype=F32)[None, :, None, None, :, None]
    qbd = jnp.pad(qbd.reshape(B, kvw, nh, G * n_new), ((0, 0), (0, 0), (0, 0), (0, LANES - G * n_new)))
    qbd = qbd.reshape(B, kvw, nl)
    gl = gates.reshape(B, n_new, G, LANES)[..., :3 * nh].reshape(B, n_new, G, nh, 3)
    gl = jnp.transpose(gl, (0, 4, 3, 2, 1)).reshape(B, 3, nh, G * n_new)
    gl = jnp.pad(gl, ((0, 0), (0, DEC_ROWS - 3), (0, 0), (0, LANES - G * n_new))).reshape(B, DEC_ROWS, nl)
    pad_rows = lambda t: jnp.pad(t.reshape(B, n_new, -1), ((0, 0), (0, DEC_ROWS - n_new), (0, 0)))
    nkv = pad_rows(kv_new)
    nwin = pad_rows(win_new)

    pooled = pool_paged(cache, layer, page_table, pos_logits, kn_w)
    nf = pooled.shape[2]
    per_b = lambda *shape: pl.BlockSpec((None,) + shape, lambda b: (b,) + (0,) * len(shape))
    part, sel = pl.pallas_call(
        functools.partial(_nsa_dec_front_kernel, past_len=past_len, n_new=n_new, n_cand=n_cand),
        grid=(B,),
        in_specs=[per_b(kvw, nl), per_b(2 * G, nf, hd), per_b(cache_win.shape[1], 2 * kvw),
                  per_b(DEC_ROWS, 2 * kvw), per_b(DEC_ROWS, nl)],
        out_specs=[per_b(kvw, nl), per_b(n_rows, LANES)],
        out_shape=[jax.ShapeDtypeStruct((B, kvw, nl), F32), jax.ShapeDtypeStruct((B, n_rows, LANES), F32)],
        scratch_shapes=[pltpu.VMEM((n_rows, LANES), F32), pltpu.VMEM((1, nl), F32),
                        pltpu.VMEM((1, nl), F32), pltpu.VMEM((kvw, nl), F32)],
        compiler_params=_cparams(("parallel",)),
        name="nsa_dec_front",
    )(qbd, pooled, cache_win, nwin, gl)

    nu = nh * n_new
    lane_pad = lambda t: jnp.pad(t, [(0, 0)] * (t.ndim - 1) + [(0, LANES - nu)])
    qg = lane_pad(qt.reshape(B, G, hd, nu))
    p6 = part.reshape(B, G, hd, nh, LANES)[..., :G * n_new].reshape(B, G, hd, nh, G, n_new)
    partg = lane_pad(jnp.stack([p6[:, g, :, :, g, :] for g in range(G)], axis=1).reshape(B, G, hd, nu))
    selg = jnp.transpose(sel[:, :, :G * n_new].reshape(B, n_rows, G, n_new), (0, 2, 1, 3))
    selg = lane_pad(jnp.tile(selg, (1, 1, 1, nh))).reshape(B, G, n_rows // bpp, bpp, LANES)
    g1 = gates.reshape(B, n_new, G, LANES)[..., :3 * nh].reshape(B, n_new, G, nh, 3)[..., 1]
    g1g = lane_pad(jnp.transpose(g1, (0, 2, 3, 1)).reshape(B, G, 1, nu))
    g1g = jnp.pad(g1g, ((0, 0), (0, 0), (0, SUBLANES - 1), (0, 0)))

    page_spec = lambda i: _page_spec(cache, layer, 1, i, npg)
    per_bs = lambda *shape: pl.BlockSpec((None,) + shape, lambda b, s, pt: (b,) + (0,) * len(shape))
    y = pl.pallas_call(
        functools.partial(_nsa_dec_sel_kernel, past_len=past_len, n_new=n_new, page=page),
        grid_spec=pltpu.PrefetchScalarGridSpec(
            num_scalar_prefetch=1,
            grid=(B, n_pages // npg),
            in_specs=[page_spec(i) for i in range(npg)]
            + [per_bs(G, hd, LANES),
               pl.BlockSpec((None, G, npg, bpp, LANES), lambda b, s, pt: (b, 0, s, 0, 0)),
               pl.BlockSpec((None, G, 1, bpp, LANES), lambda b, s, pt: (b, 0, n_pages, 0, 0)),
               pl.BlockSpec((None, DEC_ROWS, 2 * kvw), lambda b, s, pt: (b, 0, 1)),
               per_bs(G, hd, LANES),
               per_bs(G, SUBLANES, LANES)],
            out_specs=per_bs(G, LANES, hd),
            scratch_shapes=[pltpu.VMEM((G, 1, LANES), F32), pltpu.VMEM((G, 1, LANES), F32),
                            pltpu.VMEM((G, hd, LANES), F32)],
        ),
        out_shape=jax.ShapeDtypeStruct((B, G, LANES, hd), F32),
        compiler_params=_cparams(("parallel", "arbitrary")),
        name="nsa_dec_sel",
    )(page_table, *([cache] * npg), qg, selg, selg, nkv, partg, g1g)
    y = jnp.transpose(y[:, :, :nu].reshape(B, G, nh, n_new, hd), (0, 3, 1, 2, 4))
    return y.reshape(B * n_new, G * nh * hd).astype(BF16)


def _lower_bounds_kernel(x_ref, o_ref):
    x = x_ref[...]
    e = jnp.exp(x - jnp.max(x, axis=0, keepdims=True))
    p = e / jnp.sum(e, axis=0, keepdims=True)
    c = p[0:1, :]
    o_ref[0:1, :] = jnp.zeros_like(c)
    for i in range(1, x.shape[0]):
        c = c + p[i:i + 1, :]
        o_ref[i:i + 1, :] = c - p[0:1, :]


def hgrn_lower_bounds(lb_logits):
    return pl.pallas_call(
        _lower_bounds_kernel,
        out_shape=jax.ShapeDtypeStruct(lb_logits.shape, F32),
        name="hgrn_lower_bounds",
    )(lb_logits)


GATE_W = NSA_KV * LANES


def _gate_weights(w_t, o_bg):
    n_bg = 3 * NSA_KV * NSA_GROUP
    depth, _, K = w_t.shape
    w_bg = w_t[:, o_bg:o_bg + n_bg].reshape(depth, NSA_KV, 3 * NSA_GROUP, K)
    return jnp.pad(w_bg, ((0, 0), (0, 0), (0, LANES - 3 * NSA_GROUP), (0, 0))).reshape(depth, GATE_W, K)


def _project_in(x, w_t, w_bg, wq, layer, norm1_w, q_norm_w, k_norm_w, hq, nq, kvw):
    h = rmsnorm_cast(x, norm1_w)
    o_q = 4 * hq
    o_kv = o_q + nq
    o_win = o_kv + 4 * kvw
    o_mg = o_win + 2 * kvw + 3 * NSA_KV * NSA_GROUP
    pieces = dict(hgrn=(0, o_q), q=(o_q, nq), kv=(o_kv, 4 * kvw), win=(o_win, 2 * kvw),
                  mg=(o_mg, w_t.shape[1] - o_mg))
    new = {}

    def proj(name, **kw):
        off, n = pieces[name]
        if wq is not None:
            return matmul(h, wq[name], out_dtype=F32, w_t=True, name="in_" + name, **kw)
        out, new[name] = matmul(h, w_t, layer=layer, col_off=off, n=n, out_dtype=F32, w_t=True, emit_bf16=True,
                                name="in_" + name, **kw)
        return out

    tile = lambda v, n: jnp.tile(v, n // HEAD_DIM).reshape(1, n)
    ones = lambda n: jnp.ones((1, n), F32)
    zeros = lambda n: jnp.zeros((1, n), F32)
    za = proj("hgrn")
    q = proj("q", epilogue=_headnorm_epilogue, row_extras=(tile(q_norm_w, nq), ones(nq)))
    kv_flag = jnp.concatenate([zeros(2 * kvw), ones(kvw), zeros(kvw)], axis=1)
    kv = proj("kv", epilogue=_headnorm_epilogue, row_extras=(tile(k_norm_w[1], 4 * kvw), kv_flag))
    win_flag = jnp.concatenate([ones(kvw), zeros(kvw)], axis=1)
    win = proj("win", epilogue=_headnorm_epilogue, row_extras=(tile(k_norm_w[2], 2 * kvw), win_flag))
    mg = proj("mg")
    if wq is not None:
        bg = matmul(h, wq["bg"], out_dtype=F32, w_t=True, name="in_bg")
    else:
        bg, new["bg"] = matmul(h, w_bg, layer=layer, out_dtype=F32, w_t=True, emit_bf16=True, name="in_bg")
    return za, q, kv, win, bg, mg, (wq if wq is not None else new)


def _finish_layer(x, ya, yb, mg, w_f32, wq, layer, norm2_w):
    if wq is not None:
        mix = gated_merge(ya, yb, wq["a"], wq["b"], None, mg)
        x1 = matmul(mix, wq["out"], out_dtype=F32, epilogue=_residual_epilogue, tile_extras=(x,),
                    name="out_proj")
        h2 = rmsnorm_cast(x1, norm2_w)
        u = matmul(h2, wq["up"], out_dtype=BF16, epilogue=_relu2_epilogue, name="mlp_up")
        return matmul_deep_residual(u, wq["down"], x1), wq
    new = {}
    mix, new["a"], new["b"] = gated_merge(ya, yb, w_f32["a"], w_f32["b"], layer, mg, emit_bf16=True)
    cast_mm = functools.partial(matmul, layer=layer, emit_bf16=True)
    x1, new["out"] = cast_mm(mix, w_f32["out"], out_dtype=F32, epilogue=_residual_epilogue, tile_extras=(x,),
                             name="out_proj")
    h2 = rmsnorm_cast(x1, norm2_w)
    u, new["up"] = cast_mm(h2, w_f32["up"], out_dtype=BF16, epilogue=_relu2_epilogue, name="mlp_up")
    x2, new["down"] = cast_mm(u, w_f32["down"], out_dtype=F32, epilogue=_residual_epilogue, tile_extras=(x1,),
                              name="mlp_down")
    return x2, new


def kernel(x_prompt, x_sample, cache_kv, cache_win, state_hgrn, page_table, norm1_w, w_in, hgrn_lb_logits,
           hgrn_norm_w, q_norm_w, k_norm_w, cmp_pos_logits, w_branch_a, w_branch_b, w_out, norm2_w, w_up,
           w_down):
    depth = w_in.shape[0]
    B, L, D = x_prompt.shape
    Bs, Ls, _ = x_sample.shape
    H = state_hgrn.shape[2]
    hq = H * state_hgrn.shape[3]
    G, hd = cache_kv.shape[4], cache_kv.shape[5]
    kvw = G * hd
    nq = w_branch_b.shape[1]
    assert (G, hd) == (NSA_KV, HEAD_DIM) and nq == NSA_KV * NSA_GROUP * HEAD_DIM
    assert state_hgrn.shape[3] == LANES and state_hgrn.shape[4] == LANES
    lbs = hgrn_lower_bounds(hgrn_lb_logits)
    cache = cache_kv.reshape(cache_kv.shape[:3] + (2, 2 * G, hd))
    dec_chunk = 2 * SUBLANES
    assert Ls <= dec_chunk

    xp = x_prompt.reshape(B * L, D)
    xs = x_sample.reshape(Bs * Ls, D)
    kv_p, kv_s, win_p, win_s, st_p, st_s = [], [], [], [], [], []
    w_t = jnp.swapaxes(w_in, 1, 2)
    w_bg = _gate_weights(w_t, 4 * hq + nq + 6 * kvw)
    w_f32 = dict(a=w_branch_a, b=w_branch_b, out=w_out, up=w_up, down=w_down)
    for l in range(depth):
        norms = (norm1_w[l], q_norm_w[l], k_norm_w[l], hq, nq, kvw)
        za, q, kv, win, bg, mg, wq_in = _project_in(xs, w_t, w_bg, None, l, *norms)
        za_pad = jnp.pad(za.reshape(Bs, Ls, -1), ((0, 0), (0, dec_chunk - Ls), (0, 0)))
        ya, st = hgrn(za_pad.reshape(Bs * dec_chunk, -1), lbs[l], hgrn_norm_w[l], state_hgrn[l],
                      Bs, dec_chunk, H, dec_chunk, n_valid=Ls)
        ya = ya.reshape(Bs, dec_chunk, -1)[:, :Ls].reshape(Bs * Ls, -1)
        cwin = cache_win[l].reshape(Bs, cache_win.shape[2], 2 * kvw)
        yb = nsa_decode(q, kv, win, bg, cache, l, page_table, cwin, cmp_pos_logits[l],
                        k_norm_w[l, 0], Ls)
        xs, wq_out = _finish_layer(xs, ya, yb, mg, w_f32, None, l, norm2_w[l])
        kv_s.append(kv.reshape(Bs, Ls, 4, G, hd))
        win_all = jnp.concatenate([cwin, win.reshape(Bs, Ls, 2 * kvw)], axis=1)
        ws = min(WINDOW, win_all.shape[1])
        win_s.append(win_all[:, win_all.shape[1] - ws:].reshape(Bs, ws, 2, G, hd))
        st_s.append(st)
        za, q, kv, win, bg, mg, _ = _project_in(xp, w_t, w_bg, wq_in, l, *norms)
        ya, st = hgrn(za, lbs[l], hgrn_norm_w[l], None, B, L, H, HGRN_CHUNK)
        pooled = pool_prompt(kv, cmp_pos_logits[l], k_norm_w[l, 0])
        yb = nsa_prompt(q, kv, win, pooled, bg, B, L)
        xp, _ = _finish_layer(xp, ya, yb, mg, w_f32, wq_out, l, norm2_w[l])
        kv_p.append(kv.reshape(B, L, 4, G, hd))
        wk = min(WINDOW, L)
        win_p.append(win.reshape(B, L, 2, G, hd)[:, L - wk:])
        st_p.append(st)
    return (xp.reshape(B, L, D), xs.reshape(Bs, Ls, D), jnp.stack(kv_p), jnp.stack(kv_s),
            jnp.stack(win_p), jnp.stack(win_s), jnp.stack(st_p).astype(state_hgrn.dtype),
            jnp.stack(st_s).astype(state_hgrn.dtype))
```

```python
import functools
import math

import jax
import jax.numpy as jnp
from jax import lax
from jax.experimental import pallas as pl
from jax.experimental.pallas import tpu as pltpu

F32 = jnp.float32
BF16 = jnp.bfloat16

LANES = 128
SUBLANES = 8
VMEM_LIMIT = 56 * 1024 * 1024

HEAD_DIM = 128
NSA_KV = 4
NSA_GROUP = 4
BLOCK = 64
N_SEL = 16
N_LOCAL = 2
WINDOW = 512
FORCE_BONUS = float(NSA_GROUP + 1)
SCALE = HEAD_DIM ** -0.5
EPS = 1e-6
MASK_VALUE = -1e30
LOG2E = math.log2(math.e)
HGRN_CHUNK = 128


def _cparams(sem):
    return pltpu.CompilerParams(dimension_semantics=sem, vmem_limit_bytes=VMEM_LIMIT)


def _pick(n, prefs):
    for p in prefs:
        if n % p == 0:
            return p
    return n


def _rmsnorm_kernel(x_ref, w_ref, o_ref):
    x = x_ref[...]
    y = x * lax.rsqrt(jnp.mean(x * x, axis=-1, keepdims=True) + EPS)
    o_ref[...] = (y * w_ref[...]).astype(o_ref.dtype)


def rmsnorm_cast(x, w):
    M, D = x.shape
    tm = _pick(M, (256, 32))
    return pl.pallas_call(
        _rmsnorm_kernel,
        grid=(M // tm,),
        in_specs=[pl.BlockSpec((tm, D), lambda i: (i, 0)),
                  pl.BlockSpec((1, D), lambda i: (0, 0))],
        out_specs=pl.BlockSpec((tm, D), lambda i: (i, 0)),
        out_shape=jax.ShapeDtypeStruct((M, D), BF16),
        compiler_params=_cparams(("parallel",)),
        name="rmsnorm_cast",
    )(x, w.reshape(1, D))


def _mm_kernel(*refs, nk, n_tile, n_row, epilogue, w_t, emit_bf16):
    x_ref, w_ref = refs[0], refs[1]
    tile_refs = refs[2:2 + n_tile]
    row_refs = refs[2 + n_tile:2 + n_tile + n_row]
    o_ref = refs[2 + n_tile + n_row]
    n_out = 2 if emit_bf16 else 1

    def finish(acc):
        extras = [r[...] for r in tile_refs] + [r[...] for r in row_refs]
        o_ref[...] = epilogue(acc, *extras).astype(o_ref.dtype)

    w = w_ref[0] if len(w_ref.shape) == 3 else w_ref[...]
    if emit_bf16:
        w = w.astype(BF16)
        refs[3 + n_tile + n_row][...] = w
    if w_t:
        part = lax.dot_general(x_ref[...], w, (((1,), (1,)), ((), ())), preferred_element_type=F32)
    else:
        part = jnp.dot(x_ref[...], w, preferred_element_type=F32)
    if nk == 1:
        finish(part)
    else:
        acc_ref = refs[2 + n_out + n_tile + n_row]
        k = pl.program_id(2)

        @pl.when(k == 0)
        def _():
            acc_ref[...] = part

        @pl.when(k > 0)
        def _():
            acc_ref[...] += part

        @pl.when(k == nk - 1)
        def _():
            finish(acc_ref[...])


def matmul(x, w, *, out_dtype, layer=None, col_off=0, n=None, epilogue=None, tile_extras=(), row_extras=(),
           tm=None, tn=None, tk=None, w_t=False, emit_bf16=False, name="matmul"):
    M, K = x.shape
    kax, nax = (-1, -2) if w_t else (-2, -1)
    assert w.shape[kax] == K and (layer is None) == (w.ndim == 2)
    N = n or w.shape[nax]
    tm = tm or _pick(M, (1024, 512, 256, 32))
    tn = tn or _pick(N, (512,) if emit_bf16 else (1024, 512, 256, 128))
    tk = tk or (K if K <= 4096 else _pick(K, (2048,)))
    nk = K // tk
    assert M % tm == 0 and N % tn == 0 and K % tk == 0
    assert not emit_bf16 or M == tm
    if epilogue is None:
        epilogue = lambda acc: acc
    w_block = (tn, tk) if w_t else (tk, tn)
    if col_off % tn == 0:
        joff = col_off // tn
        w_index = (lambda j, k: (joff + j, k)) if w_t else (lambda j, k: (k, joff + j))
    else:
        assert w_t and col_off % (2 * SUBLANES) == 0 and layer is not None
        w_block = None
    if w_block is None:
        w_spec = pl.BlockSpec((pl.Element(1), pl.Element(tn), pl.Element(tk)),
                              lambda i, j, k: (layer, pl.multiple_of(col_off + j * tn, 2 * SUBLANES),
                                               pl.multiple_of(k * tk, LANES)))
    elif layer is None:
        w_spec = pl.BlockSpec(w_block, lambda i, j, k: w_index(j, k))
    else:
        w_spec = pl.BlockSpec((None,) + w_block, lambda i, j, k: (layer,) + w_index(j, k))
    in_specs = [pl.BlockSpec((tm, tk), lambda i, j, k: (i, k)), w_spec]
    in_specs += [pl.BlockSpec((tm, tn), lambda i, j, k: (i, j)) for _ in tile_extras]
    in_specs += [pl.BlockSpec((1, tn), lambda i, j, k: (0, j)) for _ in row_extras]
    scratch = [pltpu.VMEM((tm, tn), F32)] if nk > 1 else []
    out_specs = [pl.BlockSpec((tm, tn), lambda i, j, k: (i, j))]
    out_shape = [jax.ShapeDtypeStruct((M, N), out_dtype)]
    if emit_bf16:
        out_specs.append(pl.BlockSpec((tn, tk), lambda i, j, k: (j, k)) if w_t
                         else pl.BlockSpec((tk, tn), lambda i, j, k: (k, j)))
        out_shape.append(jax.ShapeDtypeStruct((N, K) if w_t else (K, N), BF16))
    res = pl.pallas_call(
        functools.partial(_mm_kernel, nk=nk, n_tile=len(tile_extras), n_row=len(row_extras),
                          epilogue=epilogue, w_t=w_t, emit_bf16=emit_bf16),
        grid=(M // tm, N // tn, nk),
        in_specs=in_specs,
        out_specs=out_specs,
        out_shape=out_shape,
        scratch_shapes=scratch,
        compiler_params=_cparams(("parallel", "parallel", "arbitrary")),
        name=name,
    )(x, w, *tile_extras, *row_extras)
    return res if emit_bf16 else res[0]


def _mm_deep_kernel(x_ref, w_ref, r_ref, o_ref, acc_ref, *, nk):
    k = pl.program_id(1)
    j = pl.program_id(2)
    part = lambda: jnp.dot(x_ref[...], w_ref[...], preferred_element_type=F32)

    @pl.when(k == 0)
    def _():
        acc_ref[j] = part()

    @pl.when((k > 0) & (k < nk - 1))
    def _():
        acc_ref[j] += part()

    @pl.when(k == nk - 1)
    def _():
        o_ref[...] = r_ref[...] + (acc_ref[j] + part())


def matmul_deep_residual(x, w, res):
    M, K = x.shape
    N = w.shape[1]
    tm = _pick(M, (1024, 512, 256, 32))
    tn = _pick(N, (512, 256, 128))
    tk = _pick(K, (4096, 2048, 1024, 512))
    nk = K // tk
    assert w.shape[0] == K and M % tm == 0 and N % tn == 0 and nk >= 2
    last = lambda k, j: jnp.where(k == nk - 1, j, 0)
    return pl.pallas_call(
        functools.partial(_mm_deep_kernel, nk=nk),
        grid=(M // tm, nk, N // tn),
        in_specs=[pl.BlockSpec((tm, tk), lambda i, k, j: (i, k)),
                  pl.BlockSpec((tk, tn), lambda i, k, j: (k, j)),
                  pl.BlockSpec((tm, tn), lambda i, k, j: (i, last(k, j)))],
        out_specs=pl.BlockSpec((tm, tn), lambda i, k, j: (i, last(k, j))),
        out_shape=jax.ShapeDtypeStruct((M, N), F32),
        scratch_shapes=[pltpu.VMEM((N // tn, tm, tn), F32)],
        compiler_params=_cparams(("parallel", "arbitrary", "arbitrary")),
        name="mlp_down",
    )(x, w, res)


def _headnorm_epilogue(acc, nw, flag):
    outs = []
    for c in range(acc.shape[1] // HEAD_DIM):
        sl = slice(c * HEAD_DIM, (c + 1) * HEAD_DIM)
        z = acc[:, sl]
        zn = z * lax.rsqrt(jnp.mean(z * z, axis=-1, keepdims=True) + EPS) * nw[:, sl]
        outs.append(jnp.where(flag[:, sl] != 0.0, zn, z))
    return jnp.concatenate(outs, axis=1)


def _residual_epilogue(acc, res):
    return res + acc


def _relu2_epilogue(acc):
    r = jnp.maximum(acc, 0.0)
    return r * r


def _merge_kernel(ya_ref, yb_ref, wa_ref, wb_ref, g0_ref, g1_ref, o_ref, *wq_refs):
    wa = wa_ref[...].astype(BF16)
    wb = wb_ref[...].astype(BF16)
    if wq_refs:
        wq_refs[0][...] = wa
        wq_refs[1][...] = wb
    pa = jnp.dot(ya_ref[...], wa, preferred_element_type=F32)
    pb = jnp.dot(yb_ref[...], wb, preferred_element_type=F32)
    mix = jax.nn.sigmoid(g0_ref[...]) * pa + jax.nn.sigmoid(g1_ref[...]) * pb
    o_ref[...] = mix.astype(o_ref.dtype)


def gated_merge(ya, yb, wa, wb, layer, m_gate, gate_off=0, emit_bf16=False):
    M, Ka = ya.shape
    Kb = yb.shape[1]
    D = wa.shape[-1]
    tm = _pick(M, (1024, 512, 256, 32))
    tn = _pick(D, (512, 256, 128))
    nj = D // tn
    assert gate_off % tn == 0 and (not emit_bf16 or M == tm)
    goff = gate_off // tn
    if layer is None:
        w_spec = lambda k: pl.BlockSpec((k, tn), lambda i, j: (0, j))
    else:
        w_spec = lambda k: pl.BlockSpec((None, k, tn), lambda i, j: (layer, 0, j))
    out_specs = [pl.BlockSpec((tm, tn), lambda i, j: (i, j))]
    out_shape = [jax.ShapeDtypeStruct((M, D), BF16)]
    if emit_bf16:
        out_specs += [pl.BlockSpec((k, tn), lambda i, j: (0, j)) for k in (Ka, Kb)]
        out_shape += [jax.ShapeDtypeStruct((k, D), BF16) for k in (Ka, Kb)]
    res = pl.pallas_call(
        _merge_kernel,
        grid=(M // tm, nj),
        in_specs=[pl.BlockSpec((tm, Ka), lambda i, j: (i, 0)),
                  pl.BlockSpec((tm, Kb), lambda i, j: (i, 0)),
                  w_spec(Ka), w_spec(Kb),
                  pl.BlockSpec((tm, tn), lambda i, j: (i, goff + j)),
                  pl.BlockSpec((tm, tn), lambda i, j: (i, goff + nj + j))],
        out_specs=out_specs,
        out_shape=out_shape,
        compiler_params=_cparams(("parallel", "parallel")),
        name="gated_merge",
    )(ya, yb, wa, wb, m_gate, m_gate)
    return res if emit_bf16 else res[0]


def _silu(x):
    return x * jax.nn.sigmoid(x)


def _pair_total(cm, m, t_io):
    C = cm.shape[0]
    if m == 1:
        return jnp.where((t_io & 1) != 0, pltpu.roll(cm, 1, 0), cm)
    if m == 2:
        j = t_io & 3
        return jnp.where(j == 0, pltpu.roll(cm, C - 1, 0),
                         jnp.where(j == 1, cm,
                                   jnp.where(j == 2, pltpu.roll(cm, 1, 0), pltpu.roll(cm, 2, 0))))
    x3 = cm.reshape(C // (2 * m), 2 * m, cm.shape[1])
    return jnp.broadcast_to(x3[:, m - 1:m, :], x3.shape).reshape(cm.shape)


def _hgrn_chunk(aq, af, ai, ag, lb, nw, st, n_valid):
    C = aq.shape[0]
    t_io = lax.broadcasted_iota(jnp.int32, (C, LANES), 0)
    row_io = lax.broadcasted_iota(jnp.int32, (C, C), 0)
    col_io = lax.broadcasted_iota(jnp.int32, (C, C), 1)
    split = jnp.where(row_io > col_io, row_io ^ col_io, 0)
    for sh in (1, 2, 4, 8, 16):
        split = split | (split >> sh)
    split = split - (split >> 1)
    f = lb + (1.0 - lb) * jax.nn.sigmoid(af)
    if n_valid < C:
        f = jnp.where(t_io < n_valid, f, 1.0)
    g = jnp.log(f)
    kk = 1.0 - f
    qq = _silu(aq)
    v = ai.astype(BF16)
    a = jnp.zeros((C, C), F32)
    cm = g
    m = 1
    while m < C:
        tot = _pair_total(cm, m, t_io)
        odd = (t_io & m) != 0
        z = (jnp.where(odd, qq, kk) * jnp.exp(jnp.where(odd, cm, tot - cm))).astype(BF16)
        p = lax.dot_general(z, z, (((1,), (1,)), ((), ())), preferred_element_type=F32)
        a = a + jnp.where(split == m, p, 0.0)
        cm = cm + jnp.where(odd, tot, 0.0)
        m *= 2
    b = cm
    d = jnp.sum(qq * kk, axis=-1, keepdims=True)
    a = jnp.where(row_io == col_io, d, a)
    o_intra = jnp.dot(a.astype(BF16), v, preferred_element_type=F32)
    qe = (qq * jnp.exp(b)).astype(BF16)
    o_inter = lax.dot_general(qe, st.astype(BF16), (((1,), (1,)), ((), ())), preferred_element_type=F32)
    b_end = b[C - 1:C, :]
    ku = (kk * jnp.exp(b_end - b)).astype(BF16)
    ut = lax.dot_general(v, ku, (((0,), (0,)), ((), ())), preferred_element_type=F32)
    st_new = st * jnp.exp(b_end) + ut
    o = o_inter + o_intra
    on = o * lax.rsqrt(jnp.mean(o * o, axis=-1, keepdims=True) + EPS) * nw
    return on * _silu(ag), st_new


def _hgrn_kernel(*refs, n_sub, chunk, n_valid, has_s0, hps):
    aq_ref, af_ref, ai_ref, ag_ref, lb_ref, nw_ref = refs[:6]
    s0_ref = refs[6] if has_s0 else None
    y_ref, s_ref, st_ref = refs[6 + has_s0:]
    c = pl.program_id(2)
    nw = nw_ref[...]
    for hh in range(hps):
        cols = slice(hh * LANES, (hh + 1) * LANES)

        @pl.when(c == 0)
        def _():
            st_ref[hh] = s0_ref[hh].T if has_s0 else jnp.zeros((LANES, LANES), F32)

        lb = lb_ref[:, cols]
        for j in range(n_sub):
            rows = pl.ds(j * chunk, chunk)
            y, st_new = _hgrn_chunk(aq_ref[rows, cols], af_ref[rows, cols], ai_ref[rows, cols],
                                    ag_ref[rows, cols], lb, nw, st_ref[hh], n_valid)
            st_ref[hh] = st_new
            y_ref[rows, cols] = y.astype(y_ref.dtype)

        @pl.when(c == pl.num_programs(2) - 1)
        def _():
            s_ref[hh] = st_ref[hh].T


def hgrn(za, lb, nw, s0, B, L, H, chunk, n_valid=None):
    dk = LANES
    assert L % chunk == 0 and (n_valid is None or L == chunk)
    n_sub = _pick(L // chunk, (4, 2, 1))
    tc = n_sub * chunk
    nc = L // tc
    hps = _pick(H, tuple(h for h in (16, 8, 4, 2, 1) if h * tc <= 2048))
    assert H % hps == 0
    hb = H // hps
    row = lambda k: pl.BlockSpec((tc, hps * dk), lambda b, h, c: (b * nc + c, k * hb + h))
    state = pl.BlockSpec((None, hps, dk, dk), lambda b, h, c: (b, h, 0, 0))
    has_s0 = s0 is not None
    return pl.pallas_call(
        functools.partial(_hgrn_kernel, n_sub=n_sub, chunk=chunk,
                          n_valid=chunk if n_valid is None else n_valid, has_s0=has_s0, hps=hps),
        grid=(B, hb, nc),
        in_specs=[row(0), row(1), row(2), row(3),
                  pl.BlockSpec((1, hps * dk), lambda b, h, c: (0, h)),
                  pl.BlockSpec((1, dk), lambda b, h, c: (0, 0))] + ([state] if has_s0 else []),
        out_specs=[pl.BlockSpec((tc, hps * dk), lambda b, h, c: (b * nc + c, h)), state],
        out_shape=[jax.ShapeDtypeStruct((B * L, H * dk), BF16),
                   jax.ShapeDtypeStruct((B, H, dk, dk), F32)],
        scratch_shapes=[pltpu.VMEM((hps, dk, dk), F32)],
        compiler_params=_cparams(("parallel", "parallel", "arbitrary")),
        name="hgrn",
    )(za, za, za, za, lb.reshape(1, -1), nw.reshape(1, dk), *([s0] if has_s0 else []))


def hgrn_prompt(za, lb, nw, B, L, H):
    return hgrn(za, lb, nw, None, B, L, H, HGRN_CHUNK)


def _pool_rows(x, wl, nw):
    wl = wl - jnp.max(wl, axis=0, keepdims=True)
    e = jnp.exp(wl)
    w = e / jnp.sum(e, axis=0, keepdims=True)
    R = x.shape[0] // BLOCK
    pooled = jnp.sum(x.reshape(R, BLOCK, x.shape[1]) * w[None], axis=1)
    half = x.shape[1] // 2
    outs = []
    for c in range(x.shape[1] // HEAD_DIM):
        z = pooled[:, c * HEAD_DIM:(c + 1) * HEAD_DIM]
        if c * HEAD_DIM < half:
            z = z * lax.rsqrt(jnp.mean(z * z, axis=-1, keepdims=True) + EPS) * nw
        outs.append(z)
    return jnp.concatenate(outs, axis=1)


def _pool_kernel(x_ref, wl_ref, nw_ref, o_ref):
    o_ref[...] = _pool_rows(x_ref[...], wl_ref[...], nw_ref[...])


def pool_prompt(kv, pos_logits, kn_w):
    M = kv.shape[0]
    kvw2 = kv.shape[1] // 2
    rb = SUBLANES * BLOCK
    assert M % rb == 0
    return pl.pallas_call(
        _pool_kernel,
        grid=(M // rb,),
        in_specs=[pl.BlockSpec((rb, kvw2), lambda i: (i, 0)),
                  pl.BlockSpec((BLOCK, 1), lambda i: (0, 0)),
                  pl.BlockSpec((1, HEAD_DIM), lambda i: (0, 0))],
        out_specs=pl.BlockSpec((SUBLANES, kvw2), lambda i: (i, 0)),
        out_shape=jax.ShapeDtypeStruct((M // BLOCK, kvw2), F32),
        compiler_params=_cparams(("parallel",)),
        name="pool_prompt",
    )(kv, pos_logits.reshape(BLOCK, 1), kn_w.reshape(1, HEAD_DIM))


NSA_TQ = 256
NSA_TK = 256


def _select_blocks(imp, qpos, n_cand, score_ref):
    cand = lax.broadcasted_iota(jnp.int32, imp.shape, 0)
    cur = qpos // BLOCK
    valid = cand <= cur
    forced = (cand == 0) | (cand > cur - N_LOCAL)
    score = jnp.where(valid, imp + jnp.where(forced, FORCE_BONUS, 0.0), -jnp.inf)
    score_ref[...] = score

    def body(m, rank):
        row = score_ref[pl.ds(m, 1), :]
        beats = (row > score) | ((row == score) & (cand > m))
        return rank + jnp.where(beats, 1.0, 0.0)

    rank = lax.fori_loop(0, n_cand, body, jnp.zeros(imp.shape, F32), unroll=n_cand <= 32)
    return valid & (rank < float(N_SEL))


class _Flash:
    def __init__(self, m_ref, l_ref, acc_ref):
        self.m_ref, self.l_ref, self.acc_ref = m_ref, l_ref, acc_ref

    def init(self):
        self.m_ref[...] = jnp.full(self.m_ref.shape, MASK_VALUE, F32)
        self.l_ref[...] = jnp.zeros(self.l_ref.shape, F32)
        self.acc_ref[...] = jnp.zeros(self.acc_ref.shape, F32)

    def step(self, k, v, qt_b, *, scale=None, bias=None, mask=None):
        s = jnp.dot(k.astype(BF16), qt_b, preferred_element_type=F32)
        if scale is not None:
            s = s * scale
        if bias is not None:
            s = s + bias
        if mask is not None:
            s = jnp.where(mask, s, MASK_VALUE)
        m_old = self.m_ref[...]
        m_new = jnp.maximum(m_old, jnp.max(s, axis=0, keepdims=True))
        alpha = jnp.exp2(m_old - m_new)
        p = jnp.exp2(s - m_new)
        self.l_ref[...] = alpha * self.l_ref[...] + jnp.sum(p, axis=0, keepdims=True)
        pv = lax.dot_general(v.astype(BF16), p.astype(BF16), (((0,), (0,)), ((), ())),
                             preferred_element_type=F32)
        self.acc_ref[...] = alpha * self.acc_ref[...] + pv
        self.m_ref[...] = m_new

    def result(self):
        return self.acc_ref[...] * (1.0 / self.l_ref[...])


def _nsa_prompt_kernel(q_ref, ks_ref, vs_ref, kw_ref, vw_ref, kc_ref, vc_ref, g_ref, o_ref,
                       sel_ref, score_ref, ms_ref, ls_ref, accs_ref, mw_ref, lw_ref, accw_ref, *, nb):
    tq = NSA_TQ
    qt = pl.program_id(2)
    nh = NSA_GROUP
    q = q_ref[...] * SCALE
    qT = jnp.concatenate([q[:, h * HEAD_DIM:(h + 1) * HEAD_DIM].T for h in range(nh)], axis=1)
    qt_b = (qT * LOG2E).astype(BF16)
    lane = lax.broadcasted_iota(jnp.int32, (1, nh * tq), 1)
    qpos = qt * tq + (lane & (tq - 1))

    s = jnp.dot(kc_ref[...], qT, preferred_element_type=F32, precision=lax.Precision.HIGHEST)
    n_io = lax.broadcasted_iota(jnp.int32, s.shape, 0)
    ready = ((n_io + 1) * BLOCK - 1) <= qpos
    s = jnp.where(ready, s, MASK_VALUE)
    e = jnp.exp(s - jnp.max(s, axis=0, keepdims=True))
    p = e / jnp.sum(e, axis=0, keepdims=True) * jnp.where(ready, 1.0, 0.0)
    o_cmp = lax.dot_general(vc_ref[...].astype(BF16), p.astype(BF16), (((0,), (0,)), ((), ())),
                            preferred_element_type=F32)
    imp = p[:, 0:tq]
    for h in range(1, nh):
        imp = imp + p[:, h * tq:(h + 1) * tq]
    sel = _select_blocks(imp, qpos[:, 0:tq], nb, score_ref)
    sel_bias = jnp.where(sel, 0.0, MASK_VALUE)
    tk = NSA_TK
    bpt = tk // BLOCK
    for n in range(nb):
        sel_ref[n // bpt, n % bpt:n % bpt + 1, :] = sel_bias[n:n + 1, :]

    krow = lax.broadcasted_iota(jnp.int32, (tk, nh * tq), 0)
    sel_acc = _Flash(ms_ref, ls_ref, accs_ref)
    win_acc = _Flash(mw_ref, lw_ref, accw_ref)
    sel_acc.init()
    win_acc.init()

    def tile_bias(kt):
        blk = sel_ref[kt]
        rows = [jnp.broadcast_to(jnp.concatenate([blk[j:j + 1, :]] * nh, axis=1), (BLOCK, nh * tq))
                for j in range(bpt)]
        return jnp.concatenate(rows, axis=0)

    def sel_step(kt, mask=None):
        rows = pl.ds(pl.multiple_of(kt * tk, tk), tk)
        sel_acc.step(ks_ref[rows, :], vs_ref[rows, :], qt_b, bias=tile_bias(kt), mask=mask)

    def win_step(kt, mask):
        rows = pl.ds(pl.multiple_of(kt * tk, tk), tk)
        win_acc.step(kw_ref[rows, :], vw_ref[rows, :], qt_b, mask=mask)

    kd = qt * (tq // tk)
    w_lo = jnp.maximum(kd - WINDOW // tk, 0)

    def far_body(kt, carry):
        sel_step(kt)
        return carry

    lax.fori_loop(0, w_lo, far_body, 0)

    def near_body(kt, carry):
        sel_step(kt)
        win_step(kt, (qpos - (kt * tk + krow)) < WINDOW)
        return carry

    lax.fori_loop(w_lo, kd, near_body, 0)

    for d in range(tq // tk):
        causal = ((kd + d) * tk + krow) <= qpos
        sel_step(kd + d, causal)
        win_step(kd + d, causal)
    o_sel = sel_acc.result()
    o_win = win_acc.result()

    gT = jax.nn.sigmoid(g_ref[...]).T
    for h in range(nh):
        sl = slice(h * tq, (h + 1) * tq)
        y = (gT[3 * h:3 * h + 1, :] * o_cmp[:, sl] + gT[3 * h + 1:3 * h + 2, :] * o_sel[:, sl]
             + gT[3 * h + 2:3 * h + 3, :] * o_win[:, sl])
        o_ref[:, h * HEAD_DIM:(h + 1) * HEAD_DIM] = y.T.astype(o_ref.dtype)


def nsa_prompt(q, kv, win, pooled, gates, B, L):
    tq = NSA_TQ
    G, nh, hd = NSA_KV, NSA_GROUP, HEAD_DIM
    tk = NSA_TK
    assert L % tq == 0 and tq % tk == 0 and tk % BLOCK == 0 and WINDOW % tk == 0
    nq = L // tq
    nb = L // BLOCK
    full = lambda off: pl.BlockSpec((L, hd), lambda b, g, t: (b, off + g))
    return pl.pallas_call(
        functools.partial(_nsa_prompt_kernel, nb=nb),
        grid=(B, G, nq),
        in_specs=[pl.BlockSpec((tq, nh * hd), lambda b, g, t: (b * nq + t, g)),
                  full(2 * G), full(3 * G),
                  pl.BlockSpec((L, hd), lambda b, g, t: (b, g)),
                  pl.BlockSpec((L, hd), lambda b, g, t: (b, G + g)),
                  pl.BlockSpec((nb, hd), lambda b, g, t: (b, g)),
                  pl.BlockSpec((nb, hd), lambda b, g, t: (b, G + g)),
                  pl.BlockSpec((tq, LANES), lambda b, g, t: (b * nq + t, g))],
        out_specs=pl.BlockSpec((tq, nh * hd), lambda b, g, t: (b * nq + t, g)),
        out_shape=jax.ShapeDtypeStruct((B * L, G * nh * hd), BF16),
        scratch_shapes=[pltpu.VMEM((nb * BLOCK // tk, tk // BLOCK, tq), F32),
                        pltpu.VMEM((nb, tq), F32)]
        + 2 * [pltpu.VMEM((1, nh * tq), F32), pltpu.VMEM((1, nh * tq), F32), pltpu.VMEM((hd, nh * tq), F32)],
        compiler_params=_cparams(("parallel", "parallel", "arbitrary")),
        name="nsa_prompt",
    )(q, kv, kv, win, win, pooled, pooled, gates)


DEC_PAGES_PER_STEP = 32
DEC_ROWS = 8


def _dec_qpos(past_len, n_new, shape):
    lane = lax.broadcasted_iota(jnp.int32, shape, len(shape) - 1)
    return past_len + lax.rem(lane & (LANES - 1), n_new)


def _page_spec(cache, layer, half, i, npg):
    return pl.BlockSpec((None, None, cache.shape[2], None) + cache.shape[4:],
                        lambda b, s, pt: (layer, pt[b, s * npg + i], 0, half, 0, 0))


def _page_rows(page_ref, r):
    n, rows, hd = page_ref.shape
    return page_ref.reshape(n * rows, hd)[pl.ds(r, n, stride=rows), :]


def _pool_paged_kernel(pt_ref, *refs):
    del pt_ref
    npg = DEC_PAGES_PER_STEP
    wl_ref, nw_ref, o_ref = refs[npg:npg + 3]
    wl = wl_ref[...]
    e = jnp.exp(wl - jnp.max(wl, axis=0, keepdims=True))
    w = e / jnp.sum(e, axis=0, keepdims=True)
    bpp = refs[0].shape[0] // BLOCK
    for i in range(npg):
        for r in range(2 * NSA_KV):
            x = _page_rows(refs[i], r)
            z = jnp.sum(x.reshape(bpp, BLOCK, HEAD_DIM) * w[None], axis=1)
            if r < NSA_KV:
                z = z * lax.rsqrt(jnp.mean(z * z, axis=-1, keepdims=True) + EPS) * nw_ref[...]
            o_ref[r, i * bpp:(i + 1) * bpp, :] = z


def pool_paged(cache, layer, page_table, pos_logits, kn_w):
    B, n_pages = page_table.shape
    page = cache.shape[2]
    npg = DEC_PAGES_PER_STEP
    assert n_pages % npg == 0 and page % BLOCK == 0
    bpp = page // BLOCK
    assert (npg * bpp) % SUBLANES == 0
    return pl.pallas_call(
        _pool_paged_kernel,
        grid_spec=pltpu.PrefetchScalarGridSpec(
            num_scalar_prefetch=1,
            grid=(B, n_pages // npg),
            in_specs=[_page_spec(cache, layer, 0, i, npg) for i in range(npg)]
            + [pl.BlockSpec((BLOCK, 1), lambda b, s, pt: (0, 0)),
               pl.BlockSpec((1, HEAD_DIM), lambda b, s, pt: (0, 0))],
            out_specs=pl.BlockSpec((None, 2 * NSA_KV, npg * bpp, HEAD_DIM), lambda b, s, pt: (b, 0, s, 0)),
        ),
        out_shape=jax.ShapeDtypeStruct((B, 2 * NSA_KV, n_pages * bpp, HEAD_DIM), F32),
        compiler_params=_cparams(("parallel", "arbitrary")),
        name="pool_paged",
    )(page_table, *([cache] * npg), pos_logits.reshape(BLOCK, 1), kn_w.reshape(1, HEAD_DIM))


def _nsa_dec_front_kernel(qbd_ref, pooled_ref, cwin_ref, nwin_ref, gl_ref, part_ref, sel_ref,
                          score_ref, m_ref, l_ref, acc_ref, *, past_len, n_new, n_cand):
    kvw = qbd_ref.shape[0]
    nl = qbd_ref.shape[1]
    qbd = qbd_ref[...]
    qbd_b = qbd.astype(BF16)
    qpos = _dec_qpos(past_len, n_new, (1, nl))

    nf = pooled_ref.shape[1]
    kc = jnp.concatenate([pooled_ref[g] for g in range(NSA_KV)], axis=1)
    vc = jnp.concatenate([pooled_ref[NSA_KV + g] for g in range(NSA_KV)], axis=1)
    s = jnp.dot(kc, qbd, preferred_element_type=F32, precision=lax.Precision.HIGHEST) * SCALE
    n_io = lax.broadcasted_iota(jnp.int32, s.shape, 0)
    ready = ((n_io + 1) * BLOCK - 1) <= qpos
    s = jnp.where(ready, s, MASK_VALUE)
    e = jnp.exp(s - jnp.max(s, axis=0, keepdims=True))
    p = e / jnp.sum(e, axis=0, keepdims=True) * jnp.where(ready, 1.0, 0.0)
    o_cmp = lax.dot_general(vc.astype(BF16), p.astype(BF16), (((0,), (0,)), ((), ())),
                            preferred_element_type=F32)
    imp = p[:, 0:LANES]
    for h in range(1, NSA_GROUP):
        imp = imp + p[:, h * LANES:(h + 1) * LANES]
    imp = jnp.concatenate([imp, jnp.zeros((score_ref.shape[0] - nf, LANES), F32)], axis=0)
    sel = _select_blocks(imp, qpos[:, 0:LANES], n_cand, score_ref)
    sel_ref[...] = jnp.where(sel, 0.0, MASK_VALUE)

    win_acc = _Flash(m_ref, l_ref, acc_ref)
    win_acc.init()
    w_rows = cwin_ref.shape[0]
    krow = lax.broadcasted_iota(jnp.int32, (w_rows, nl), 0)
    d = qpos - (past_len - w_rows + krow)
    win_acc.step(cwin_ref[:, 0:kvw], cwin_ref[:, kvw:2 * kvw], qbd_b, scale=SCALE * LOG2E,
                 mask=(d >= 0) & (d < WINDOW))
    nrow = lax.broadcasted_iota(jnp.int32, (nwin_ref.shape[0], nl), 0)
    d = qpos - (past_len + nrow)
    win_acc.step(nwin_ref[:, 0:kvw], nwin_ref[:, kvw:2 * kvw], qbd_b, scale=SCALE * LOG2E,
                 mask=(d >= 0) & (d < WINDOW) & (nrow < n_new))
    g = jax.nn.sigmoid(gl_ref[...])
    part_ref[...] = g[0:1, :] * o_cmp + g[2:3, :] * win_acc.result()


def _nsa_dec_sel_kernel(pt_ref, *refs, past_len, n_new, page):
    del pt_ref
    npg = DEC_PAGES_PER_STEP
    page_refs = refs[:npg]
    qg_ref, sel_ref, sel_new_ref, nkv_ref, part_ref, g1_ref, y_ref, m_ref, l_ref, acc_ref = refs[npg:]
    G, hd = NSA_KV, HEAD_DIM
    step = pl.program_id(1)
    chains = [_Flash(m_ref.at[g], l_ref.at[g], acc_ref.at[g]) for g in range(G)]
    qg_b = [qg_ref[g].astype(BF16) for g in range(G)]

    @pl.when(step == 0)
    def _():
        for c in chains:
            c.init()

    for g in range(G):
        bias = jnp.concatenate([jnp.broadcast_to(sel_ref[g, i][j:j + 1, :], (BLOCK, LANES))
                                for i in range(npg) for j in range(page // BLOCK)], axis=0)
        k = jnp.concatenate([_page_rows(page_refs[i], g) for i in range(npg)], axis=0)
        v = jnp.concatenate([_page_rows(page_refs[i], G + g) for i in range(npg)], axis=0)
        chains[g].step(k, v, qg_b[g], scale=SCALE * LOG2E, bias=bias)

    @pl.when(step == pl.num_programs(1) - 1)
    def _():
        nrow = lax.broadcasted_iota(jnp.int32, (nkv_ref.shape[0], LANES), 0)
        lane = lax.broadcasted_iota(jnp.int32, (1, LANES), 1)
        qpos = past_len + lax.rem(lane, n_new)
        mask = (past_len + nrow <= qpos) & (nrow < n_new)
        kvw = G * hd
        for g in range(G):
            chains[g].step(nkv_ref[:, g * hd:(g + 1) * hd], nkv_ref[:, kvw + g * hd:kvw + (g + 1) * hd],
                           qg_b[g], scale=SCALE * LOG2E, bias=sel_new_ref[g, 0][0:1, :], mask=mask)
            y = part_ref[g] + jax.nn.sigmoid(g1_ref[g][0:1, :]) * chains[g].result()
            y_ref[g] = y.T


def nsa_decode(q, kv_new, win_new, gates, cache, layer, page_table, cache_win, pos_logits, kn_w, n_new):
    B, n_pages = page_table.shape
    page = cache.shape[2]
    G, nh, hd = NSA_KV, NSA_GROUP, HEAD_DIM
    kvw = G * hd
    nl = nh * LANES
    past_len = n_pages * page
    npg = DEC_PAGES_PER_STEP
    bpp = page // BLOCK
    n_cand = -(-(past_len + n_new) // BLOCK)
    n_rows = -(-(n_cand + bpp) // SUBLANES) * SUBLANES // bpp * bpp
    assert G * n_new <= LANES and n_new <= DEC_ROWS and n_new <= BLOCK and past_len % BLOCK == 0

    q5 = q.reshape(B, n_new, G, nh, hd)
    qt = jnp.transpose(q5, (0, 2, 4, 3, 1))
    qbd = qt[:, :, :, :, None, :] * jnp.eye(G, dtype=F32)[None, :, None, None, :, None]
    qbd = jnp.pad(qbd.reshape(B, kvw, nh, G * n_new), ((0, 0), (0, 0), (0, 0), (0, LANES - G * n_new)))
    qbd = qbd.reshape(B, kvw, nl)
    gl = gates.reshape(B, n_new, G, LANES)[..., :3 * nh].reshape(B, n_new, G, nh, 3)
    gl = jnp.transpose(gl, (0, 4, 3, 2, 1)).reshape(B, 3, nh, G * n_new)
    gl = jnp.pad(gl, ((0, 0), (0, DEC_ROWS - 3), (0, 0), (0, LANES - G * n_new))).reshape(B, DEC_ROWS, nl)
    pad_rows = lambda t: jnp.pad(t.reshape(B, n_new, -1), ((0, 0), (0, DEC_ROWS - n_new), (0, 0)))
    nkv = pad_rows(kv_new)
    nwin = pad_rows(win_new)

    pooled = pool_paged(cache, layer, page_table, pos_logits, kn_w)
    nf = pooled.shape[2]
    per_b = lambda *shape: pl.BlockSpec((None,) + shape, lambda b: (b,) + (0,) * len(shape))
    part, sel = pl.pallas_call(
        functools.partial(_nsa_dec_front_kernel, past_len=past_len, n_new=n_new, n_cand=n_cand),
        grid=(B,),
        in_specs=[per_b(kvw, nl), per_b(2 * G, nf, hd), per_b(cache_win.shape[1], 2 * kvw),
                  per_b(DEC_ROWS, 2 * kvw), per_b(DEC_ROWS, nl)],
        out_specs=[per_b(kvw, nl), per_b(n_rows, LANES)],
        out_shape=[jax.ShapeDtypeStruct((B, kvw, nl), F32), jax.ShapeDtypeStruct((B, n_rows, LANES), F32)],
        scratch_shapes=[pltpu.VMEM((n_rows, LANES), F32), pltpu.VMEM((1, nl), F32),
                        pltpu.VMEM((1, nl), F32), pltpu.VMEM((kvw, nl), F32)],
        compiler_params=_cparams(("parallel",)),
        name="nsa_dec_front",
    )(qbd, pooled, cache_win, nwin, gl)

    nu = nh * n_new
    lane_pad = lambda t: jnp.pad(t, [(0, 0)] * (t.ndim - 1) + [(0, LANES - nu)])
    qg = lane_pad(qt.reshape(B, G, hd, nu))
    p6 = part.reshape(B, G, hd, nh, LANES)[..., :G * n_new].reshape(B, G, hd, nh, G, n_new)
    partg = lane_pad(jnp.stack([p6[:, g, :, :, g, :] for g in range(G)], axis=1).reshape(B, G, hd, nu))
    selg = jnp.transpose(sel[:, :, :G * n_new].reshape(B, n_rows, G, n_new), (0, 2, 1, 3))
    selg = lane_pad(jnp.tile(selg, (1, 1, 1, nh))).reshape(B, G, n_rows // bpp, bpp, LANES)
    g1 = gates.reshape(B, n_new, G, LANES)[..., :3 * nh].reshape(B, n_new, G, nh, 3)[..., 1]
    g1g = lane_pad(jnp.transpose(g1, (0, 2, 3, 1)).reshape(B, G, 1, nu))
    g1g = jnp.pad(g1g, ((0, 0), (0, 0), (0, SUBLANES - 1), (0, 0)))

    page_spec = lambda i: _page_spec(cache, layer, 1, i, npg)
    per_bs = lambda *shape: pl.BlockSpec((None,) + shape, lambda b, s, pt: (b,) + (0,) * len(shape))
    y = pl.pallas_call(
        functools.partial(_nsa_dec_sel_kernel, past_len=past_len, n_new=n_new, page=page),
        grid_spec=pltpu.PrefetchScalarGridSpec(
            num_scalar_prefetch=1,
            grid=(B, n_pages // npg),
            in_specs=[page_spec(i) for i in range(npg)]
            + [per_bs(G, hd, LANES),
               pl.BlockSpec((None, G, npg, bpp, LANES), lambda b, s, pt: (b, 0, s, 0, 0)),
               pl.BlockSpec((None, G, 1, bpp, LANES), lambda b, s, pt: (b, 0, n_pages, 0, 0)),
               pl.BlockSpec((None, DEC_ROWS, 2 * kvw), lambda b, s, pt: (b, 0, 1)),
               per_bs(G, hd, LANES),
               per_bs(G, SUBLANES, LANES)],
            out_specs=per_bs(G, LANES, hd),
            scratch_shapes=[pltpu.VMEM((G, 1, LANES), F32), pltpu.VMEM((G, 1, LANES), F32),
                            pltpu.VMEM((G, hd, LANES), F32)],
        ),
        out_shape=jax.ShapeDtypeStruct((B, G, LANES, hd), F32),
        compiler_params=_cparams(("parallel", "arbitrary")),
        name="nsa_dec_sel",
    )(page_table, *([cache] * npg), qg, selg, selg, nkv, partg, g1g)
    y = jnp.transpose(y[:, :, :nu].reshape(B, G, nh, n_new, hd), (0, 3, 1, 2, 4))
    return y.reshape(B * n_new, G * nh * hd).astype(BF16)


def _lower_bounds_kernel(x_ref, o_ref):
    x = x_ref[...]
    e = jnp.exp(x - jnp.max(x, axis=0, keepdims=True))
    p = e / jnp.sum(e, axis=0, keepdims=True)
    c = p[0:1, :]
    o_ref[0:1, :] = jnp.zeros_like(c)
    for i in range(1, x.shape[0]):
        c = c + p[i:i + 1, :]
        o_ref[i:i + 1, :] = c - p[0:1, :]


def hgrn_lower_bounds(lb_logits):
    return pl.pallas_call(
        _lower_bounds_kernel,
        out_shape=jax.ShapeDtypeStruct(lb_logits.shape, F32),
        name="hgrn_lower_bounds",
    )(lb_logits)


GATE_W = NSA_KV * LANES


def _gate_weights(w_t, o_bg):
    n_bg = 3 * NSA_KV * NSA_GROUP
    depth, _, K = w_t.shape
    w_bg = w_t[:, o_bg:o_bg + n_bg].reshape(depth, NSA_KV, 3 * NSA_GROUP, K)
    return jnp.pad(w_bg, ((0, 0), (0, 0), (0, LANES - 3 * NSA_GROUP), (0, 0))).reshape(depth, GATE_W, K)


def _project_in(x, w_t, w_bg, wq, layer, norm1_w, q_norm_w, k_norm_w, hq, nq, kvw):
    h = rmsnorm_cast(x, norm1_w)
    o_q = 4 * hq
    o_kv = o_q + nq
    o_win = o_kv + 4 * kvw
    o_mg = o_win + 2 * kvw + 3 * NSA_KV * NSA_GROUP
    pieces = dict(hgrn=(0, o_q), q=(o_q, nq), kv=(o_kv, 4 * kvw), win=(o_win, 2 * kvw),
                  mg=(o_mg, w_t.shape[1] - o_mg))
    new = {}

    def proj(name, **kw):
        off, n = pieces[name]
        if wq is not None:
            return matmul(h, wq[name], out_dtype=F32, w_t=True, name="in_" + name, **kw)
        out, new[name] = matmul(h, w_t, layer=layer, col_off=off, n=n, out_dtype=F32, w_t=True, emit_bf16=True,
                                name="in_" + name, **kw)
        return out

    tile = lambda v, n: jnp.tile(v, n // HEAD_DIM).reshape(1, n)
    ones = lambda n: jnp.ones((1, n), F32)
    zeros = lambda n: jnp.zeros((1, n), F32)
    za = proj("hgrn")
    q = proj("q", epilogue=_headnorm_epilogue, row_extras=(tile(q_norm_w, nq), ones(nq)))
    kv_flag = jnp.concatenate([zeros(2 * kvw), ones(kvw), zeros(kvw)], axis=1)
    kv = proj("kv", epilogue=_headnorm_epilogue, row_extras=(tile(k_norm_w[1], 4 * kvw), kv_flag))
    win_flag = jnp.concatenate([ones(kvw), zeros(kvw)], axis=1)
    win = proj("win", epilogue=_headnorm_epilogue, row_extras=(tile(k_norm_w[2], 2 * kvw), win_flag))
    mg = proj("mg")
    if wq is not None:
        bg = matmul(h, wq["bg"], out_dtype=F32, w_t=True, name="in_bg")
    else:
        bg, new["bg"] = matmul(h, w_bg, layer=layer, out_dtype=F32, w_t=True, emit_bf16=True, name="in_bg")
    return za, q, kv, win, bg, mg, (wq if wq is not None else new)


def _finish_layer(x, ya, yb, mg, w_f32, wq, layer, norm2_w):
    if wq is not None:
        mix = gated_merge(ya, yb, wq["a"], wq["b"], None, mg)
        x1 = matmul(mix, wq["out"], out_dtype=F32, epilogue=_residual_epilogue, tile_extras=(x,),
                    name="out_proj")
        h2 = rmsnorm_cast(x1, norm2_w)
        u = matmul(h2, wq["up"], out_dtype=BF16, epilogue=_relu2_epilogue, name="mlp_up")
        return matmul_deep_residual(u, wq["down"], x1), wq
    new = {}
    mix, new["a"], new["b"] = gated_merge(ya, yb, w_f32["a"], w_f32["b"], layer, mg, emit_bf16=True)
    cast_mm = functools.partial(matmul, layer=layer, emit_bf16=True)
    x1, new["out"] = cast_mm(mix, w_f32["out"], out_dtype=F32, epilogue=_residual_epilogue, tile_extras=(x,),
                             name="out_proj")
    h2 = rmsnorm_cast(x1, norm2_w)
    u, new["up"] = cast_mm(h2, w_f32["up"], out_dtype=BF16, epilogue=_relu2_epilogue, name="mlp_up")
    x2, new["down"] = cast_mm(u, w_f32["down"], out_dtype=F32, epilogue=_residual_epilogue, tile_extras=(x1,),
                              name="mlp_down")
    return x2, new


def kernel(x_prompt, x_sample, cache_kv, cache_win, state_hgrn, page_table, norm1_w, w_in, hgrn_lb_logits,
           hgrn_norm_w, q_norm_w, k_norm_w, cmp_pos_logits, w_branch_a, w_branch_b, w_out, norm2_w, w_up,
           w_down):
    depth = w_in.shape[0]
    B, L, D = x_prompt.shape
    Bs, Ls, _ = x_sample.shape
    H = state_hgrn.shape[2]
    hq = H * state_hgrn.shape[3]
    G, hd = cache_kv.shape[4], cache_kv.shape[5]
    kvw = G * hd
    nq = w_branch_b.shape[1]
    assert (G, hd) == (NSA_KV, HEAD_DIM) and nq == NSA_KV * NSA_GROUP * HEAD_DIM
    assert state_hgrn.shape[3] == LANES and state_hgrn.shape[4] == LANES
    lbs = hgrn_lower_bounds(hgrn_lb_logits)
    cache = cache_kv.reshape(cache_kv.shape[:3] + (2, 2 * G, hd))
    dec_chunk = 2 * SUBLANES
    assert Ls <= dec_chunk

    xp = x_prompt.reshape(B * L, D)
    xs = x_sample.reshape(Bs * Ls, D)
    kv_p, kv_s, win_p, win_s, st_p, st_s = [], [], [], [], [], []
    w_t = jnp.swapaxes(w_in, 1, 2)
    w_bg = _gate_weights(w_t, 4 * hq + nq + 6 * kvw)
    w_f32 = dict(a=w_branch_a, b=w_branch_b, out=w_out, up=w_up, down=w_down)
    for l in range(depth):
        norms = (norm1_w[l], q_norm_w[l], k_norm_w[l], hq, nq, kvw)
        za, q, kv, win, bg, mg, wq_in = _project_in(xs, w_t, w_bg, None, l, *norms)
        za_pad = jnp.pad(za.reshape(Bs, Ls, -1), ((0, 0), (0, dec_chunk - Ls), (0, 0)))
        ya, st = hgrn(za_pad.reshape(Bs * dec_chunk, -1), lbs[l], hgrn_norm_w[l], state_hgrn[l],
                      Bs, dec_chunk, H, dec_chunk, n_valid=Ls)
        ya = ya.reshape(Bs, dec_chunk, -1)[:, :Ls].reshape(Bs * Ls, -1)
        cwin = cache_win[l].reshape(Bs, cache_win.shape[2], 2 * kvw)
        yb = nsa_decode(q, kv, win, bg, cache, l, page_table, cwin, cmp_pos_logits[l],
                        k_norm_w[l, 0], Ls)
        xs, wq_out = _finish_layer(xs, ya, yb, mg, w_f32, None, l, norm2_w[l])
        kv_s.append(kv.reshape(Bs, Ls, 4, G, hd))
        win_all = jnp.concatenate([cwin, win.reshape(Bs, Ls, 2 * kvw)], axis=1)
        ws = min(WINDOW, win_all.shape[1])
        win_s.append(win_all[:, win_all.shape[1] - ws:].reshape(Bs, ws, 2, G, hd))
        st_s.append(st)
        za, q, kv, win, bg, mg, _ = _project_in(xp, w_t, w_bg, wq_in, l, *norms)
        ya, st = hgrn(za, lbs[l], hgrn_norm_w[l], None, B, L, H, HGRN_CHUNK)
        pooled = pool_prompt(kv, cmp_pos_logits[l], k_norm_w[l, 0])
        yb = nsa_prompt(q, kv, win, pooled, bg, B, L)
        xp, _ = _finish_layer(xp, ya, yb, mg, w_f32, wq_out, l, norm2_w[l])
        kv_p.append(kv.reshape(B, L, 4, G, hd))
        wk = min(WINDOW, L)
        win_p.append(win.reshape(B, L, 2, G, hd)[:, L - wk:])
        st_p.append(st)
    return (xp.reshape(B, L, D), xs.reshape(Bs, Ls, D), jnp.stack(kv_p), jnp.stack(kv_s),
            jnp.stack(win_p), jnp.stack(win_s), jnp.stack(st_p).astype(state_hgrn.dtype),
            jnp.stack(st_s).astype(state_hgrn.dtype))
```
